```python
import math
import jax
import jax.numpy as jnp
from jax import lax
import numpy as np

D_MODEL = 1024
BATCH = 4
SEQ = 8192
DEPTH = 2

N_EVEN = (DEPTH + 1) // 2
N_ODD = DEPTH // 2
NORM_EPS = 1e-6
NEG_INF = -1e30
FORCED_SCORE = 1e9

NSA_HEADS = 8
NSA_KV_GROUPS = 2
NSA_HEAD_DIM = 64
NSA_Q_PER_GROUP = NSA_HEADS // NSA_KV_GROUPS
CMP_BLOCK = 32
CMP_STRIDE = 16
SLC_BLOCK = 64
SLC_TOPK = 16
WINDOW = 512
Q_BLOCK = 128
ROPE_THETA = 500000.0
ROT_DIM = NSA_HEAD_DIM // 4

SSD_HEADS = 8
SSD_HEAD_DIM = 64
SSD_INNER = SSD_HEADS * SSD_HEAD_DIM
SSD_GROUPS = 2
SSD_STATE = 128
SSD_CONV = 4
SSD_CHUNK = 256
SSD_CONV_DIM = SSD_INNER + 2 * SSD_GROUPS * SSD_STATE

NSA_Q_W = NSA_HEADS * NSA_HEAD_DIM
NSA_KV_W = NSA_KV_GROUPS * NSA_HEAD_DIM
NSA_GATE_W = NSA_HEADS * 3
HY_SPLITS = (NSA_Q_W, 6 * NSA_KV_W, NSA_GATE_W, SSD_INNER, SSD_CONV_DIM, SSD_HEADS)
HY_IN_W = sum(HY_SPLITS)
HY_MIX_W = NSA_Q_W + SSD_INNER

GDN_HEADS = 8
GDN_DK = 128
GDN_DV = 128
GDN_CONV = 4
GDN_CHUNK = 64
GDN_QK_W = GDN_HEADS * GDN_DK
GDN_V_W = GDN_HEADS * GDN_DV
GDN_CONV_DIM = 2 * GDN_QK_W + GDN_V_W
GDN_SPLITS = (GDN_CONV_DIM, GDN_V_W, GDN_HEADS, GDN_HEADS)
GDN_IN_W = sum(GDN_SPLITS)

FFN_DIM = 2816
FFN_CONV = 3

kernel_name = 'hybrid_nsa_ssd_gdn_convffn'


def split_last(t, sizes):
    return jnp.split(t, np.cumsum(sizes)[:-1].tolist(), axis=-1)


def rms_norm(x, g):
    xf = x.astype(jnp.float32)
    y = xf * lax.rsqrt(jnp.mean(xf * xf, axis=-1, keepdims=True) + NORM_EPS)
    return (y * g.astype(jnp.float32)).astype(x.dtype)


def l2_norm(x):
    xf = x.astype(jnp.float32)
    return (xf * lax.rsqrt(jnp.sum(xf * xf, axis=-1, keepdims=True) + NORM_EPS)).astype(x.dtype)


def causal_dwconv(x, w, b=None):
    width = w.shape[0]
    seq = x.shape[1]
    xp = jnp.pad(x, ((0, 0), (width - 1, 0), (0, 0)))
    y = xp[:, 0:seq] * w[0]
    for j in range(1, width):
        y = y + xp[:, j:j + seq] * w[j]
    return y if b is None else y + b


def masked_softmax(s, mask):
    p = jax.nn.softmax(jnp.where(mask, s.astype(jnp.float32), NEG_INF), axis=-1)
    return jnp.where(mask, p, 0.0)


def partial_rope(x, positions):
    inv_freq = ROPE_THETA ** (-jnp.arange(0, ROT_DIM, 2, dtype=jnp.float32) / ROT_DIM)
    ang = positions.astype(jnp.float32)[..., None] * inv_freq
    cos = jnp.cos(ang)[:, :, None, :]
    sin = jnp.sin(ang)[:, :, None, :]
    half = ROT_DIM // 2
    xr = x[..., :ROT_DIM].astype(jnp.float32)
    x1, x2 = xr[..., :half], xr[..., half:]
    rot = jnp.concatenate([x1 * cos - x2 * sin, x2 * cos + x1 * sin], axis=-1).astype(x.dtype)
    return jnp.concatenate([rot, x[..., ROT_DIM:]], axis=-1)


def nsa_attention(q, k_cmp, v_cmp, k_slc, v_slc, k_win, v_win, gates,
                  pe_k, pe_v, ck_w1, ck_b1, ck_w2, cv_w1, cv_b1, cv_w2):
    bsz, seq = q.shape[0], q.shape[1]
    grp, rep, dh = NSA_KV_GROUPS, NSA_Q_PER_GROUP, NSA_HEAD_DIM
    scale = dh ** -0.5
    q = q.reshape(bsz, seq, grp, rep, dh).transpose(0, 2, 3, 1, 4)
    gates = gates.reshape(bsz, seq, grp, rep, 3).transpose(0, 2, 3, 1, 4)
    k_cmp, v_cmp, k_slc, v_slc, k_win, v_win = [
        t.transpose(0, 2, 1, 3) for t in (k_cmp, v_cmp, k_slc, v_slc, k_win, v_win)]

    n_cmp = (seq - CMP_BLOCK) // CMP_STRIDE + 1
    idx = np.arange(n_cmp)[:, None] * CMP_STRIDE + np.arange(CMP_BLOCK)[None, :]
    cmp_end = jnp.asarray(idx[:, -1])

    def compress(t, pe, w1, b1, w2):
        blocks = t[:, :, idx] + pe
        hid = jax.nn.silu(blocks.reshape(bsz, grp, n_cmp, CMP_BLOCK * dh) @ w1 + b1)
        return hid @ w2

    kc = compress(k_cmp, pe_k, ck_w1, ck_b1, ck_w2)
    vc = compress(v_cmp, pe_v, cv_w1, cv_b1, cv_w2)

    n_slc = seq // SLC_BLOCK
    top = min(SLC_TOPK, n_slc)
    cs = np.arange(n_cmp) * CMP_STRIDE
    ss = np.arange(n_slc) * SLC_BLOCK
    overlap = jnp.asarray(((cs[:, None] <= ss[None, :] + SLC_BLOCK - 1)
                           & (cs[:, None] + CMP_BLOCK - 1 >= ss[None, :])).astype(np.float32))
    blk = jnp.arange(n_slc)
    gather = jax.vmap(jax.vmap(lambda kk, ii: kk[ii]))

    kwp = jnp.pad(k_win, ((0, 0), (0, 0), (WINDOW, 0), (0, 0)))
    vwp = jnp.pad(v_win, ((0, 0), (0, 0), (WINDOW, 0), (0, 0)))

    def block_fn(i):
        t0 = i * Q_BLOCK
        tpos = t0 + jnp.arange(Q_BLOCK)
        qb = lax.dynamic_slice_in_dim(q, t0, Q_BLOCK, axis=3)
        gb = lax.dynamic_slice_in_dim(gates, t0, Q_BLOCK, axis=3)
        mask_c = cmp_end[None, :] <= tpos[:, None]
        p_c = masked_softmax(jnp.einsum('bgrqd,bgnd->bgrqn', qb, kc) * scale, mask_c)
        o_c = jnp.einsum('bgrqn,bgnd->bgrqd', p_c.astype(vc.dtype), vc)
        imp = jnp.einsum('bgrqn,nj->bgqj', p_c, overlap).astype(jnp.float32)
        cur = tpos // SLC_BLOCK
        valid = blk[None, :] <= cur[:, None]
        forced = (blk[None, :] == 0) | (blk[None, :] == cur[:, None]) | (blk[None, :] == cur[:, None] - 1)
        imp = jnp.where(forced, FORCED_SCORE, jnp.where(valid, imp, -1.0))
        _, sel = lax.top_k(imp, top)
        tok = (sel[..., None] * SLC_BLOCK + jnp.arange(SLC_BLOCK)).reshape(
            bsz, grp, Q_BLOCK, top * SLC_BLOCK)
        mask_s = (tok <= tpos[:, None])[:, :, None]
        ks = gather(k_slc, tok)
        vs = gather(v_slc, tok)
        p_s = masked_softmax(jnp.einsum('bgrqd,bgqtd->bgrqt', qb, ks) * scale, mask_s)
        o_s = jnp.einsum('bgrqt,bgqtd->bgrqd', p_s.astype(vs.dtype), vs)
        kw = lax.dynamic_slice_in_dim(kwp, t0, WINDOW + Q_BLOCK, axis=2)
        vw = lax.dynamic_slice_in_dim(vwp, t0, WINDOW + Q_BLOCK, axis=2)
        kpos = t0 - WINDOW + jnp.arange(WINDOW + Q_BLOCK)
        dlt = tpos[:, None] - kpos[None, :]
        mask_w = (kpos[None, :] >= 0) & (dlt >= 0) & (dlt < WINDOW)
        p_w = masked_softmax(jnp.einsum('bgrqd,bgkd->bgrqk', qb, kw) * scale, mask_w)
        o_w = jnp.einsum('bgrqk,bgkd->bgrqd', p_w.astype(vw.dtype), vw)
        return gb[..., 0:1] * o_c + gb[..., 1:2] * o_s + gb[..., 2:3] * o_w

    out = lax.map(block_fn, jnp.arange(seq // Q_BLOCK))
    return out.transpose(1, 0, 4, 2, 3, 5).reshape(bsz, seq, NSA_HEADS * dh)


def ssd_chunked(x, dt, a_head, b_in, c_in):
    in_dtype = x.dtype
    f32 = jnp.float32
    bsz, seq, nh, hp = x.shape
    ng, ns = b_in.shape[2], b_in.shape[3]
    cl = math.gcd(SSD_CHUNK, seq)
    nc = seq // cl
    xdt = (x.astype(f32) * dt.astype(f32)[..., None]).reshape(bsz, nc, cl, nh, hp)
    a = (dt.astype(f32) * a_head.astype(f32)).reshape(bsz, nc, cl, nh).transpose(0, 3, 1, 2)
    bh = jnp.repeat(b_in.astype(f32), nh // ng, axis=2).reshape(bsz, nc, cl, nh, ns)
    ch = jnp.repeat(c_in.astype(f32), nh // ng, axis=2).reshape(bsz, nc, cl, nh, ns)
    a_cs = jnp.cumsum(a, axis=-1)
    causal = np.tril(np.ones((cl, cl), dtype=bool))
    decay_in = jnp.exp(jnp.where(causal, a_cs[..., :, None] - a_cs[..., None, :], -jnp.inf))
    cb = jnp.einsum('bclhn,bcshn->bhcls', ch, bh) * decay_in
    y_diag = jnp.einsum('bhcls,bcshp->bclhp', cb, xdt)
    states = jnp.einsum('bclhn,bhcl,bclhp->bchpn', bh, jnp.exp(a_cs[..., -1:] - a_cs), xdt)
    chunk_decay = jnp.exp(a_cs[..., -1])

    def step(hs, inp):
        st, dec = inp
        return hs * dec[..., None, None] + st, hs

    _, prev = lax.scan(step, jnp.zeros((bsz, nh, hp, ns), f32),
                       (states.transpose(1, 0, 2, 3, 4), chunk_decay.transpose(2, 0, 1)))
    prev = prev.transpose(1, 0, 2, 3, 4)
    y_off = jnp.einsum('bclhn,bchpn,bhcl->bclhp', ch, prev, jnp.exp(a_cs))
    return (y_diag + y_off).reshape(bsz, seq, nh, hp).astype(in_dtype)


def gated_delta_chunked(q, k, v, g, beta):
    in_dtype = v.dtype
    f32 = jnp.float32
    bsz, seq, nh, dk = q.shape
    dv = v.shape[-1]
    cl = GDN_CHUNK
    nc = seq // cl

    def chunks(t):
        return t.astype(f32).reshape(bsz, nc, cl, nh, -1).transpose(0, 3, 1, 2, 4)

    q = chunks(q) * dk ** -0.5
    k = chunks(k)
    v = chunks(v)
    beta = chunks(beta[..., None])[..., 0]
    g_cs = jnp.cumsum(chunks(g[..., None])[..., 0], axis=-1)
    incl = np.tril(np.ones((cl, cl), dtype=bool))
    strict = np.tril(np.ones((cl, cl), dtype=bool), -1)
    decay = jnp.exp(jnp.where(incl, g_cs[..., :, None] - g_cs[..., None, :], -jnp.inf))
    kb = k * beta[..., None]
    a_kk = jnp.where(strict, jnp.einsum('bhncd,bhnsd->bhncs', kb, k) * decay, 0.0)
    eye = jnp.eye(cl, dtype=f32)
    t_inv = lax.linalg.triangular_solve(a_kk + eye, jnp.broadcast_to(eye, a_kk.shape),
                                        left_side=True, lower=True, unit_diagonal=True)
    u = jnp.einsum('bhncs,bhnsd->bhncd', t_inv, v * beta[..., None])
    w = jnp.einsum('bhncs,bhnsd->bhncd', t_inv, kb * jnp.exp(g_cs)[..., None])
    a_qk = jnp.einsum('bhncd,bhnsd->bhncs', q, k) * decay

    def step(state, inp):
        q_i, k_i, u_i, w_i, a_i, g_i = inp
        v_new = u_i - jnp.einsum('bhcd,bhde->bhce', w_i, state)
        o = (jnp.einsum('bhcd,bhde->bhce', q_i * jnp.exp(g_i)[..., None], state)
             + jnp.einsum('bhcs,bhse->bhce', a_i, v_new))
        g_last = g_i[..., -1:]
        state = (state * jnp.exp(g_last)[..., None]
                 + jnp.einsum('bhcd,bhce->bhde', k_i * jnp.exp(g_last - g_i)[..., None], v_new))
        return state, o

    xs = tuple(jnp.moveaxis(t, 2, 0) for t in (q, k, u, w, a_qk, g_cs))
    _, o = lax.scan(step, jnp.zeros((bsz, nh, dk, dv), f32), xs)
    return o.transpose(1, 0, 3, 2, 4).reshape(bsz, seq, nh, dv).astype(in_dtype)


def hybrid_mixer(h, positions, w_in, w_out, pe_k, pe_v, ck_w1, ck_b1, ck_w2, cv_w1, cv_b1, cv_w2,
                 conv_w, conv_b, dt_bias, a_log, d_skip, norm_g):
    bsz, seq, _ = h.shape
    proj = h @ w_in
    q, kv, gate_logits, z, xbc, dt = split_last(proj, HY_SPLITS)
    q = partial_rope(q.reshape(bsz, seq, NSA_HEADS, NSA_HEAD_DIM), positions)
    kv = kv.reshape(bsz, seq, 6, NSA_KV_GROUPS, NSA_HEAD_DIM)
    k_cmp = partial_rope(kv[:, :, 0], positions)
    v_cmp = kv[:, :, 1]
    k_slc = partial_rope(kv[:, :, 2], positions)
    v_slc = kv[:, :, 3]
    k_win = partial_rope(kv[:, :, 4], positions)
    v_win = kv[:, :, 5]
    gates = jax.nn.sigmoid(gate_logits.reshape(bsz, seq, NSA_HEADS, 3))
    o_nsa = nsa_attention(q, k_cmp, v_cmp, k_slc, v_slc, k_win, v_win, gates,
                          pe_k, pe_v, ck_w1, ck_b1, ck_w2, cv_w1, cv_b1, cv_w2)
    xbc = jax.nn.silu(causal_dwconv(xbc, conv_w, conv_b))
    xs, bs, cs = split_last(xbc, (SSD_INNER, SSD_GROUPS * SSD_STATE, SSD_GROUPS * SSD_STATE))
    xs = xs.reshape(bsz, seq, SSD_HEADS, SSD_HEAD_DIM)
    dt = jax.nn.softplus(dt + dt_bias)
    y = ssd_chunked(xs, dt, -jnp.exp(a_log),
                    bs.reshape(bsz, seq, SSD_GROUPS, SSD_STATE),
                    cs.reshape(bsz, seq, SSD_GROUPS, SSD_STATE)) + d_skip[:, None] * xs
    y = y.reshape(bsz, seq, SSD_INNER) * jax.nn.silu(z)
    y = rms_norm(y.reshape(bsz, seq, SSD_GROUPS, SSD_INNER // SSD_GROUPS),
                 norm_g.reshape(SSD_GROUPS, SSD_INNER // SSD_GROUPS)).reshape(bsz, seq, SSD_INNER)
    return jnp.concatenate([o_nsa, y], axis=-1) @ w_out


def gdn_mixer(h, w_in, conv_w, dt_bias, a_log, norm_g, w_out):
    bsz, seq, _ = h.shape
    qkv, z, b, a = split_last(h @ w_in, GDN_SPLITS)
    qkv = jax.nn.silu(causal_dwconv(qkv, conv_w))
    q, k, v = split_last(qkv, (GDN_QK_W, GDN_QK_W, GDN_V_W))
    q = l2_norm(q.reshape(bsz, seq, GDN_HEADS, GDN_DK))
    k = l2_norm(k.reshape(bsz, seq, GDN_HEADS, GDN_DK))
    v = v.reshape(bsz, seq, GDN_HEADS, GDN_DV)
    beta = jax.nn.sigmoid(b)
    g = -jnp.exp(a_log) * jax.nn.softplus(a + dt_bias)
    o = gated_delta_chunked(q, k, v, g, beta)
    o = rms_norm(o, norm_g) * jax.nn.silu(z.reshape(bsz, seq, GDN_HEADS, GDN_DV))
    return o.reshape(bsz, seq, GDN_V_W) @ w_out


def conv_ffn(h, w_up, conv_w, conv_b, w_down):
    u = causal_dwconv(h @ w_up, conv_w, conv_b)
    gate, val = jnp.split(u, 2, axis=-1)
    return (jax.nn.silu(gate) * val) @ w_down


def setup_inputs(seed: int = 0) -> dict:
    key = jax.random.key(seed)
    keys = iter(jax.random.split(key, 40))
    f32 = jnp.float32

    def nrm(shape, scale):
        return jax.random.normal(next(keys), shape, f32) * scale

    def gain(shape):
        return 1.0 + nrm(shape, 0.02)

    def dt_bias(shape):
        u = jax.random.uniform(next(keys), shape, f32)
        dt = jnp.exp(u * (math.log(0.1) - math.log(1e-3)) + math.log(1e-3))
        return dt + jnp.log(-jnp.expm1(-dt))

    def a_log(shape):
        return jnp.log(jax.random.uniform(next(keys), shape, f32, 1.0, 16.0))

    E, O, L = N_EVEN, N_ODD, DEPTH
    dh = NSA_HEAD_DIM
    x = jax.random.normal(next(keys), (BATCH, SEQ, D_MODEL), f32)
    offs = jax.random.randint(next(keys), (BATCH, 1), 0, 1024, dtype=jnp.int32)
    positions = (offs + jnp.arange(SEQ, dtype=jnp.int32)[None, :]).astype(jnp.int32)
    return {
        'x': x,
        'positions': positions,
        'norm_mix_pre': gain((L, D_MODEL)),
        'norm_mix_post': gain((L, D_MODEL)),
        'norm_ffn_pre': gain((L, D_MODEL)),
        'norm_ffn_post': gain((L, D_MODEL)),
        'hy_w_in': nrm((E, D_MODEL, HY_IN_W), D_MODEL ** -0.5),
        'hy_w_out': nrm((E, HY_MIX_W, D_MODEL), HY_MIX_W ** -0.5),
        'nsa_pe_k': nrm((E, CMP_BLOCK, dh), 0.02),
        'nsa_pe_v': nrm((E, CMP_BLOCK, dh), 0.02),
        'nsa_ck_w1': nrm((E, CMP_BLOCK * dh, dh), (CMP_BLOCK * dh) ** -0.5),
        'nsa_ck_b1': nrm((E, dh), 0.01),
        'nsa_ck_w2': nrm((E, dh, dh), dh ** -0.5),
        'nsa_cv_w1': nrm((E, CMP_BLOCK * dh, dh), (CMP_BLOCK * dh) ** -0.5),
        'nsa_cv_b1': nrm((E, dh), 0.01),
        'nsa_cv_w2': nrm((E, dh, dh), dh ** -0.5),
        'ssd_conv_w': nrm((E, SSD_CONV, SSD_CONV_DIM), SSD_CONV ** -0.5),
        'ssd_conv_b': nrm((E, SSD_CONV_DIM), 0.01),
        'ssd_dt_bias': dt_bias((E, SSD_HEADS)),
        'ssd_a_log': a_log((E, SSD_HEADS)),
        'ssd_d': gain((E, SSD_HEADS)),
        'ssd_norm': gain((E, SSD_INNER)),
        'gdn_w_in': nrm((O, D_MODEL, GDN_IN_W), D_MODEL ** -0.5),
        'gdn_conv_w': nrm((O, GDN_CONV, GDN_CONV_DIM), GDN_CONV ** -0.5),
        'gdn_dt_bias': dt_bias((O, GDN_HEADS)),
        'gdn_a_log': a_log((O, GDN_HEADS)),
        'gdn_norm': gain((O, GDN_DV)),
        'gdn_w_out': nrm((O, GDN_V_W, D_MODEL), GDN_V_W ** -0.5),
        'ffn_w_up': nrm((L, D_MODEL, 2 * FFN_DIM), D_MODEL ** -0.5),
        'ffn_conv_w': nrm((L, FFN_CONV, 2 * FFN_DIM), FFN_CONV ** -0.5),
        'ffn_conv_b': nrm((L, 2 * FFN_DIM), 0.01),
        'ffn_w_down': nrm((L, FFN_DIM, D_MODEL), FFN_DIM ** -0.5),
    }


def reference(x, positions, norm_mix_pre, norm_mix_post, norm_ffn_pre, norm_ffn_post,
              hy_w_in, hy_w_out, nsa_pe_k, nsa_pe_v, nsa_ck_w1, nsa_ck_b1, nsa_ck_w2,
              nsa_cv_w1, nsa_cv_b1, nsa_cv_w2, ssd_conv_w, ssd_conv_b, ssd_dt_bias, ssd_a_log,
              ssd_d, ssd_norm, gdn_w_in, gdn_conv_w, gdn_dt_bias, gdn_a_log, gdn_norm, gdn_w_out,
              ffn_w_up, ffn_conv_w, ffn_conv_b, ffn_w_down):
    for layer in range(DEPTH):
        e = layer // 2
        hn = rms_norm(x, norm_mix_pre[layer])
        if layer % 2 == 0:
            mix = hybrid_mixer(hn, positions, hy_w_in[e], hy_w_out[e], nsa_pe_k[e], nsa_pe_v[e],
                               nsa_ck_w1[e], nsa_ck_b1[e], nsa_ck_w2[e], nsa_cv_w1[e], nsa_cv_b1[e],
                               nsa_cv_w2[e], ssd_conv_w[e], ssd_conv_b[e], ssd_dt_bias[e],
                               ssd_a_log[e], ssd_d[e], ssd_norm[e])
        else:
            mix = gdn_mixer(hn, gdn_w_in[e], gdn_conv_w[e], gdn_dt_bias[e], gdn_a_log[e],
                            gdn_norm[e], gdn_w_out[e])
        x = x + rms_norm(mix, norm_mix_post[layer])
        hn = rms_norm(x, norm_ffn_pre[layer])
        f = conv_ffn(hn, ffn_w_up[layer], ffn_conv_w[layer], ffn_conv_b[layer], ffn_w_down[layer])
        x = x + rms_norm(f, norm_ffn_post[layer])
    return x
```

```python
import functools
import math

import numpy as np
import jax
import jax.numpy as jnp
from jax import lax
from jax.experimental import pallas as pl
from jax.experimental.pallas import tpu as pltpu

F32 = jnp.float32
BF16 = jnp.bfloat16

D_MODEL = 1024
NORM_EPS = 1e-6
MASKED = -1e30

NSA_HEADS = 8
NSA_GROUPS = 2
NSA_REP = NSA_HEADS // NSA_GROUPS
NSA_DH = 64
CMP_BLOCK = 32
CMP_STRIDE = 16
SLC_BLOCK = 64
SLC_TOPK = 16
WINDOW = 512
ROPE_THETA = 500000.0
ROT_DIM = NSA_DH // 4
FORCED_SCORE = 1e9

SSD_HEADS = 8
SSD_P = 64
SSD_INNER = SSD_HEADS * SSD_P
SSD_GROUPS = 2
SSD_N = 128
SSD_CHUNK = 256
SSD_CONV_DIM = SSD_INNER + 2 * SSD_GROUPS * SSD_N

GDN_HEADS = 8
GDN_DK = 128
GDN_DV = 128
GDN_CHUNK = 64
GDN_QK_W = GDN_HEADS * GDN_DK
GDN_V_W = GDN_HEADS * GDN_DV
GDN_CONV_DIM = 2 * GDN_QK_W + GDN_V_W

FFN_DIM = 2816

LANE = 128
CONV_HALO = 8
FFN_HALO = 16
VMEM_LIMIT = 56 * 1024 * 1024


def _cparams(sem):
    return pltpu.CompilerParams(dimension_semantics=sem, vmem_limit_bytes=VMEM_LIMIT)


def _rms(x, g):
    return x * lax.rsqrt(jnp.mean(x * x, axis=-1, keepdims=True) + NORM_EPS) * g


def _silu(x):
    return x * jax.nn.sigmoid(x)


def _softplus(x):
    return jnp.maximum(x, 0.0) + jnp.log1p(jnp.exp(-jnp.abs(x)))


def _dot(a, b):
    return jnp.dot(a, b, preferred_element_type=F32)


def _dot_nt(a, b):
    return lax.dot_general(a, b, (((1,), (1,)), ((), ())), preferred_element_type=F32)


def _dot_tn(a, b):
    return lax.dot_general(a, b, (((0,), (0,)), ((), ())), preferred_element_type=F32)


def _split3(x):
    x1 = x.astype(BF16)
    r1 = x - x1.astype(F32)
    x2 = r1.astype(BF16)
    x3 = (r1 - x2.astype(F32)).astype(BF16)
    return x1, x2, x3


def _split2(x):
    x1 = x.astype(BF16)
    x2 = (x - x1.astype(F32)).astype(BF16)
    return x1, x2


def _cumsum_cols(tril, a):
    a1, a2, a3 = _split3(a)
    return _dot(tril, a1) + _dot(tril, a2) + _dot(tril, a3)


def _cumsum_rows_t(a, triu):
    a1, a2, a3 = _split3(a)
    return _dot_tn(a1, triu) + _dot_tn(a2, triu) + _dot_tn(a3, triu)


def _dot_f32(a, b):
    a1, a2 = _split2(a)
    b1, b2 = _split2(b)
    return _dot(a1, b1) + _dot(a1, b2) + _dot(a2, b1)


def _causal_conv(ext, cw, halo):
    width = cw.shape[0]
    y = ext[halo:] * cw[width - 1:width]
    for k in range(1, width):
        y = y + pltpu.roll(ext, k, 0)[halo:] * cw[width - 1 - k:width - k]
    return y


def _inproj_kernel(x_ref, g_ref, w_ref, *out_refs, splits):
    hn = _rms(x_ref[...], g_ref[...]).astype(BF16)
    y = _dot(hn, w_ref[...])
    off = 0
    for o_ref, n in zip(out_refs, splits):
        o_ref[...] = y[:, off:off + n].astype(o_ref.dtype)
        off += n


def norm_matmul(x2d, g, w, splits, tm=512):
    t = x2d.shape[0]
    n = w.shape[1]
    assert sum(splits) == n and t % tm == 0
    return pl.pallas_call(
        functools.partial(_inproj_kernel, splits=splits),
        grid=(t // tm,),
        in_specs=[pl.BlockSpec((tm, D_MODEL), lambda i: (i, 0)),
                  pl.BlockSpec((1, D_MODEL), lambda i: (0, 0)),
                  pl.BlockSpec((D_MODEL, n), lambda i: (0, 0))],
        out_specs=[pl.BlockSpec((tm, s), lambda i: (i, 0)) for s in splits],
        out_shape=[jax.ShapeDtypeStruct((t, s), F32) for s in splits],
        compiler_params=_cparams(("parallel",)),
        name="norm_matmul",
    )(x2d, g.reshape(1, D_MODEL), w)


def _outproj_kernel(*refs, n_in):
    a_refs = refs[:n_in]
    w_ref, x_ref, g_ref, o_ref = refs[n_in:]
    acc = None
    off = 0
    for a_ref in a_refs:
        k = a_ref.shape[-1]
        part = _dot(a_ref[...], w_ref[off:off + k, :])
        acc = part if acc is None else acc + part
        off += k
    o_ref[...] = x_ref[...] + _rms(acc, g_ref[...])


def outproj_residual(parts, w, x2d, g, tm=512):
    t = x2d.shape[0]
    return pl.pallas_call(
        functools.partial(_outproj_kernel, n_in=len(parts)),
        grid=(t // tm,),
        in_specs=[pl.BlockSpec((tm, p.shape[1]), lambda i: (i, 0)) for p in parts]
        + [pl.BlockSpec(w.shape, lambda i: (0, 0)),
           pl.BlockSpec((tm, D_MODEL), lambda i: (i, 0)),
           pl.BlockSpec((1, D_MODEL), lambda i: (0, 0))],
        out_specs=pl.BlockSpec((tm, D_MODEL), lambda i: (i, 0)),
        out_shape=jax.ShapeDtypeStruct((t, D_MODEL), F32),
        compiler_params=_cparams(("parallel",)),
        name="outproj_residual",
    )(*parts, w, x2d, g.reshape(1, D_MODEL))


def _ffn_kernel(x_ref, halo_ref, gpre_ref, wg_ref, wv_ref, cwg_ref, cwv_ref, cbg_ref, cbv_ref, wd_ref,
                gpost_ref, o_ref, hn_ref, acc_ref, *, tm, seq):
    i = pl.program_id(0)
    j = pl.program_id(1)

    @pl.when(j == 0)
    def _():
        keep = ((i * tm) % seq != 0).astype(F32)
        hn_ref[:FFN_HALO, :] = (_rms(halo_ref[...], gpre_ref[...]) * keep).astype(BF16)
        hn_ref[FFN_HALO:, :] = _rms(x_ref[...], gpre_ref[...]).astype(BF16)
        acc_ref[...] = jnp.zeros_like(acc_ref)

    hn = hn_ref[...]

    def branch(w_ref, cw_ref, cb_ref):
        u = _dot(hn, w_ref[...])
        return _causal_conv(u, cw_ref[...], FFN_HALO) + cb_ref[...]

    gate = branch(wg_ref, cwg_ref, cbg_ref)
    val = branch(wv_ref, cwv_ref, cbv_ref)
    h = (_silu(gate) * val).astype(BF16)
    acc_ref[...] += _dot(h, wd_ref[...])

    @pl.when(j == pl.num_programs(1) - 1)
    def _():
        o_ref[...] = x_ref[...] + _rms(acc_ref[...], gpost_ref[...])


def conv_ffn(x2d, seq, gpre, w_up, conv_w, conv_b, w_down, gpost, tm=512, tf=1408):
    t = x2d.shape[0]
    nf = FFN_DIM // tf
    assert FFN_DIM % tf == 0 and t % tm == 0 and seq % tm == 0 and tm % FFN_HALO == 0
    hb = tm // FFN_HALO
    cb2 = conv_b.reshape(1, 2 * FFN_DIM)
    return pl.pallas_call(
        functools.partial(_ffn_kernel, tm=tm, seq=seq),
        grid=(t // tm, nf),
        in_specs=[pl.BlockSpec((tm, D_MODEL), lambda i, j: (i, 0)),
                  pl.BlockSpec((FFN_HALO, D_MODEL), lambda i, j: (jnp.maximum(i * hb - 1, 0), 0)),
                  pl.BlockSpec((1, D_MODEL), lambda i, j: (0, 0)),
                  pl.BlockSpec((D_MODEL, tf), lambda i, j: (0, j)),
                  pl.BlockSpec((D_MODEL, tf), lambda i, j: (0, nf + j)),
                  pl.BlockSpec((3, tf), lambda i, j: (0, j)),
                  pl.BlockSpec((3, tf), lambda i, j: (0, nf + j)),
                  pl.BlockSpec((1, tf), lambda i, j: (0, j)),
                  pl.BlockSpec((1, tf), lambda i, j: (0, nf + j)),
                  pl.BlockSpec((tf, D_MODEL), lambda i, j: (j, 0)),
                  pl.BlockSpec((1, D_MODEL), lambda i, j: (0, 0))],
        out_specs=pl.BlockSpec((tm, D_MODEL), lambda i, j: (i, 0)),
        out_shape=jax.ShapeDtypeStruct((t, D_MODEL), F32),
        scratch_shapes=[pltpu.VMEM((tm + FFN_HALO, D_MODEL), BF16),
                        pltpu.VMEM((tm, D_MODEL), F32)],
        compiler_params=_cparams(("parallel", "arbitrary")),
        name="conv_ffn",
    )(x2d, x2d, gpre.reshape(1, D_MODEL), w_up, w_up, conv_w, conv_w, cb2, cb2, w_down,
      gpost.reshape(1, D_MODEL))


def _nsa_prep_kernel(q_ref, kv_ref, pos_ref, freq_ref, sign_ref,
                     qh_ref, kcmp_ref, vcmp_ref, kslc_ref, vslc_ref, kwin_ref, vwin_ref):
    pos = pos_ref[0].astype(F32)
    ang = pos * freq_ref[...]
    cs = jnp.cos(ang)
    sn = jnp.sin(ang) * sign_ref[...]
    lane = lax.broadcasted_iota(jnp.int32, (1, LANE), 1) % NSA_DH
    first_half = lane < ROT_DIM // 2

    def rope(x):
        partner = jnp.where(first_half, pltpu.roll(x, LANE - ROT_DIM // 2, 1), pltpu.roll(x, ROT_DIM // 2, 1))
        return x * cs + partner * sn

    scale = NSA_DH ** -0.5
    for j in range(NSA_HEADS // 2):
        t = rope(q_ref[0, :, j * LANE:(j + 1) * LANE]) * scale
        qh_ref[0, 2 * j] = t[:, :NSA_DH].astype(BF16)
        qh_ref[0, 2 * j + 1] = t[:, NSA_DH:].astype(BF16)

    outs = (kcmp_ref, vcmp_ref, kslc_ref, vslc_ref, kwin_ref, vwin_ref)
    for i, o_ref in enumerate(outs):
        t = kv_ref[0, :, i * LANE:(i + 1) * LANE]
        if i % 2 == 0:
            t = rope(t)
        for g in range(NSA_GROUPS):
            o_ref[0, g] = t[:, g * NSA_DH:(g + 1) * NSA_DH].astype(o_ref.dtype)


def nsa_prep(q, kv, positions, ts=256):
    b, s, _ = q.shape
    inv_freq = ROPE_THETA ** (-jnp.arange(0, ROT_DIM, 2, dtype=F32) / ROT_DIM)
    head_freq = jnp.concatenate([inv_freq, inv_freq, jnp.zeros((NSA_DH - ROT_DIM,), F32)])
    freq = jnp.tile(head_freq, LANE // NSA_DH).reshape(1, LANE)
    head_sign = np.zeros((NSA_DH,), np.float32)
    head_sign[:ROT_DIM // 2] = -1.0
    head_sign[ROT_DIM // 2:ROT_DIM] = 1.0
    sign = jnp.asarray(np.tile(head_sign, LANE // NSA_DH).reshape(1, LANE))
    kv_shape = (b, NSA_GROUPS, s, NSA_DH)
    kv_spec = pl.BlockSpec((1, NSA_GROUPS, ts, NSA_DH), lambda bi, i: (bi, 0, i, 0))
    return pl.pallas_call(
        _nsa_prep_kernel,
        grid=(b, s // ts),
        in_specs=[pl.BlockSpec((1, ts, q.shape[2]), lambda bi, i: (bi, i, 0)),
                  pl.BlockSpec((1, ts, kv.shape[2]), lambda bi, i: (bi, i, 0)),
                  pl.BlockSpec((1, ts, 1), lambda bi, i: (bi, i, 0)),
                  pl.BlockSpec((1, LANE), lambda bi, i: (0, 0)),
                  pl.BlockSpec((1, LANE), lambda bi, i: (0, 0))],
        out_specs=[pl.BlockSpec((1, NSA_HEADS, ts, NSA_DH), lambda bi, i: (bi, 0, i, 0)),
                   kv_spec, kv_spec, kv_spec, kv_spec, kv_spec, kv_spec],
        out_shape=[jax.ShapeDtypeStruct((b, NSA_HEADS, s, NSA_DH), BF16),
                   jax.ShapeDtypeStruct(kv_shape, F32), jax.ShapeDtypeStruct(kv_shape, F32),
                   jax.ShapeDtypeStruct(kv_shape, BF16), jax.ShapeDtypeStruct(kv_shape, BF16),
                   jax.ShapeDtypeStruct(kv_shape, BF16), jax.ShapeDtypeStruct(kv_shape, BF16)],
        compiler_params=_cparams(("parallel", "parallel")),
        name="nsa_prep",
    )(q, kv, positions.reshape(b, s, 1), freq, sign)


def _compress_kernel(k_ref, v_ref, pek_ref, pev_ref, kw1_ref, vw1_ref, kb1_ref, vb1_ref, kw2_ref, vw2_ref,
                     kc_ref, vc_ref):
    half = CMP_STRIDE * NSA_DH

    def mlp(t_ref, pe_ref, w1_ref, b1_ref, w2_ref, o_ref):
        a = t_ref[0, 0]
        n = a.shape[0]
        h_lo = _dot((a + pe_ref[:, :half]).astype(BF16), w1_ref[:half, :])
        h_hi = _dot((a + pe_ref[:, half:]).astype(BF16), w1_ref[half:, :])
        hid = _silu(h_lo + pltpu.roll(h_hi, n - 1, 0) + b1_ref[...])
        o_ref[0, 0] = _dot(hid.astype(BF16), w2_ref[...]).astype(o_ref.dtype)

    mlp(k_ref, pek_ref, kw1_ref, kb1_ref, kw2_ref, kc_ref)
    mlp(v_ref, pev_ref, vw1_ref, vb1_ref, vw2_ref, vc_ref)


def nsa_compress(kcmp, vcmp, pe_k, pe_v, ck_w1, ck_b1, ck_w2, cv_w1, cv_b1, cv_w2):
    b, g, s, dh = kcmp.shape
    n = s // CMP_STRIDE
    wide = CMP_STRIDE * dh
    assert CMP_BLOCK == 2 * CMP_STRIDE
    kr = kcmp.reshape(b, g, n, wide)
    vr = vcmp.reshape(b, g, n, wide)
    blk = pl.BlockSpec((1, 1, n, wide), lambda bi, gi: (bi, gi, 0, 0))

    def full(a):
        return pl.BlockSpec(a.shape, lambda bi, gi: (0,) * a.ndim)

    args = (pe_k.reshape(1, 2 * wide), pe_v.reshape(1, 2 * wide), ck_w1.astype(BF16), cv_w1.astype(BF16),
            ck_b1.reshape(1, dh), cv_b1.reshape(1, dh), ck_w2.astype(BF16), cv_w2.astype(BF16))
    out_spec = pl.BlockSpec((1, 1, n, dh), lambda bi, gi: (bi, gi, 0, 0))
    return pl.pallas_call(
        _compress_kernel,
        grid=(b, g),
        in_specs=[blk, blk] + [full(a) for a in args],
        out_specs=[out_spec, out_spec],
        out_shape=[jax.ShapeDtypeStruct((b, g, n, dh), BF16)] * 2,
        compiler_params=_cparams(("parallel", "parallel")),
        name="nsa_compress",
    )(kr, vr, *args)


ATT_TQ = 128
ATT_TK = 512


def _att_kernel(q_ref, kc_ref, vc_ref, ks_ref, vs_ref, kw_ref, vw_ref, gate_ref, ov_ref, o_ref, *, seq):
    tq, tk, rep, dh = ATT_TQ, ATT_TK, NSA_REP, NSA_DH
    rows = rep * tq
    nblk = seq // SLC_BLOCK
    ncmp = kc_ref.shape[2]
    t0 = pl.program_id(2) * tq
    q = q_ref[0].reshape(rows, dh)
    tpos = t0 + lax.broadcasted_iota(jnp.int32, (tq, 1), 0)

    s_c = _dot_nt(q, kc_ref[0, 0]).reshape(rep, tq, ncmp)
    cmp_end = lax.broadcasted_iota(jnp.int32, (1, ncmp), 1) * CMP_STRIDE + (CMP_BLOCK - 1)
    vis = (cmp_end <= tpos)[None]
    s_c = jnp.where(vis, s_c, MASKED)
    m_c = jnp.max(s_c, axis=-1, keepdims=True)
    p_c = jnp.where(vis, jnp.exp(s_c - m_c), 0.0)
    l_c = jnp.sum(p_c, axis=-1, keepdims=True)
    p_c = p_c * jnp.where(l_c > 0.0, 1.0 / l_c, 0.0)
    o_c = _dot(p_c.reshape(rows, ncmp).astype(BF16), vc_ref[0, 0])

    p_sum = jnp.sum(p_c, axis=0)
    p_hi, p_lo = _split2(p_sum)
    imp = _dot(p_hi, ov_ref[...]) + _dot(p_lo, ov_ref[...])
    blk = lax.broadcasted_iota(jnp.int32, (1, nblk), 1)
    cur = tpos // SLC_BLOCK
    forced = (blk == 0) | (blk == cur) | (blk == cur - 1)
    imp = jnp.where(forced, FORCED_SCORE, jnp.where(blk <= cur, imp, -1.0))
    sel = jnp.zeros((tq, nblk), F32)
    for _ in range(min(SLC_TOPK, nblk)):
        best = jnp.max(imp, axis=-1, keepdims=True)
        first = jnp.min(jnp.where(imp == best, blk, nblk), axis=-1, keepdims=True)
        hit = blk == first
        sel = jnp.where(hit, 1.0, sel)
        imp = jnp.where(hit, -jnp.inf, imp)
    sel = sel.astype(BF16)

    blocks_per_tile = tk // SLC_BLOCK
    expand = (lax.broadcasted_iota(jnp.int32, (nblk, tk), 0)
              - lax.broadcasted_iota(jnp.int32, (nblk, tk), 1) // SLC_BLOCK)
    klane = lax.broadcasted_iota(jnp.int32, (1, tk), 1)

    def slc_step(kt, carry):
        m, l, acc = carry
        k0 = pl.multiple_of(kt * tk, tk)
        k = ks_ref[0, 0, pl.ds(k0, tk), :]
        v = vs_ref[0, 0, pl.ds(k0, tk), :]
        s = _dot_nt(q, k).reshape(rep, tq, tk)
        e = jnp.where(expand == kt * blocks_per_tile, 1.0, 0.0).astype(BF16)
        chosen = _dot(sel, e)
        ok = jnp.where(k0 + klane <= tpos, chosen, 0.0) > 0.5
        s = jnp.where(ok[None], s, MASKED)
        m_new = jnp.maximum(m, jnp.max(s, axis=-1, keepdims=True))
        alpha = jnp.exp(m - m_new)
        p = jnp.exp(s - m_new)
        l = alpha * l + jnp.sum(p, axis=-1, keepdims=True)
        pv = _dot(p.reshape(rows, tk).astype(BF16), v).reshape(rep, tq, dh)
        return m_new, l, alpha * acc + pv

    n_kt = (t0 + tq + tk - 1) // tk
    init = (jnp.full((rep, tq, 1), MASKED, F32), jnp.zeros((rep, tq, 1), F32), jnp.zeros((rep, tq, dh), F32))
    _, l_s, acc_s = lax.fori_loop(0, n_kt, slc_step, init)
    o_s = acc_s / l_s

    band = WINDOW + tq
    start = pl.multiple_of(jnp.maximum(t0 - WINDOW, 0), tq)
    kw = kw_ref[0, 0, pl.ds(start, band), :]
    vw = vw_ref[0, 0, pl.ds(start, band), :]
    s_w = _dot_nt(q, kw).reshape(rep, tq, band)
    dlt = tpos - (start + lax.broadcasted_iota(jnp.int32, (1, band), 1))
    ok_w = (dlt >= 0) & (dlt < WINDOW)
    s_w = jnp.where(ok_w[None], s_w, MASKED)
    p_w = jnp.exp(s_w - jnp.max(s_w, axis=-1, keepdims=True))
    l_w = jnp.sum(p_w, axis=-1, keepdims=True)
    o_w = _dot(p_w.reshape(rows, band).astype(BF16), vw).reshape(rep, tq, dh) / l_w

    gates = jax.nn.sigmoid(gate_ref[0])
    o_c = o_c.reshape(rep, tq, dh)
    for r in range(rep):
        o_r = (gates[:, 3 * r:3 * r + 1] * o_c[r] + gates[:, 3 * r + 1:3 * r + 2] * o_s[r]
               + gates[:, 3 * r + 2:3 * r + 3] * o_w[r])
        o_ref[0, :, r * dh:(r + 1) * dh] = o_r.astype(o_ref.dtype)


def _overlap_matrix(seq):
    ncp = seq // CMP_STRIDE
    n_cmp = (seq - CMP_BLOCK) // CMP_STRIDE + 1
    nblk = seq // SLC_BLOCK
    cs = np.arange(ncp) * CMP_STRIDE
    ss = np.arange(nblk) * SLC_BLOCK
    ov = ((cs[:, None] <= ss[None, :] + SLC_BLOCK - 1) & (cs[:, None] + CMP_BLOCK - 1 >= ss[None, :])
          & (np.arange(ncp)[:, None] < n_cmp))
    return jnp.asarray(ov.astype(np.float32), dtype=BF16)


def nsa_attention(qh, kc, vc, kslc, vslc, kwin, vwin, gates):
    b, _, s, dh = qh.shape
    g = NSA_GROUPS
    ncp = kc.shape[2]
    nblk = s // SLC_BLOCK
    assert s % ATT_TK == 0 and s >= WINDOW + ATT_TQ
    ov = _overlap_matrix(s)
    seq_spec = pl.BlockSpec((1, 1, s, dh), lambda bi, gi, i: (bi, gi, 0, 0))
    cmp_spec = pl.BlockSpec((1, 1, ncp, dh), lambda bi, gi, i: (bi, gi, 0, 0))
    return pl.pallas_call(
        functools.partial(_att_kernel, seq=s),
        grid=(b, g, s // ATT_TQ),
        in_specs=[pl.BlockSpec((1, NSA_REP, ATT_TQ, dh), lambda bi, gi, i: (bi, gi, i, 0)),
                  cmp_spec, cmp_spec, seq_spec, seq_spec, seq_spec, seq_spec,
                  pl.BlockSpec((1, ATT_TQ, LANE), lambda bi, gi, i: (bi, i, gi)),
                  pl.BlockSpec((ncp, nblk), lambda bi, gi, i: (0, 0))],
        out_specs=pl.BlockSpec((1, ATT_TQ, NSA_REP * dh), lambda bi, gi, i: (bi, i, gi)),
        out_shape=jax.ShapeDtypeStruct((b, s, NSA_HEADS * dh), BF16),
        compiler_params=_cparams(("parallel", "parallel", "arbitrary")),
        name="nsa_attention",
    )(qh, kc, vc, kslc, vslc, kwin, vwin, gates, ov)


def _ssd_kernel(z_ref, xbc_ref, dt_ref, cw_ref, cb_ref, dtb_ref, alog_ref, drow_ref, ng_ref, tril_ref, triu_ref,
                o_ref, carry_ref, state_ref, y_ref):
    cl = SSD_CHUNK
    hpg = SSD_HEADS // SSD_GROUPS

    @pl.when(pl.program_id(1) == 0)
    def _():
        carry_ref[...] = jnp.zeros_like(carry_ref)
        state_ref[...] = jnp.zeros_like(state_ref)

    xbc = xbc_ref[0]
    ext = jnp.concatenate([carry_ref[...], xbc], axis=0)
    act = _silu(_causal_conv(ext, cw_ref[...], CONV_HALO) + cb_ref[...])
    carry_ref[...] = xbc[cl - CONV_HALO:, :]

    xs = act[:, :SSD_INNER]
    dt = _softplus(dt_ref[0] + dtb_ref[...])
    a = dt * (-jnp.exp(alog_ref[...]))
    acs_c = _cumsum_cols(tril_ref[...], a)
    acs_r = _cumsum_rows_t(a, triu_ref[...])
    causal = lax.broadcasted_iota(jnp.int32, (cl, cl), 0) >= lax.broadcasted_iota(jnp.int32, (cl, cl), 1)

    for g in range(SSD_GROUPS):
        bg = act[:, SSD_INNER + g * SSD_N:SSD_INNER + (g + 1) * SSD_N].astype(BF16)
        cg = act[:, SSD_INNER + (SSD_GROUPS + g) * SSD_N:SSD_INNER + (SSD_GROUPS + g + 1) * SSD_N].astype(BF16)
        cb = _dot_nt(cg, bg)
        state = state_ref[g]
        y_off = _dot(cg, state.astype(BF16))
        weighted = []
        decay = []
        for hl in range(hpg):
            h = g * hpg + hl
            col = acs_c[:, h:h + 1]
            row = acs_r[h:h + 1, :]
            lmat = jnp.exp(jnp.where(causal, col - row, MASKED))
            xh = xs[:, h * SSD_P:(h + 1) * SSD_P]
            xdt = xh * dt[:, h:h + 1]
            y_diag = _dot((cb * lmat).astype(BF16), xdt.astype(BF16))
            y_ref[:, h * SSD_P:(h + 1) * SSD_P] = (y_diag + y_off[:, hl * SSD_P:(hl + 1) * SSD_P] * jnp.exp(col)
                                                   + drow_ref[:, h * SSD_P:(h + 1) * SSD_P] * xh)
            a_last = acs_c[cl - 1:cl, h:h + 1]
            weighted.append((xdt * jnp.exp(a_last - col)).astype(BF16))
            decay.append(jnp.broadcast_to(jnp.exp(a_last), (1, SSD_P)))
        contrib = _dot_tn(bg, jnp.concatenate(weighted, axis=1))
        state_ref[g] = state * jnp.concatenate(decay, axis=1) + contrib

    y = y_ref[...] * _silu(z_ref[0])
    gw = SSD_INNER // SSD_GROUPS
    for g in range(SSD_GROUPS):
        o_ref[0, :, g * gw:(g + 1) * gw] = _rms(y[:, g * gw:(g + 1) * gw],
                                                 ng_ref[:, g * gw:(g + 1) * gw]).astype(o_ref.dtype)


def _tri(n, upper):
    m = np.triu(np.ones((n, n), np.float32)) if upper else np.tril(np.ones((n, n), np.float32))
    return jnp.asarray(m, dtype=BF16)


def _lane_row(v, offset=0):
    row = jnp.zeros((1, LANE), F32)
    return row.at[0, offset:offset + v.shape[0]].set(v)


def ssd_mixer(z, xbc, dts, conv_w, conv_b, dt_bias, a_log, d_skip, norm_g):
    b, s, _ = z.shape
    cl = SSD_CHUNK
    hpg = SSD_HEADS // SSD_GROUPS

    def full(a):
        return pl.BlockSpec(a.shape, lambda bi, c: (0,) * a.ndim)

    args = (conv_w, conv_b.reshape(1, -1), _lane_row(dt_bias), _lane_row(a_log),
            jnp.repeat(d_skip, SSD_P).reshape(1, SSD_INNER), norm_g.reshape(1, SSD_INNER),
            _tri(cl, False), _tri(cl, True))
    return pl.pallas_call(
        _ssd_kernel,
        grid=(b, s // cl),
        in_specs=[pl.BlockSpec((1, cl, SSD_INNER), lambda bi, c: (bi, c, 0)),
                  pl.BlockSpec((1, cl, SSD_CONV_DIM), lambda bi, c: (bi, c, 0)),
                  pl.BlockSpec((1, cl, LANE), lambda bi, c: (bi, c, 0))] + [full(a) for a in args],
        out_specs=pl.BlockSpec((1, cl, SSD_INNER), lambda bi, c: (bi, c, 0)),
        out_shape=jax.ShapeDtypeStruct((b, s, SSD_INNER), BF16),
        scratch_shapes=[pltpu.VMEM((CONV_HALO, SSD_CONV_DIM), F32),
                        pltpu.VMEM((SSD_GROUPS, SSD_N, hpg * SSD_P), F32),
                        pltpu.VMEM((cl, SSD_INNER), F32)],
        compiler_params=_cparams(("parallel", "arbitrary")),
        name="ssd_mixer",
    )(z, xbc, dts, *args)


def _unit_lower_inverse(a, row, col):
    eye = jnp.where(row == col, 1.0, 0.0)
    same16 = (row // 16) == (col // 16)
    same32 = (row // 32) == (col // 32)
    n1 = jnp.where(same16, -a, 0.0)
    p = eye + n1
    n2 = _dot_f32(n1, n1)
    p = p + _dot_f32(p, n2)
    n4 = _dot_f32(n2, n2)
    p = p + _dot_f32(p, n4)
    n8 = _dot_f32(n4, n4)
    p = p + _dot_f32(p, n8)
    off1 = jnp.where(same16, 0.0, jnp.where(same32, a, 0.0))
    p = p - _dot_f32(_dot_f32(p, off1), p)
    off2 = jnp.where(same32, 0.0, a)
    p = p - _dot_f32(_dot_f32(p, off2), p)
    return p


def _gdn_kernel(qkv_ref, z_ref, sm_ref, cw_ref, dtb_ref, alog_ref, ng_ref, tril_ref, triu_ref,
                o_ref, carry_ref, state_ref):
    cl = GDN_CHUNK
    assert cl == 64

    @pl.when(pl.program_id(1) == 0)
    def _():
        carry_ref[...] = jnp.zeros_like(carry_ref)
        state_ref[...] = jnp.zeros_like(state_ref)

    qkv = qkv_ref[0]
    ext = jnp.concatenate([carry_ref[...], qkv], axis=0)
    act = _silu(_causal_conv(ext, cw_ref[...], CONV_HALO))
    carry_ref[...] = qkv[cl - CONV_HALO:, :]

    sm = sm_ref[0]
    beta_all = jax.nn.sigmoid(sm)
    g_all = -jnp.exp(alog_ref[...]) * _softplus(sm + dtb_ref[...])
    gcs_c = _cumsum_cols(tril_ref[...], g_all)
    gcs_r = _cumsum_rows_t(g_all, triu_ref[...])
    row = lax.broadcasted_iota(jnp.int32, (cl, cl), 0)
    col = lax.broadcasted_iota(jnp.int32, (cl, cl), 1)
    incl = row >= col

    for h in range(GDN_HEADS):
        q = act[:, h * GDN_DK:(h + 1) * GDN_DK]
        k = act[:, GDN_QK_W + h * GDN_DK:GDN_QK_W + (h + 1) * GDN_DK]
        v = act[:, 2 * GDN_QK_W + h * GDN_DV:2 * GDN_QK_W + (h + 1) * GDN_DV]
        qn = q * lax.rsqrt(jnp.sum(q * q, axis=-1, keepdims=True) + NORM_EPS) * GDN_DK ** -0.5
        kn = k * lax.rsqrt(jnp.sum(k * k, axis=-1, keepdims=True) + NORM_EPS)
        beta = beta_all[:, h:h + 1]
        gc = gcs_c[:, GDN_HEADS + h:GDN_HEADS + h + 1]
        gr = gcs_r[GDN_HEADS + h:GDN_HEADS + h + 1, :]
        dec = jnp.exp(jnp.where(incl, gc - gr, MASKED))
        kb = kn * beta
        knb = kn.astype(BF16)
        a_kk = jnp.where(row > col, _dot_nt(kb.astype(BF16), knb) * dec, 0.0)
        t_inv = _unit_lower_inverse(a_kk, row, col).astype(BF16)
        eg = jnp.exp(gc)
        u = _dot(t_inv, (v * beta).astype(BF16))
        w = _dot(t_inv, (kb * eg).astype(BF16))
        a_qk = _dot_nt(qn.astype(BF16), knb) * dec

        state = state_ref[h]
        sb = state.astype(BF16)
        v_new = u - _dot(w.astype(BF16), sb)
        vb = v_new.astype(BF16)
        o = _dot((qn * eg).astype(BF16), sb) + _dot(a_qk.astype(BF16), vb)
        g_last = gc[cl - 1:cl, :]
        state_ref[h] = state * jnp.exp(g_last) + _dot_tn((kn * jnp.exp(g_last - gc)).astype(BF16), vb)

        o = _rms(o, ng_ref[...]) * _silu(z_ref[0, :, h * GDN_DV:(h + 1) * GDN_DV])
        o_ref[0, :, h * GDN_DV:(h + 1) * GDN_DV] = o.astype(o_ref.dtype)


def gdn_mixer(qkv, z, sm, conv_w, dt_bias, a_log, norm_g):
    b, s, _ = qkv.shape
    cl = GDN_CHUNK

    def full(a):
        return pl.BlockSpec(a.shape, lambda bi, c: (0,) * a.ndim)

    args = (conv_w, _lane_row(dt_bias, GDN_HEADS), _lane_row(a_log, GDN_HEADS), norm_g.reshape(1, GDN_DV),
            _tri(cl, False), _tri(cl, True))
    return pl.pallas_call(
        _gdn_kernel,
        grid=(b, s // cl),
        in_specs=[pl.BlockSpec((1, cl, GDN_CONV_DIM), lambda bi, c: (bi, c, 0)),
                  pl.BlockSpec((1, cl, GDN_V_W), lambda bi, c: (bi, c, 0)),
                  pl.BlockSpec((1, cl, LANE), lambda bi, c: (bi, c, 0))] + [full(a) for a in args],
        out_specs=pl.BlockSpec((1, cl, GDN_V_W), lambda bi, c: (bi, c, 0)),
        out_shape=jax.ShapeDtypeStruct((b, s, GDN_V_W), BF16),
        scratch_shapes=[pltpu.VMEM((CONV_HALO, GDN_CONV_DIM), F32),
                        pltpu.VMEM((GDN_HEADS, GDN_DK, GDN_DV), F32)],
        compiler_params=_cparams(("parallel", "arbitrary")),
        name="gdn_mixer",
    )(qkv, z, sm, *args)


def _pad_cols(w, width=LANE):
    return jnp.pad(w, ((0, 0), (0, width - w.shape[1])))


HY_Q_W = NSA_HEADS * NSA_DH
HY_KV_W = 6 * NSA_GROUPS * NSA_DH
HY_GATE_W = NSA_HEADS * 3


def hybrid_layer(x2d, b, s, positions, g_pre, g_post, w_in, w_out, pe_k, pe_v, ck_w1, ck_b1, ck_w2,
                 cv_w1, cv_b1, cv_w2, conv_w, conv_b, dt_bias, a_log, d_skip, norm_g):
    o = 0
    cols = {}
    for name, width in (("q", HY_Q_W), ("kv", HY_KV_W), ("gate", HY_GATE_W), ("z", SSD_INNER),
                        ("xbc", SSD_CONV_DIM), ("dt", SSD_HEADS)):
        cols[name] = w_in[:, o:o + width]
        o += width
    gpg = HY_GATE_W // NSA_GROUPS
    w_cat = jnp.concatenate(
        [cols["q"], cols["kv"], cols["z"], cols["xbc"]]
        + [_pad_cols(cols["gate"][:, g * gpg:(g + 1) * gpg]) for g in range(NSA_GROUPS)]
        + [_pad_cols(cols["dt"])], axis=1).astype(BF16)
    splits = (HY_Q_W, HY_KV_W, SSD_INNER, SSD_CONV_DIM, NSA_GROUPS * LANE, LANE)
    q, kv, z, xbc, gates, dts = norm_matmul(x2d, g_pre, w_cat, splits)

    qh, kcmp, vcmp, kslc, vslc, kwin, vwin = nsa_prep(q.reshape(b, s, -1), kv.reshape(b, s, -1), positions)
    kc, vc = nsa_compress(kcmp, vcmp, pe_k, pe_v, ck_w1, ck_b1, ck_w2, cv_w1, cv_b1, cv_w2)
    o_nsa = nsa_attention(qh, kc, vc, kslc, vslc, kwin, vwin, gates.reshape(b, s, -1))
    y = ssd_mixer(z.reshape(b, s, -1), xbc.reshape(b, s, -1), dts.reshape(b, s, -1),
                  conv_w, conv_b, dt_bias, a_log, d_skip, norm_g)
    return outproj_residual([o_nsa.reshape(b * s, -1), y.reshape(b * s, -1)], w_out.astype(BF16), x2d, g_post)


def gdn_layer(x2d, b, s, g_pre, g_post, w_in, conv_w, dt_bias, a_log, norm_g, w_out):
    w_cat = jnp.concatenate([w_in[:, :GDN_CONV_DIM + GDN_V_W], _pad_cols(w_in[:, GDN_CONV_DIM + GDN_V_W:])],
                            axis=1).astype(BF16)
    qkv, z, sm = norm_matmul(x2d, g_pre, w_cat, (GDN_CONV_DIM, GDN_V_W, LANE))
    o = gdn_mixer(qkv.reshape(b, s, -1), z.reshape(b, s, -1), sm.reshape(b, s, -1), conv_w, dt_bias, a_log, norm_g)
    return outproj_residual([o.reshape(b * s, -1)], w_out.astype(BF16), x2d, g_post)


def kernel(x, positions, norm_mix_pre, norm_mix_post, norm_ffn_pre, norm_ffn_post, hy_w_in, hy_w_out, nsa_pe_k, nsa_pe_v, nsa_ck_w1, nsa_ck_b1, nsa_ck_w2, nsa_cv_w1, nsa_cv_b1, nsa_cv_w2, ssd_conv_w, ssd_conv_b, ssd_dt_bias, ssd_a_log, ssd_d, ssd_norm, gdn_w_in, gdn_conv_w, gdn_dt_bias, gdn_a_log, gdn_norm, gdn_w_out, ffn_w_up, ffn_conv_w, ffn_conv_b, ffn_w_down):
    b, s, d = x.shape
    x2d = x.reshape(b * s, d)
    depth = norm_mix_pre.shape[0]
    for layer in range(depth):
        e = layer // 2
        if layer % 2 == 0:
            x2d = hybrid_layer(x2d, b, s, positions, norm_mix_pre[layer], norm_mix_post[layer], hy_w_in[e],
                               hy_w_out[e], nsa_pe_k[e], nsa_pe_v[e], nsa_ck_w1[e], nsa_ck_b1[e], nsa_ck_w2[e],
                               nsa_cv_w1[e], nsa_cv_b1[e], nsa_cv_w2[e], ssd_conv_w[e], ssd_conv_b[e],
                               ssd_dt_bias[e], ssd_a_log[e], ssd_d[e], ssd_norm[e])
        else:
            x2d = gdn_layer(x2d, b, s, norm_mix_pre[layer], norm_mix_post[layer], gdn_w_in[e], gdn_conv_w[e],
                            gdn_dt_bias[e], gdn_a_log[e], gdn_norm[e], gdn_w_out[e])
        x2d = conv_ffn(x2d, s, norm_ffn_pre[layer], ffn_w_up[layer].astype(BF16), ffn_conv_w[layer],
                       ffn_conv_b[layer], ffn_w_down[layer].astype(BF16), norm_ffn_post[layer])
    return x2d.reshape(b, s, d)
```

```python
import functools
import math

import numpy as np
import jax
import jax.numpy as jnp
from jax import lax
from jax.experimental import pallas as pl
from jax.experimental.pallas import tpu as pltpu

F32 = jnp.float32
BF16 = jnp.bfloat16

D_MODEL = 1024
NORM_EPS = 1e-6
MASKED = -1e30

NSA_HEADS = 8
NSA_GROUPS = 2
NSA_REP = NSA_HEADS // NSA_GROUPS
NSA_DH = 64
CMP_BLOCK = 32
CMP_STRIDE = 16
SLC_BLOCK = 64
SLC_TOPK = 16
WINDOW = 512
ROPE_THETA = 500000.0
ROT_DIM = NSA_DH // 4
FORCED_SCORE = 1e9

SSD_HEADS = 8
SSD_P = 64
SSD_INNER = SSD_HEADS * SSD_P
SSD_GROUPS = 2
SSD_N = 128
SSD_CHUNK = 256
SSD_CONV_DIM = SSD_INNER + 2 * SSD_GROUPS * SSD_N

GDN_HEADS = 8
GDN_DK = 128
GDN_DV = 128
GDN_CHUNK = 64
GDN_QK_W = GDN_HEADS * GDN_DK
GDN_V_W = GDN_HEADS * GDN_DV
GDN_CONV_DIM = 2 * GDN_QK_W + GDN_V_W

FFN_DIM = 2816

LANE = 128
CONV_HALO = 8
FFN_HALO = 16
VMEM_LIMIT = 56 * 1024 * 1024


def _cparams(sem):
    return pltpu.CompilerParams(dimension_semantics=sem, vmem_limit_bytes=VMEM_LIMIT)


def _rms(x, g):
    return x * lax.rsqrt(jnp.mean(x * x, axis=-1, keepdims=True) + NORM_EPS) * g


def _silu(x):
    return x * jax.nn.sigmoid(x)


def _softplus(x):
    return jnp.maximum(x, 0.0) + jnp.log1p(jnp.exp(-jnp.abs(x)))


def _dot(a, b):
    return jnp.dot(a, b, preferred_element_type=F32)


def _dot_nt(a, b):
    return lax.dot_general(a, b, (((1,), (1,)), ((), ())), preferred_element_type=F32)


def _dot_tn(a, b):
    return lax.dot_general(a, b, (((0,), (0,)), ((), ())), preferred_element_type=F32)


def _split3(x):
    x1 = x.astype(BF16)
    r1 = x - x1.astype(F32)
    x2 = r1.astype(BF16)
    x3 = (r1 - x2.astype(F32)).astype(BF16)
    return x1, x2, x3


def _split2(x):
    x1 = x.astype(BF16)
    x2 = (x - x1.astype(F32)).astype(BF16)
    return x1, x2


def _cumsum_cols(tril, a):
    a1, a2, a3 = _split3(a)
    return _dot(tril, a1) + _dot(tril, a2) + _dot(tril, a3)


def _cumsum_rows_t(a, triu):
    a1, a2, a3 = _split3(a)
    return _dot_tn(a1, triu) + _dot_tn(a2, triu) + _dot_tn(a3, triu)


def _dot_f32(a, b):
    a1, a2 = _split2(a)
    b1, b2 = _split2(b)
    return _dot(a1, b1) + _dot(a1, b2) + _dot(a2, b1)


def _causal_conv(ext, cw, halo):
    width = cw.shape[0]
    y = ext[halo:] * cw[width - 1:width]
    for k in range(1, width):
        y = y + pltpu.roll(ext, k, 0)[halo:] * cw[width - 1 - k:width - k]
    return y


def _inproj_kernel(x_ref, g_ref, w_ref, *out_refs, splits):
    hn = _rms(x_ref[...], g_ref[...]).astype(BF16)
    y = _dot(hn, w_ref[...])
    off = 0
    for o_ref, n in zip(out_refs, splits):
        o_ref[...] = y[:, off:off + n].astype(o_ref.dtype)
        off += n


def norm_matmul(x2d, g, w, splits, tm=512):
    t = x2d.shape[0]
    n = w.shape[1]
    assert sum(splits) == n and t % tm == 0
    return pl.pallas_call(
        functools.partial(_inproj_kernel, splits=splits),
        grid=(t // tm,),
        in_specs=[pl.BlockSpec((tm, D_MODEL), lambda i: (i, 0)),
                  pl.BlockSpec((1, D_MODEL), lambda i: (0, 0)),
                  pl.BlockSpec((D_MODEL, n), lambda i: (0, 0))],
        out_specs=[pl.BlockSpec((tm, s), lambda i: (i, 0)) for s in splits],
        out_shape=[jax.ShapeDtypeStruct((t, s), F32) for s in splits],
        compiler_params=_cparams(("parallel",)),
        name="norm_matmul",
    )(x2d, g.reshape(1, D_MODEL), w)


def _outproj_kernel(*refs, n_in):
    a_refs = refs[:n_in]
    w_ref, x_ref, g_ref, o_ref = refs[n_in:]
    acc = None
    off = 0
    for a_ref in a_refs:
        k = a_ref.shape[-1]
        part = _dot(a_ref[...], w_ref[off:off + k, :])
        acc = part if acc is None else acc + part
        off += k
    o_ref[...] = x_ref[...] + _rms(acc, g_ref[...])


def outproj_residual(parts, w, x2d, g, tm=512):
    t = x2d.shape[0]
    return pl.pallas_call(
        functools.partial(_outproj_kernel, n_in=len(parts)),
        grid=(t // tm,),
        in_specs=[pl.BlockSpec((tm, p.shape[1]), lambda i: (i, 0)) for p in parts]
        + [pl.BlockSpec(w.shape, lambda i: (0, 0)),
           pl.BlockSpec((tm, D_MODEL), lambda i: (i, 0)),
           pl.BlockSpec((1, D_MODEL), lambda i: (0, 0))],
        out_specs=pl.BlockSpec((tm, D_MODEL), lambda i: (i, 0)),
        out_shape=jax.ShapeDtypeStruct((t, D_MODEL), F32),
        compiler_params=_cparams(("parallel",)),
        name="outproj_residual",
    )(*parts, w, x2d, g.reshape(1, D_MODEL))


def _ffn_kernel(x_ref, halo_ref, gpre_ref, wg_ref, wv_ref, cwg_ref, cwv_ref, cbg_ref, cbv_ref, wd_ref,
                gpost_ref, o_ref, hn_ref, acc_ref, *, tm, seq):
    i = pl.program_id(0)
    j = pl.program_id(1)

    @pl.when(j == 0)
    def _():
        keep = ((i * tm) % seq != 0).astype(F32)
        hn_ref[:FFN_HALO, :] = (_rms(halo_ref[...], gpre_ref[...]) * keep).astype(BF16)
        hn_ref[FFN_HALO:, :] = _rms(x_ref[...], gpre_ref[...]).astype(BF16)
        acc_ref[...] = jnp.zeros_like(acc_ref)

    hn = hn_ref[...]

    def branch(w_ref, cw_ref, cb_ref):
        u = _dot(hn, w_ref[...])
        return _causal_conv(u, cw_ref[...], FFN_HALO) + cb_ref[...]

    gate = branch(wg_ref, cwg_ref, cbg_ref)
    val = branch(wv_ref, cwv_ref, cbv_ref)
    h = (_silu(gate) * val).astype(BF16)
    acc_ref[...] += _dot(h, wd_ref[...])

    @pl.when(j == pl.num_programs(1) - 1)
    def _():
        o_ref[...] = x_ref[...] + _rms(acc_ref[...], gpost_ref[...])


def conv_ffn(x2d, seq, gpre, w_up, conv_w, conv_b, w_down, gpost, tm=512, tf=1408):
    t = x2d.shape[0]
    nf = FFN_DIM // tf
    assert FFN_DIM % tf == 0 and t % tm == 0 and seq % tm == 0 and tm % FFN_HALO == 0
    hb = tm // FFN_HALO
    cb2 = conv_b.reshape(1, 2 * FFN_DIM)
    return pl.pallas_call(
        functools.partial(_ffn_kernel, tm=tm, seq=seq),
        grid=(t // tm, nf),
        in_specs=[pl.BlockSpec((tm, D_MODEL), lambda i, j: (i, 0)),
                  pl.BlockSpec((FFN_HALO, D_MODEL), lambda i, j: (jnp.maximum(i * hb - 1, 0), 0)),
                  pl.BlockSpec((1, D_MODEL), lambda i, j: (0, 0)),
                  pl.BlockSpec((D_MODEL, tf), lambda i, j: (0, j)),
                  pl.BlockSpec((D_MODEL, tf), lambda i, j: (0, nf + j)),
                  pl.BlockSpec((3, tf), lambda i, j: (0, j)),
                  pl.BlockSpec((3, tf), lambda i, j: (0, nf + j)),
                  pl.BlockSpec((1, tf), lambda i, j: (0, j)),
                  pl.BlockSpec((1, tf), lambda i, j: (0, nf + j)),
                  pl.BlockSpec((tf, D_MODEL), lambda i, j: (j, 0)),
                  pl.BlockSpec((1, D_MODEL), lambda i, j: (0, 0))],
        out_specs=pl.BlockSpec((tm, D_MODEL), lambda i, j: (i, 0)),
        out_shape=jax.ShapeDtypeStruct((t, D_MODEL), F32),
        scratch_shapes=[pltpu.VMEM((tm + FFN_HALO, D_MODEL), BF16),
                        pltpu.VMEM((tm, D_MODEL), F32)],
        compiler_params=_cparams(("parallel", "arbitrary")),
        name="conv_ffn",
    )(x2d, x2d, gpre.reshape(1, D_MODEL), w_up, w_up, conv_w, conv_w, cb2, cb2, w_down,
      gpost.reshape(1, D_MODEL))


def _nsa_prep_kernel(q_ref, kv_ref, pos_ref, freq_ref, sign_ref,
                     qh_ref, kcmp_ref, vcmp_ref, kslc_ref, vslc_ref, kwin_ref, vwin_ref):
    pos = pos_ref[0].astype(F32)
    ang = pos * freq_ref[...]
    cs = jnp.cos(ang)
    sn = jnp.sin(ang) * sign_ref[...]
    lane = lax.broadcasted_iota(jnp.int32, (1, LANE), 1) % NSA_DH
    first_half = lane < ROT_DIM // 2

    def rope(x):
        partner = jnp.where(first_half, pltpu.roll(x, LANE - ROT_DIM // 2, 1), pltpu.roll(x, ROT_DIM // 2, 1))
        return x * cs + partner * sn

    scale = NSA_DH ** -0.5
    for j in range(NSA_HEADS // 2):
        t = rope(q_ref[0, :, j * LANE:(j + 1) * LANE]) * scale
        qh_ref[0, 2 * j] = t[:, :NSA_DH].astype(BF16)
        qh_ref[0, 2 * j + 1] = t[:, NSA_DH:].astype(BF16)

    outs = (kcmp_ref, vcmp_ref, kslc_ref, vslc_ref, kwin_ref, vwin_ref)
    for i, o_ref in enumerate(outs):
        t = kv_ref[0, :, i * LANE:(i + 1) * LANE]
        if i % 2 == 0:
            t = rope(t)
        for g in range(NSA_GROUPS):
            o_ref[0, g] = t[:, g * NSA_DH:(g + 1) * NSA_DH].astype(o_ref.dtype)


def nsa_prep(q, kv, positions, ts=256):
    b, s, _ = q.shape
    inv_freq = ROPE_THETA ** (-jnp.arange(0, ROT_DIM, 2, dtype=F32) / ROT_DIM)
    head_freq = jnp.concatenate([inv_freq, inv_freq, jnp.zeros((NSA_DH - ROT_DIM,), F32)])
    freq = jnp.tile(head_freq, LANE // NSA_DH).reshape(1, LANE)
    head_sign = np.zeros((NSA_DH,), np.float32)
    head_sign[:ROT_DIM // 2] = -1.0
    head_sign[ROT_DIM // 2:ROT_DIM] = 1.0
    sign = jnp.asarray(np.tile(head_sign, LANE // NSA_DH).reshape(1, LANE))
    kv_shape = (b, NSA_GROUPS, s, NSA_DH)
    kv_spec = pl.BlockSpec((1, NSA_GROUPS, ts, NSA_DH), lambda bi, i: (bi, 0, i, 0))
    return pl.pallas_call(
        _nsa_prep_kernel,
        grid=(b, s // ts),
        in_specs=[pl.BlockSpec((1, ts, q.shape[2]), lambda bi, i: (bi, i, 0)),
                  pl.BlockSpec((1, ts, kv.shape[2]), lambda bi, i: (bi, i, 0)),
                  pl.BlockSpec((1, ts, 1), lambda bi, i: (bi, i, 0)),
                  pl.BlockSpec((1, LANE), lambda bi, i: (0, 0)),
                  pl.BlockSpec((1, LANE), lambda bi, i: (0, 0))],
        out_specs=[pl.BlockSpec((1, NSA_HEADS, ts, NSA_DH), lambda bi, i: (bi, 0, i, 0)),
                   kv_spec, kv_spec, kv_spec, kv_spec, kv_spec, kv_spec],
        out_shape=[jax.ShapeDtypeStruct((b, NSA_HEADS, s, NSA_DH), BF16),
                   jax.ShapeDtypeStruct(kv_shape, F32), jax.ShapeDtypeStruct(kv_shape, F32),
                   jax.ShapeDtypeStruct(kv_shape, BF16), jax.ShapeDtypeStruct(kv_shape, BF16),
                   jax.ShapeDtypeStruct(kv_shape, BF16), jax.ShapeDtypeStruct(kv_shape, BF16)],
        compiler_params=_cparams(("parallel", "parallel")),
        name="nsa_prep",
    )(q, kv, positions.reshape(b, s, 1), freq, sign)


def _compress_kernel(k_ref, v_ref, pek_ref, pev_ref, kw1_ref, vw1_ref, kb1_ref, vb1_ref, kw2_ref, vw2_ref,
                     kc_ref, vc_ref):
    half = CMP_STRIDE * NSA_DH

    def mlp(t_ref, pe_ref, w1_ref, b1_ref, w2_ref, o_ref):
        a = t_ref[0, 0]
        n = a.shape[0]
        h_lo = _dot((a + pe_ref[:, :half]).astype(BF16), w1_ref[:half, :])
        h_hi = _dot((a + pe_ref[:, half:]).astype(BF16), w1_ref[half:, :])
        hid = _silu(h_lo + pltpu.roll(h_hi, n - 1, 0) + b1_ref[...])
        o_ref[0, 0] = _dot(hid.astype(BF16), w2_ref[...]).astype(o_ref.dtype)

    mlp(k_ref, pek_ref, kw1_ref, kb1_ref, kw2_ref, kc_ref)
    mlp(v_ref, pev_ref, vw1_ref, vb1_ref, vw2_ref, vc_ref)


def nsa_compress(kcmp, vcmp, pe_k, pe_v, ck_w1, ck_b1, ck_w2, cv_w1, cv_b1, cv_w2):
    b, g, s, dh = kcmp.shape
    n = s // CMP_STRIDE
    wide = CMP_STRIDE * dh
    assert CMP_BLOCK == 2 * CMP_STRIDE
    kr = kcmp.reshape(b, g, n, wide)
    vr = vcmp.reshape(b, g, n, wide)
    blk = pl.BlockSpec((1, 1, n, wide), lambda bi, gi: (bi, gi, 0, 0))

    def full(a):
        return pl.BlockSpec(a.shape, lambda bi, gi: (0,) * a.ndim)

    args = (pe_k.reshape(1, 2 * wide), pe_v.reshape(1, 2 * wide), ck_w1.astype(BF16), cv_w1.astype(BF16),
            ck_b1.reshape(1, dh), cv_b1.reshape(1, dh), ck_w2.astype(BF16), cv_w2.astype(BF16))
    out_spec = pl.BlockSpec((1, 1, n, dh), lambda bi, gi: (bi, gi, 0, 0))
    return pl.pallas_call(
        _compress_kernel,
        grid=(b, g),
        in_specs=[blk, blk] + [full(a) for a in args],
        out_specs=[out_spec, out_spec],
        out_shape=[jax.ShapeDtypeStruct((b, g, n, dh), BF16)] * 2,
        compiler_params=_cparams(("parallel", "parallel")),
        name="nsa_compress",
    )(kr, vr, *args)


ATT_TQ = 128
ATT_TK = 512


def _att_kernel(q_ref, kc_ref, vc_ref, ks_ref, vs_ref, kw_ref, vw_ref, gate_ref, ov_ref, o_ref, *, seq):
    tq, tk, rep, dh = ATT_TQ, ATT_TK, NSA_REP, NSA_DH
    rows = rep * tq
    nblk = seq // SLC_BLOCK
    ncmp = kc_ref.shape[2]
    t0 = pl.program_id(2) * tq
    q = q_ref[0].reshape(rows, dh)
    tpos = t0 + lax.broadcasted_iota(jnp.int32, (tq, 1), 0)

    s_c = _dot_nt(q, kc_ref[0, 0]).reshape(rep, tq, ncmp)
    cmp_end = lax.broadcasted_iota(jnp.int32, (1, ncmp), 1) * CMP_STRIDE + (CMP_BLOCK - 1)
    vis = (cmp_end <= tpos)[None]
    s_c = jnp.where(vis, s_c, MASKED)
    m_c = jnp.max(s_c, axis=-1, keepdims=True)
    p_c = jnp.where(vis, jnp.exp(s_c - m_c), 0.0)
    l_c = jnp.sum(p_c, axis=-1, keepdims=True)
    p_c = p_c * jnp.where(l_c > 0.0, 1.0 / l_c, 0.0)
    o_c = _dot(p_c.reshape(rows, ncmp).astype(BF16), vc_ref[0, 0])

    p_sum = jnp.sum(p_c, axis=0)
    p_hi, p_lo = _split2(p_sum)
    imp = _dot(p_hi, ov_ref[...]) + _dot(p_lo, ov_ref[...])
    blk = lax.broadcasted_iota(jnp.int32, (1, nblk), 1)
    cur = tpos // SLC_BLOCK
    forced = (blk == 0) | (blk == cur) | (blk == cur - 1)
    imp = jnp.where(forced, FORCED_SCORE, jnp.where(blk <= cur, imp, -1.0))
    sel = jnp.zeros((tq, nblk), F32)
    for _ in range(min(SLC_TOPK, nblk)):
        best = jnp.max(imp, axis=-1, keepdims=True)
        first = jnp.min(jnp.where(imp == best, blk, nblk), axis=-1, keepdims=True)
        hit = blk == first
        sel = jnp.where(hit, 1.0, sel)
        imp = jnp.where(hit, -jnp.inf, imp)
    sel = sel.astype(BF16)

    blocks_per_tile = tk // SLC_BLOCK
    expand = (lax.broadcasted_iota(jnp.int32, (nblk, tk), 0)
              - lax.broadcasted_iota(jnp.int32, (nblk, tk), 1) // SLC_BLOCK)
    klane = lax.broadcasted_iota(jnp.int32, (1, tk), 1)

    def slc_step(kt, carry):
        m, l, acc = carry
        k0 = pl.multiple_of(kt * tk, tk)
        k = ks_ref[0, 0, pl.ds(k0, tk), :]
        v = vs_ref[0, 0, pl.ds(k0, tk), :]
        s = _dot_nt(q, k).reshape(rep, tq, tk)
        e = jnp.where(expand == kt * blocks_per_tile, 1.0, 0.0).astype(BF16)
        chosen = _dot(sel, e)
        ok = jnp.where(k0 + klane <= tpos, chosen, 0.0) > 0.5
        s = jnp.where(ok[None], s, MASKED)
        m_new = jnp.maximum(m, jnp.max(s, axis=-1, keepdims=True))
        alpha = jnp.exp(m - m_new)
        p = jnp.exp(s - m_new)
        l = alpha * l + jnp.sum(p, axis=-1, keepdims=True)
        pv = _dot(p.reshape(rows, tk).astype(BF16), v).reshape(rep, tq, dh)
        return m_new, l, alpha * acc + pv

    n_kt = (t0 + tq + tk - 1) // tk
    init = (jnp.full((rep, tq, 1), MASKED, F32), jnp.zeros((rep, tq, 1), F32), jnp.zeros((rep, tq, dh), F32))
    _, l_s, acc_s = lax.fori_loop(0, n_kt, slc_step, init)
    o_s = acc_s / l_s

    band = WINDOW + tq
    start = pl.multiple_of(jnp.maximum(t0 - WINDOW, 0), tq)
    kw = kw_ref[0, 0, pl.ds(start, band), :]
    vw = vw_ref[0, 0, pl.ds(start, band), :]
    s_w = _dot_nt(q, kw).reshape(rep, tq, band)
    dlt = tpos - (start + lax.broadcasted_iota(jnp.int32, (1, band), 1))
    ok_w = (dlt >= 0) & (dlt < WINDOW)
    s_w = jnp.where(ok_w[None], s_w, MASKED)
    p_w = jnp.exp(s_w - jnp.max(s_w, axis=-1, keepdims=True))
    l_w = jnp.sum(p_w, axis=-1, keepdims=True)
    o_w = _dot(p_w.reshape(rows, band).astype(BF16), vw).reshape(rep, tq, dh) / l_w

    gates = jax.nn.sigmoid(gate_ref[0])
    o_c = o_c.reshape(rep, tq, dh)
    for r in range(rep):
        o_r = (gates[:, 3 * r:3 * r + 1] * o_c[r] + gates[:, 3 * r + 1:3 * r + 2] * o_s[r]
               + gates[:, 3 * r + 2:3 * r + 3] * o_w[r])
        o_ref[0, :, r * dh:(r + 1) * dh] = o_r.astype(o_ref.dtype)


def _overlap_matrix(seq):
    ncp = seq // CMP_STRIDE
    n_cmp = (seq - CMP_BLOCK) // CMP_STRIDE + 1
    nblk = seq // SLC_BLOCK
    cs = np.arange(ncp) * CMP_STRIDE
    ss = np.arange(nblk) * SLC_BLOCK
    ov = ((cs[:, None] <= ss[None, :] + SLC_BLOCK - 1) & (cs[:, None] + CMP_BLOCK - 1 >= ss[None, :])
          & (np.arange(ncp)[:, None] < n_cmp))
    return jnp.asarray(ov.astype(np.float32), dtype=BF16)


def nsa_attention(qh, kc, vc, kslc, vslc, kwin, vwin, gates):
    b, _, s, dh = qh.shape
    g = NSA_GROUPS
    ncp = kc.shape[2]
    nblk = s // SLC_BLOCK
    assert s % ATT_TK == 0 and s >= WINDOW + ATT_TQ
    ov = _overlap_matrix(s)
    seq_spec = pl.BlockSpec((1, 1, s, dh), lambda bi, gi, i: (bi, gi, 0, 0))
    cmp_spec = pl.BlockSpec((1, 1, ncp, dh), lambda bi, gi, i: (bi, gi, 0, 0))
    return pl.pallas_call(
        functools.partial(_att_kernel, seq=s),
        grid=(b, g, s // ATT_TQ),
        in_specs=[pl.BlockSpec((1, NSA_REP, ATT_TQ, dh), lambda bi, gi, i: (bi, gi, i, 0)),
                  cmp_spec, cmp_spec, seq_spec, seq_spec, seq_spec, seq_spec,
                  pl.BlockSpec((1, ATT_TQ, LANE), lambda bi, gi, i: (bi, i, gi)),
                  pl.BlockSpec((ncp, nblk), lambda bi, gi, i: (0, 0))],
        out_specs=pl.BlockSpec((1, ATT_TQ, NSA_REP * dh), lambda bi, gi, i: (bi, i, gi)),
        out_shape=jax.ShapeDtypeStruct((b, s, NSA_HEADS * dh), BF16),
        compiler_params=_cparams(("parallel", "parallel", "arbitrary")),
        name="nsa_attention",
    )(qh, kc, vc, kslc, vslc, kwin, vwin, gates, ov)


def _ssd_kernel(z_ref, xbc_ref, dt_ref, cw_ref, cb_ref, dtb_ref, alog_ref, drow_ref, ng_ref, tril_ref, triu_ref,
                o_ref, carry_ref, state_ref, y_ref):
    cl = SSD_CHUNK
    hpg = SSD_HEADS // SSD_GROUPS

    @pl.when(pl.program_id(1) == 0)
    def _():
        carry_ref[...] = jnp.zeros_like(carry_ref)
        state_ref[...] = jnp.zeros_like(state_ref)

    xbc = xbc_ref[0]
    ext = jnp.concatenate([carry_ref[...], xbc], axis=0)
    act = _silu(_causal_conv(ext, cw_ref[...], CONV_HALO) + cb_ref[...])
    carry_ref[...] = xbc[cl - CONV_HALO:, :]

    xs = act[:, :SSD_INNER]
    dt = _softplus(dt_ref[0] + dtb_ref[...])
    a = dt * (-jnp.exp(alog_ref[...]))
    acs_c = _cumsum_cols(tril_ref[...], a)
    acs_r = _cumsum_rows_t(a, triu_ref[...])
    causal = lax.broadcasted_iota(jnp.int32, (cl, cl), 0) >= lax.broadcasted_iota(jnp.int32, (cl, cl), 1)

    for g in range(SSD_GROUPS):
        bg = act[:, SSD_INNER + g * SSD_N:SSD_INNER + (g + 1) * SSD_N].astype(BF16)
        cg = act[:, SSD_INNER + (SSD_GROUPS + g) * SSD_N:SSD_INNER + (SSD_GROUPS + g + 1) * SSD_N].astype(BF16)
        cb = _dot_nt(cg, bg)
        state = state_ref[g]
        y_off = _dot(cg, state.astype(BF16))
        weighted = []
        decay = []
        for hl in range(hpg):
            h = g * hpg + hl
            col = acs_c[:, h:h + 1]
            row = acs_r[h:h + 1, :]
            lmat = jnp.exp(jnp.where(causal, col - row, MASKED))
            xh = xs[:, h * SSD_P:(h + 1) * SSD_P]
            xdt = xh * dt[:, h:h + 1]
            y_diag = _dot((cb * lmat).astype(BF16), xdt.astype(BF16))
            y_ref[:, h * SSD_P:(h + 1) * SSD_P] = (y_diag + y_off[:, hl * SSD_P:(hl + 1) * SSD_P] * jnp.exp(col)
                                                   + drow_ref[:, h * SSD_P:(h + 1) * SSD_P] * xh)
            a_last = acs_c[cl - 1:cl, h:h + 1]
            weighted.append((xdt * jnp.exp(a_last - col)).astype(BF16))
            decay.append(jnp.broadcast_to(jnp.exp(a_last), (1, SSD_P)))
        contrib = _dot_tn(bg, jnp.concatenate(weighted, axis=1))
        state_ref[g] = state * jnp.concatenate(decay, axis=1) + contrib

    y = y_ref[...] * _silu(z_ref[0])
    gw = SSD_INNER // SSD_GROUPS
    for g in range(SSD_GROUPS):
        o_ref[0, :, g * gw:(g + 1) * gw] = _rms(y[:, g * gw:(g + 1) * gw],
                                                 ng_ref[:, g * gw:(g + 1) * gw]).astype(o_ref.dtype)


def _tri(n, upper):
    m = np.triu(np.ones((n, n), np.float32)) if upper else np.tril(np.ones((n, n), np.float32))
    return jnp.asarray(m, dtype=BF16)


def _lane_row(v, offset=0):
    row = jnp.zeros((1, LANE), F32)
    return row.at[0, offset:offset + v.shape[0]].set(v)


def ssd_mixer(z, xbc, dts, conv_w, conv_b, dt_bias, a_log, d_skip, norm_g):
    b, s, _ = z.shape
    cl = SSD_CHUNK
    hpg = SSD_HEADS // SSD_GROUPS

    def full(a):
        return pl.BlockSpec(a.shape, lambda bi, c: (0,) * a.ndim)

    args = (conv_w, conv_b.reshape(1, -1), _lane_row(dt_bias), _lane_row(a_log),
            jnp.repeat(d_skip, SSD_P).reshape(1, SSD_INNER), norm_g.reshape(1, SSD_INNER),
            _tri(cl, False), _tri(cl, True))
    return pl.pallas_call(
        _ssd_kernel,
        grid=(b, s // cl),
        in_specs=[pl.BlockSpec((1, cl, SSD_INNER), lambda bi, c: (bi, c, 0)),
                  pl.BlockSpec((1, cl, SSD_CONV_DIM), lambda bi, c: (bi, c, 0)),
                  pl.BlockSpec((1, cl, LANE), lambda bi, c: (bi, c, 0))] + [full(a) for a in args],
        out_specs=pl.BlockSpec((1, cl, SSD_INNER), lambda bi, c: (bi, c, 0)),
        out_shape=jax.ShapeDtypeStruct((b, s, SSD_INNER), BF16),
        scratch_shapes=[pltpu.VMEM((CONV_HALO, SSD_CONV_DIM), F32),
                        pltpu.VMEM((SSD_GROUPS, SSD_N, hpg * SSD_P), F32),
                        pltpu.VMEM((cl, SSD_INNER), F32)],
        compiler_params=_cparams(("parallel", "arbitrary")),
        name="ssd_mixer",
    )(z, xbc, dts, *args)


def _mm_split(a, b):
    return _dot(a[0], b[0]) + _dot(a[0], b[1]) + _dot(a[1], b[0])


def _unit_lower_inverses(a_list, row, col):
    eye = jnp.where(row == col, 1.0, 0.0)
    same16 = (row // 16) == (col // 16)
    same32 = (row // 32) == (col // 32)
    n = [_split2(jnp.where(same16, -a, 0.0)) for a in a_list]
    p = [eye + jnp.where(same16, -a, 0.0) for a in a_list]
    for _ in range(3):
        n = [_split2(_mm_split(x, x)) for x in n]
        p = [x + _mm_split(_split2(x), y) for x, y in zip(p, n)]
    for level_mask in (jnp.where(same16, 0.0, jnp.where(same32, 1.0, 0.0)), jnp.where(same32, 0.0, 1.0)):
        off = [_split2(a * level_mask) for a in a_list]
        ps = [_split2(x) for x in p]
        t = [_split2(_mm_split(x, y)) for x, y in zip(ps, off)]
        p = [x - _mm_split(y, z) for x, y, z in zip(p, t, ps)]
    return p


def _gdn_kernel(qkv_ref, z_ref, sm_ref, cw_ref, dtb_ref, alog_ref, ng_ref, tril_ref, triu_ref,
                o_ref, carry_ref, state_ref):
    cl = GDN_CHUNK
    heads = range(GDN_HEADS)

    @pl.when(pl.program_id(1) == 0)
    def _():
        carry_ref[...] = jnp.zeros_like(carry_ref)
        state_ref[...] = jnp.zeros_like(state_ref)

    qkv = qkv_ref[0]
    ext = jnp.concatenate([carry_ref[...], qkv], axis=0)
    act = _silu(_causal_conv(ext, cw_ref[...], CONV_HALO))
    carry_ref[...] = qkv[cl - CONV_HALO:, :]

    sm = sm_ref[0]
    beta_all = jax.nn.sigmoid(sm)
    g_all = -jnp.exp(alog_ref[...]) * _softplus(sm + dtb_ref[...])
    gcs_c = _cumsum_cols(tril_ref[...], g_all)
    gcs_r = _cumsum_rows_t(g_all, triu_ref[...])
    row = lax.broadcasted_iota(jnp.int32, (cl, cl), 0)
    col = lax.broadcasted_iota(jnp.int32, (cl, cl), 1)
    incl = row >= col

    def l2n(x):
        return x * lax.rsqrt(jnp.sum(x * x, axis=-1, keepdims=True) + NORM_EPS)

    qn = [l2n(act[:, h * GDN_DK:(h + 1) * GDN_DK]) * GDN_DK ** -0.5 for h in heads]
    kn = [l2n(act[:, GDN_QK_W + h * GDN_DK:GDN_QK_W + (h + 1) * GDN_DK]) for h in heads]
    v = [act[:, 2 * GDN_QK_W + h * GDN_DV:2 * GDN_QK_W + (h + 1) * GDN_DV] for h in heads]
    beta = [beta_all[:, h:h + 1] for h in heads]
    gc = [gcs_c[:, GDN_HEADS + h:GDN_HEADS + h + 1] for h in heads]
    dec = [jnp.exp(jnp.where(incl, gc[h] - gcs_r[GDN_HEADS + h:GDN_HEADS + h + 1, :], MASKED)) for h in heads]
    eg = [jnp.exp(g) for g in gc]
    kb = [kn[h] * beta[h] for h in heads]
    knb = [x.astype(BF16) for x in kn]
    a_kk = [jnp.where(row > col, _dot_nt(kb[h].astype(BF16), knb[h]) * dec[h], 0.0) for h in heads]
    a_qk = [(_dot_nt(qn[h].astype(BF16), knb[h]) * dec[h]).astype(BF16) for h in heads]
    t_inv = [t.astype(BF16) for t in _unit_lower_inverses(a_kk, row, col)]
    uw = [_dot(t_inv[h], jnp.concatenate([v[h] * beta[h], kb[h] * eg[h]], axis=1).astype(BF16)) for h in heads]

    state = [state_ref[h] for h in heads]
    sb = [s.astype(BF16) for s in state]
    ws = [_dot(jnp.concatenate([uw[h][:, GDN_DV:], qn[h] * eg[h]], axis=0).astype(BF16), sb[h]) for h in heads]
    vb = [(uw[h][:, :GDN_DV] - ws[h][:cl]).astype(BF16) for h in heads]
    o = [ws[h][cl:] + _dot(a_qk[h], vb[h]) for h in heads]
    g_last = [g[cl - 1:cl, :] for g in gc]
    for h in heads:
        state_ref[h] = (state[h] * jnp.exp(g_last[h])
                        + _dot_tn((kn[h] * jnp.exp(g_last[h] - gc[h])).astype(BF16), vb[h]))
    for h in heads:
        out = _rms(o[h], ng_ref[...]) * _silu(z_ref[0, :, h * GDN_DV:(h + 1) * GDN_DV])
        o_ref[0, :, h * GDN_DV:(h + 1) * GDN_DV] = out.astype(o_ref.dtype)


def gdn_mixer(qkv, z, sm, conv_w, dt_bias, a_log, norm_g):
    b, s, _ = qkv.shape
    cl = GDN_CHUNK

    def full(a):
        return pl.BlockSpec(a.shape, lambda bi, c: (0,) * a.ndim)

    args = (conv_w, _lane_row(dt_bias, GDN_HEADS), _lane_row(a_log, GDN_HEADS), norm_g.reshape(1, GDN_DV),
            _tri(cl, False), _tri(cl, True))
    return pl.pallas_call(
        _gdn_kernel,
        grid=(b, s // cl),
        in_specs=[pl.BlockSpec((1, cl, GDN_CONV_DIM), lambda bi, c: (bi, c, 0)),
                  pl.BlockSpec((1, cl, GDN_V_W), lambda bi, c: (bi, c, 0)),
                  pl.BlockSpec((1, cl, LANE), lambda bi, c: (bi, c, 0))] + [full(a) for a in args],
        out_specs=pl.BlockSpec((1, cl, GDN_V_W), lambda bi, c: (bi, c, 0)),
        out_shape=jax.ShapeDtypeStruct((b, s, GDN_V_W), BF16),
        scratch_shapes=[pltpu.VMEM((CONV_HALO, GDN_CONV_DIM), F32),
                        pltpu.VMEM((GDN_HEADS, GDN_DK, GDN_DV), F32)],
        compiler_params=_cparams(("parallel", "arbitrary")),
        name="gdn_mixer",
    )(qkv, z, sm, *args)


def _pad_cols(w, width=LANE):
    return jnp.pad(w, ((0, 0), (0, width - w.shape[1])))


HY_Q_W = NSA_HEADS * NSA_DH
HY_KV_W = 6 * NSA_GROUPS * NSA_DH
HY_GATE_W = NSA_HEADS * 3


def hybrid_layer(x2d, b, s, positions, g_pre, g_post, w_in, w_out, pe_k, pe_v, ck_w1, ck_b1, ck_w2,
                 cv_w1, cv_b1, cv_w2, conv_w, conv_b, dt_bias, a_log, d_skip, norm_g):
    o = 0
    cols = {}
    for name, width in (("q", HY_Q_W), ("kv", HY_KV_W), ("gate", HY_GATE_W), ("z", SSD_INNER),
                        ("xbc", SSD_CONV_DIM), ("dt", SSD_HEADS)):
        cols[name] = w_in[:, o:o + width]
        o += width
    gpg = HY_GATE_W // NSA_GROUPS
    w_cat = jnp.concatenate(
        [cols["q"], cols["kv"], cols["z"], cols["xbc"]]
        + [_pad_cols(cols["gate"][:, g * gpg:(g + 1) * gpg]) for g in range(NSA_GROUPS)]
        + [_pad_cols(cols["dt"])], axis=1).astype(BF16)
    splits = (HY_Q_W, HY_KV_W, SSD_INNER, SSD_CONV_DIM, NSA_GROUPS * LANE, LANE)
    q, kv, z, xbc, gates, dts = norm_matmul(x2d, g_pre, w_cat, splits)

    qh, kcmp, vcmp, kslc, vslc, kwin, vwin = nsa_prep(q.reshape(b, s, -1), kv.reshape(b, s, -1), positions)
    kc, vc = nsa_compress(kcmp, vcmp, pe_k, pe_v, ck_w1, ck_b1, ck_w2, cv_w1, cv_b1, cv_w2)
    o_nsa = nsa_attention(qh, kc, vc, kslc, vslc, kwin, vwin, gates.reshape(b, s, -1))
    y = ssd_mixer(z.reshape(b, s, -1), xbc.reshape(b, s, -1), dts.reshape(b, s, -1),
                  conv_w, conv_b, dt_bias, a_log, d_skip, norm_g)
    return outproj_residual([o_nsa.reshape(b * s, -1), y.reshape(b * s, -1)], w_out.astype(BF16), x2d, g_post)


def gdn_layer(x2d, b, s, g_pre, g_post, w_in, conv_w, dt_bias, a_log, norm_g, w_out):
    w_cat = jnp.concatenate([w_in[:, :GDN_CONV_DIM + GDN_V_W], _pad_cols(w_in[:, GDN_CONV_DIM + GDN_V_W:])],
                            axis=1).astype(BF16)
    qkv, z, sm = norm_matmul(x2d, g_pre, w_cat, (GDN_CONV_DIM, GDN_V_W, LANE))
    o = gdn_mixer(qkv.reshape(b, s, -1), z.reshape(b, s, -1), sm.reshape(b, s, -1), conv_w, dt_bias, a_log, norm_g)
    return outproj_residual([o.reshape(b * s, -1)], w_out.astype(BF16), x2d, g_post)


def kernel(x, positions, norm_mix_pre, norm_mix_post, norm_ffn_pre, norm_ffn_post, hy_w_in, hy_w_out, nsa_pe_k, nsa_pe_v, nsa_ck_w1, nsa_ck_b1, nsa_ck_w2, nsa_cv_w1, nsa_cv_b1, nsa_cv_w2, ssd_conv_w, ssd_conv_b, ssd_dt_bias, ssd_a_log, ssd_d, ssd_norm, gdn_w_in, gdn_conv_w, gdn_dt_bias, gdn_a_log, gdn_norm, gdn_w_out, ffn_w_up, ffn_conv_w, ffn_conv_b, ffn_w_down):
    b, s, d = x.shape
    x2d = x.reshape(b * s, d)
    depth = norm_mix_pre.shape[0]
    for layer in range(depth):
        e = layer // 2
        if layer % 2 == 0:
            x2d = hybrid_layer(x2d, b, s, positions, norm_mix_pre[layer], norm_mix_post[layer], hy_w_in[e],
                               hy_w_out[e], nsa_pe_k[e], nsa_pe_v[e], nsa_ck_w1[e], nsa_ck_b1[e], nsa_ck_w2[e],
                               nsa_cv_w1[e], nsa_cv_b1[e], nsa_cv_w2[e], ssd_conv_w[e], ssd_conv_b[e],
                               ssd_dt_bias[e], ssd_a_log[e], ssd_d[e], ssd_norm[e])
        else:
            x2d = gdn_layer(x2d, b, s, norm_mix_pre[layer], norm_mix_post[layer], gdn_w_in[e], gdn_conv_w[e],
                            gdn_dt_bias[e], gdn_a_log[e], gdn_norm[e], gdn_w_out[e])
        x2d = conv_ffn(x2d, s, norm_ffn_pre[layer], ffn_w_up[layer].astype(BF16), ffn_conv_w[layer],
                       ffn_conv_b[layer], ffn_w_down[layer].astype(BF16), norm_ffn_post[layer])
    return x2d.reshape(b, s, d)
```

```python
import functools
import math

import numpy as np
import jax
import jax.numpy as jnp
from jax import lax
from jax.experimental import pallas as pl
from jax.experimental.pallas import tpu as pltpu

F32 = jnp.float32
BF16 = jnp.bfloat16

D_MODEL = 1024
NORM_EPS = 1e-6
MASKED = -1e30

NSA_HEADS = 8
NSA_GROUPS = 2
NSA_REP = NSA_HEADS // NSA_GROUPS
NSA_DH = 64
CMP_BLOCK = 32
CMP_STRIDE = 16
SLC_BLOCK = 64
SLC_TOPK = 16
WINDOW = 512
ROPE_THETA = 500000.0
ROT_DIM = NSA_DH // 4
FORCED_SCORE = 1e9

SSD_HEADS = 8
SSD_P = 64
SSD_INNER = SSD_HEADS * SSD_P
SSD_GROUPS = 2
SSD_N = 128
SSD_CHUNK = 256
SSD_CONV_DIM = SSD_INNER + 2 * SSD_GROUPS * SSD_N

GDN_HEADS = 8
GDN_DK = 128
GDN_DV = 128
GDN_CHUNK = 64
GDN_QK_W = GDN_HEADS * GDN_DK
GDN_V_W = GDN_HEADS * GDN_DV
GDN_CONV_DIM = 2 * GDN_QK_W + GDN_V_W

FFN_DIM = 2816

LANE = 128
CONV_HALO = 8
FFN_HALO = 16
VMEM_LIMIT = 56 * 1024 * 1024


def _cparams(sem):
    return pltpu.CompilerParams(dimension_semantics=sem, vmem_limit_bytes=VMEM_LIMIT)


def _rms(x, g):
    return x * lax.rsqrt(jnp.mean(x * x, axis=-1, keepdims=True) + NORM_EPS) * g


def _silu(x):
    return x * jax.nn.sigmoid(x)


def _softplus(x):
    return jnp.maximum(x, 0.0) + jnp.log1p(jnp.exp(-jnp.abs(x)))


def _dot(a, b):
    return jnp.dot(a, b, preferred_element_type=F32)


def _dot_nt(a, b):
    return lax.dot_general(a, b, (((1,), (1,)), ((), ())), preferred_element_type=F32)


def _dot_tn(a, b):
    return lax.dot_general(a, b, (((0,), (0,)), ((), ())), preferred_element_type=F32)


def _split3(x):
    x1 = x.astype(BF16)
    r1 = x - x1.astype(F32)
    x2 = r1.astype(BF16)
    x3 = (r1 - x2.astype(F32)).astype(BF16)
    return x1, x2, x3


def _split2(x):
    x1 = x.astype(BF16)
    x2 = (x - x1.astype(F32)).astype(BF16)
    return x1, x2


def _cumsum_cols(tril, a):
    a1, a2, a3 = _split3(a)
    return _dot(tril, a1) + _dot(tril, a2) + _dot(tril, a3)


def _cumsum_rows_t(a, triu):
    a1, a2, a3 = _split3(a)
    return _dot_tn(a1, triu) + _dot_tn(a2, triu) + _dot_tn(a3, triu)


def _dot_f32(a, b):
    a1, a2 = _split2(a)
    b1, b2 = _split2(b)
    return _dot(a1, b1) + _dot(a1, b2) + _dot(a2, b1)


def _causal_conv(ext, cw, halo):
    width = cw.shape[0]
    y = ext[halo:] * cw[width - 1:width]
    for k in range(1, width):
        y = y + pltpu.roll(ext, k, 0)[halo:] * cw[width - 1 - k:width - k]
    return y


def _inproj_kernel(x_ref, g_ref, w_ref, *out_refs, splits):
    hn = _rms(x_ref[...], g_ref[...]).astype(BF16)
    y = _dot(hn, w_ref[...])
    off = 0
    for o_ref, n in zip(out_refs, splits):
        o_ref[...] = y[:, off:off + n].astype(o_ref.dtype)
        off += n


def norm_matmul(x2d, g, w, splits, tm=512):
    t = x2d.shape[0]
    n = w.shape[1]
    assert sum(splits) == n and t % tm == 0
    return pl.pallas_call(
        functools.partial(_inproj_kernel, splits=splits),
        grid=(t // tm,),
        in_specs=[pl.BlockSpec((tm, D_MODEL), lambda i: (i, 0)),
                  pl.BlockSpec((1, D_MODEL), lambda i: (0, 0)),
                  pl.BlockSpec((D_MODEL, n), lambda i: (0, 0))],
        out_specs=[pl.BlockSpec((tm, s), lambda i: (i, 0)) for s in splits],
        out_shape=[jax.ShapeDtypeStruct((t, s), F32) for s in splits],
        compiler_params=_cparams(("parallel",)),
        name="norm_matmul",
    )(x2d, g.reshape(1, D_MODEL), w)


def _outproj_kernel(*refs, n_in):
    a_refs = refs[:n_in]
    w_ref, x_ref, g_ref, o_ref = refs[n_in:]
    acc = None
    off = 0
    for a_ref in a_refs:
        k = a_ref.shape[-1]
        part = _dot(a_ref[...], w_ref[off:off + k, :])
        acc = part if acc is None else acc + part
        off += k
    o_ref[...] = x_ref[...] + _rms(acc, g_ref[...])


def outproj_residual(parts, w, x2d, g, tm=512):
    t = x2d.shape[0]
    return pl.pallas_call(
        functools.partial(_outproj_kernel, n_in=len(parts)),
        grid=(t // tm,),
        in_specs=[pl.BlockSpec((tm, p.shape[1]), lambda i: (i, 0)) for p in parts]
        + [pl.BlockSpec(w.shape, lambda i: (0, 0)),
           pl.BlockSpec((tm, D_MODEL), lambda i: (i, 0)),
           pl.BlockSpec((1, D_MODEL), lambda i: (0, 0))],
        out_specs=pl.BlockSpec((tm, D_MODEL), lambda i: (i, 0)),
        out_shape=jax.ShapeDtypeStruct((t, D_MODEL), F32),
        compiler_params=_cparams(("parallel",)),
        name="outproj_residual",
    )(*parts, w, x2d, g.reshape(1, D_MODEL))


def _ffn_kernel(x_ref, halo_ref, gpre_ref, wg_ref, wv_ref, cwg_ref, cwv_ref, cbg_ref, cbv_ref, wd_ref,
                gpost_ref, o_ref, hn_ref, acc_ref, *, tm, seq):
    i = pl.program_id(0)
    j = pl.program_id(1)

    @pl.when(j == 0)
    def _():
        keep = ((i * tm) % seq != 0).astype(F32)
        hn_ref[:FFN_HALO, :] = (_rms(halo_ref[...], gpre_ref[...]) * keep).astype(BF16)
        hn_ref[FFN_HALO:, :] = _rms(x_ref[...], gpre_ref[...]).astype(BF16)
        acc_ref[...] = jnp.zeros_like(acc_ref)

    hn = hn_ref[...]

    def branch(w_ref, cw_ref, cb_ref):
        u = _dot(hn, w_ref[...])
        return _causal_conv(u, cw_ref[...], FFN_HALO) + cb_ref[...]

    gate = branch(wg_ref, cwg_ref, cbg_ref)
    val = branch(wv_ref, cwv_ref, cbv_ref)
    h = (_silu(gate) * val).astype(BF16)
    acc_ref[...] += _dot(h, wd_ref[...])

    @pl.when(j == pl.num_programs(1) - 1)
    def _():
        o_ref[...] = x_ref[...] + _rms(acc_ref[...], gpost_ref[...])


def conv_ffn(x2d, seq, gpre, w_up, conv_w, conv_b, w_down, gpost, tm=512, tf=1408):
    t = x2d.shape[0]
    nf = FFN_DIM // tf
    assert FFN_DIM % tf == 0 and t % tm == 0 and seq % tm == 0 and tm % FFN_HALO == 0
    hb = tm // FFN_HALO
    cb2 = conv_b.reshape(1, 2 * FFN_DIM)
    return pl.pallas_call(
        functools.partial(_ffn_kernel, tm=tm, seq=seq),
        grid=(t // tm, nf),
        in_specs=[pl.BlockSpec((tm, D_MODEL), lambda i, j: (i, 0)),
                  pl.BlockSpec((FFN_HALO, D_MODEL), lambda i, j: (jnp.maximum(i * hb - 1, 0), 0)),
                  pl.BlockSpec((1, D_MODEL), lambda i, j: (0, 0)),
                  pl.BlockSpec((D_MODEL, tf), lambda i, j: (0, j)),
                  pl.BlockSpec((D_MODEL, tf), lambda i, j: (0, nf + j)),
                  pl.BlockSpec((3, tf), lambda i, j: (0, j)),
                  pl.BlockSpec((3, tf), lambda i, j: (0, nf + j)),
                  pl.BlockSpec((1, tf), lambda i, j: (0, j)),
                  pl.BlockSpec((1, tf), lambda i, j: (0, nf + j)),
                  pl.BlockSpec((tf, D_MODEL), lambda i, j: (j, 0)),
                  pl.BlockSpec((1, D_MODEL), lambda i, j: (0, 0))],
        out_specs=pl.BlockSpec((tm, D_MODEL), lambda i, j: (i, 0)),
        out_shape=jax.ShapeDtypeStruct((t, D_MODEL), F32),
        scratch_shapes=[pltpu.VMEM((tm + FFN_HALO, D_MODEL), BF16),
                        pltpu.VMEM((tm, D_MODEL), F32)],
        compiler_params=_cparams(("parallel", "arbitrary")),
        name="conv_ffn",
    )(x2d, x2d, gpre.reshape(1, D_MODEL), w_up, w_up, conv_w, conv_w, cb2, cb2, w_down,
      gpost.reshape(1, D_MODEL))


def _nsa_prep_kernel(q_ref, kv_ref, pos_ref, freq_ref, sign_ref,
                     qh_ref, kcmp_ref, vcmp_ref, kslc_ref, vslc_ref, kwin_ref, vwin_ref):
    pos = pos_ref[0].astype(F32)
    ang = pos * freq_ref[...]
    cs = jnp.cos(ang)
    sn = jnp.sin(ang) * sign_ref[...]
    lane = lax.broadcasted_iota(jnp.int32, (1, LANE), 1) % NSA_DH
    first_half = lane < ROT_DIM // 2

    def rope(x):
        partner = jnp.where(first_half, pltpu.roll(x, LANE - ROT_DIM // 2, 1), pltpu.roll(x, ROT_DIM // 2, 1))
        return x * cs + partner * sn

    scale = NSA_DH ** -0.5
    for j in range(NSA_HEADS // 2):
        t = rope(q_ref[0, :, j * LANE:(j + 1) * LANE]) * scale
        qh_ref[0, 2 * j] = t[:, :NSA_DH].astype(BF16)
        qh_ref[0, 2 * j + 1] = t[:, NSA_DH:].astype(BF16)

    for i, o_ref in ((0, kcmp_ref), (2, kslc_ref), (4, kwin_ref), (1, vcmp_ref)):
        t = kv_ref[0, :, i * LANE:(i + 1) * LANE]
        if i % 2 == 0:
            t = rope(t)
        for g in range(NSA_GROUPS):
            o_ref[0, g] = t[:, g * NSA_DH:(g + 1) * NSA_DH].astype(o_ref.dtype)
    for i, o_ref in ((3, vslc_ref), (5, vwin_ref)):
        t = kv_ref[0, :, i * LANE:(i + 1) * LANE].T
        for g in range(NSA_GROUPS):
            o_ref[0, g] = t[g * NSA_DH:(g + 1) * NSA_DH, :].astype(o_ref.dtype)


def nsa_prep(q, kv, positions, ts=256):
    b, s, _ = q.shape
    inv_freq = ROPE_THETA ** (-jnp.arange(0, ROT_DIM, 2, dtype=F32) / ROT_DIM)
    head_freq = jnp.concatenate([inv_freq, inv_freq, jnp.zeros((NSA_DH - ROT_DIM,), F32)])
    freq = jnp.tile(head_freq, LANE // NSA_DH).reshape(1, LANE)
    head_sign = np.zeros((NSA_DH,), np.float32)
    head_sign[:ROT_DIM // 2] = -1.0
    head_sign[ROT_DIM // 2:ROT_DIM] = 1.0
    sign = jnp.asarray(np.tile(head_sign, LANE // NSA_DH).reshape(1, LANE))
    tok_shape = (b, NSA_GROUPS, s, NSA_DH)
    tok_spec = pl.BlockSpec((1, NSA_GROUPS, ts, NSA_DH), lambda bi, i: (bi, 0, i, 0))
    feat_shape = (b, NSA_GROUPS, NSA_DH, s)
    feat_spec = pl.BlockSpec((1, NSA_GROUPS, NSA_DH, ts), lambda bi, i: (bi, 0, 0, i))
    return pl.pallas_call(
        _nsa_prep_kernel,
        grid=(b, s // ts),
        in_specs=[pl.BlockSpec((1, ts, q.shape[2]), lambda bi, i: (bi, i, 0)),
                  pl.BlockSpec((1, ts, kv.shape[2]), lambda bi, i: (bi, i, 0)),
                  pl.BlockSpec((1, ts, 1), lambda bi, i: (bi, i, 0)),
                  pl.BlockSpec((1, LANE), lambda bi, i: (0, 0)),
                  pl.BlockSpec((1, LANE), lambda bi, i: (0, 0))],
        out_specs=[pl.BlockSpec((1, NSA_HEADS, ts, NSA_DH), lambda bi, i: (bi, 0, i, 0)),
                   tok_spec, tok_spec, tok_spec, feat_spec, tok_spec, feat_spec],
        out_shape=[jax.ShapeDtypeStruct((b, NSA_HEADS, s, NSA_DH), BF16),
                   jax.ShapeDtypeStruct(tok_shape, F32), jax.ShapeDtypeStruct(tok_shape, F32),
                   jax.ShapeDtypeStruct(tok_shape, BF16), jax.ShapeDtypeStruct(feat_shape, BF16),
                   jax.ShapeDtypeStruct(tok_shape, BF16), jax.ShapeDtypeStruct(feat_shape, BF16)],
        compiler_params=_cparams(("parallel", "parallel")),
        name="nsa_prep",
    )(q, kv, positions.reshape(b, s, 1), freq, sign)


def _compress_kernel(k_ref, v_ref, pek_ref, pev_ref, kw1_ref, vw1_ref, kb1_ref, vb1_ref, kw2_ref, vw2_ref,
                     kc_ref, vc_ref):
    half = CMP_STRIDE * NSA_DH

    def mlp(t_ref, pe_ref, w1_ref, b1_ref, w2_ref):
        a = t_ref[0, 0]
        n = a.shape[0]
        h_lo = _dot((a + pe_ref[:, :half]).astype(BF16), w1_ref[:half, :])
        h_hi = _dot((a + pe_ref[:, half:]).astype(BF16), w1_ref[half:, :])
        hid = _silu(h_lo + pltpu.roll(h_hi, n - 1, 0) + b1_ref[...])
        return _dot(hid.astype(BF16), w2_ref[...])

    kc_ref[0, 0] = mlp(k_ref, pek_ref, kw1_ref, kb1_ref, kw2_ref).astype(kc_ref.dtype)
    vc_ref[0, 0] = mlp(v_ref, pev_ref, vw1_ref, vb1_ref, vw2_ref).astype(vc_ref.dtype).T


def nsa_compress(kcmp, vcmp, pe_k, pe_v, ck_w1, ck_b1, ck_w2, cv_w1, cv_b1, cv_w2):
    b, g, s, dh = kcmp.shape
    n = s // CMP_STRIDE
    wide = CMP_STRIDE * dh
    assert CMP_BLOCK == 2 * CMP_STRIDE
    kr = kcmp.reshape(b, g, n, wide)
    vr = vcmp.reshape(b, g, n, wide)
    blk = pl.BlockSpec((1, 1, n, wide), lambda bi, gi: (bi, gi, 0, 0))

    def full(a):
        return pl.BlockSpec(a.shape, lambda bi, gi: (0,) * a.ndim)

    args = (pe_k.reshape(1, 2 * wide), pe_v.reshape(1, 2 * wide), ck_w1.astype(BF16), cv_w1.astype(BF16),
            ck_b1.reshape(1, dh), cv_b1.reshape(1, dh), ck_w2.astype(BF16), cv_w2.astype(BF16))
    return pl.pallas_call(
        _compress_kernel,
        grid=(b, g),
        in_specs=[blk, blk] + [full(a) for a in args],
        out_specs=[pl.BlockSpec((1, 1, n, dh), lambda bi, gi: (bi, gi, 0, 0)),
                   pl.BlockSpec((1, 1, dh, n), lambda bi, gi: (bi, gi, 0, 0))],
        out_shape=[jax.ShapeDtypeStruct((b, g, n, dh), BF16), jax.ShapeDtypeStruct((b, g, dh, n), BF16)],
        compiler_params=_cparams(("parallel", "parallel")),
        name="nsa_compress",
    )(kr, vr, *args)


ATT_TQ = 128
ATT_TK = 256


def _att_kernel(q_ref, kc_ref, vc_ref, ks_ref, vs_ref, kw_ref, vw_ref, gate_ref, ov_ref, o_ref,
                sel_ref, sa_ref, sb_ref, *, seq):
    tq, tk, rep, dh = ATT_TQ, ATT_TK, NSA_REP, NSA_DH
    cols = rep * tq
    nblk = seq // SLC_BLOCK
    ncmp = kc_ref.shape[2]
    t0 = pl.program_id(2) * tq
    q = q_ref[0].reshape(cols, dh)
    tpos = t0 + lax.broadcasted_iota(jnp.int32, (1, tq), 1)

    def per_head(x):
        return jnp.concatenate([x] * rep, axis=1)

    cmp_end = lax.broadcasted_iota(jnp.int32, (ncmp, 1), 0) * CMP_STRIDE + (CMP_BLOCK - 1)
    s_c = _dot_nt(kc_ref[0, 0], q) + per_head(jnp.where(cmp_end <= tpos, 0.0, MASKED))
    p_c = jnp.exp(s_c - jnp.max(s_c, axis=0, keepdims=True))
    l_c = jnp.sum(p_c, axis=0, keepdims=True)
    p_c = p_c * (per_head(jnp.where(tpos >= CMP_BLOCK - 1, 1.0, 0.0)) / l_c)
    o_c = _dot(vc_ref[0, 0], p_c.astype(BF16))

    p_sum = p_c[:, :tq]
    for r in range(1, rep):
        p_sum = p_sum + p_c[:, r * tq:(r + 1) * tq]
    p_hi, p_lo = _split2(p_sum)
    imp = _dot(ov_ref[...], p_hi) + _dot(ov_ref[...], p_lo)
    blk = lax.broadcasted_iota(jnp.int32, (nblk, 1), 0)
    cur = tpos // SLC_BLOCK
    forced = (blk == 0) | (blk == cur) | (blk == cur - 1)
    imp = jnp.where(forced, FORCED_SCORE, jnp.where(blk <= cur, imp, -1.0))
    sel = jnp.full((nblk, tq), MASKED, F32)
    for _ in range(min(SLC_TOPK, nblk)):
        best = jnp.max(imp, axis=0, keepdims=True)
        first = jnp.min(jnp.where(imp == best, blk, nblk), axis=0, keepdims=True)
        hit = blk == first
        sel = jnp.where(hit, 0.0, sel)
        imp = jnp.where(hit, -jnp.inf, imp)
    sel_ref[...] = sel

    blocks_per_tile = tk // SLC_BLOCK
    kidx = lax.broadcasted_iota(jnp.int32, (tk, 1), 0)

    last_tile = seq // tk - 1

    def scores(kt):
        k0 = pl.multiple_of(kt * tk, tk)
        return _dot_nt(ks_ref[0, 0, pl.ds(k0, tk), :], q)

    def update(kt, s_ref, carry):
        m, l, acc = carry
        k0 = pl.multiple_of(kt * tk, tk)
        v = vs_ref[0, 0, :, pl.ds(k0, tk)]
        chosen = jnp.concatenate(
            [jnp.broadcast_to(sel_ref[pl.ds(kt * blocks_per_tile + i, 1), :], (SLC_BLOCK, tq))
             for i in range(blocks_per_tile)], axis=0)
        bias = jnp.where(k0 + kidx <= tpos, chosen, MASKED)
        s = s_ref[...] + per_head(bias)
        m_new = jnp.maximum(m, jnp.max(s, axis=0, keepdims=True))
        alpha = jnp.exp(m - m_new)
        p = jnp.exp(s - m_new)
        l = alpha * l + jnp.sum(p, axis=0, keepdims=True)
        return m_new, l, alpha * acc + _dot(v, p.astype(BF16))

    def slc_pair(j, carry):
        sb_ref[...] = scores(2 * j + 1)
        carry = update(2 * j, sa_ref, carry)
        sa_ref[...] = scores(jnp.minimum(2 * j + 2, last_tile))
        return update(2 * j + 1, sb_ref, carry)

    n_pairs = (t0 + tq + 2 * tk - 1) // (2 * tk)
    sa_ref[...] = scores(0)
    init = (jnp.full((1, cols), MASKED, F32), jnp.zeros((1, cols), F32), jnp.zeros((dh, cols), F32))
    _, l_s, acc_s = lax.fori_loop(0, n_pairs, slc_pair, init)
    o_s = acc_s / l_s

    band = WINDOW + tq
    start = pl.multiple_of(jnp.maximum(t0 - WINDOW, 0), tq)
    dlt = tpos - (start + lax.broadcasted_iota(jnp.int32, (band, 1), 0))
    bias_w = jnp.where((dlt >= 0) & (dlt < WINDOW), 0.0, MASKED)
    s_w = _dot_nt(kw_ref[0, 0, pl.ds(start, band), :], q) + per_head(bias_w)
    p_w = jnp.exp(s_w - jnp.max(s_w, axis=0, keepdims=True))
    l_w = jnp.sum(p_w, axis=0, keepdims=True)
    o_w = _dot(vw_ref[0, 0, :, pl.ds(start, band)], p_w.astype(BF16)) / l_w

    gates = jax.nn.sigmoid(gate_ref[0]).T
    merged = []
    for r in range(rep):
        sl = slice(r * tq, (r + 1) * tq)
        merged.append(gates[3 * r:3 * r + 1] * o_c[:, sl] + gates[3 * r + 1:3 * r + 2] * o_s[:, sl]
                      + gates[3 * r + 2:3 * r + 3] * o_w[:, sl])
    o_ref[0] = jnp.concatenate(merged, axis=0).T.astype(o_ref.dtype)


def _overlap_matrix(seq):
    ncp = seq // CMP_STRIDE
    n_cmp = (seq - CMP_BLOCK) // CMP_STRIDE + 1
    nblk = seq // SLC_BLOCK
    cs = np.arange(ncp) * CMP_STRIDE
    ss = np.arange(nblk) * SLC_BLOCK
    ov = ((cs[None, :] <= ss[:, None] + SLC_BLOCK - 1) & (cs[None, :] + CMP_BLOCK - 1 >= ss[:, None])
          & (np.arange(ncp)[None, :] < n_cmp))
    return jnp.asarray(ov.astype(np.float32), dtype=BF16)


def nsa_attention(qh, kc, vc, kslc, vslc, kwin, vwin, gates):
    b, _, s, dh = qh.shape
    g = NSA_GROUPS
    ncp = kc.shape[2]
    nblk = s // SLC_BLOCK
    assert s % (2 * ATT_TK) == 0 and s >= WINDOW + ATT_TQ
    ov = _overlap_matrix(s)
    tok_spec = pl.BlockSpec((1, 1, s, dh), lambda bi, gi, i: (bi, gi, 0, 0))
    feat_spec = pl.BlockSpec((1, 1, dh, s), lambda bi, gi, i: (bi, gi, 0, 0))
    return pl.pallas_call(
        functools.partial(_att_kernel, seq=s),
        grid=(b, g, s // ATT_TQ),
        in_specs=[pl.BlockSpec((1, NSA_REP, ATT_TQ, dh), lambda bi, gi, i: (bi, gi, i, 0)),
                  pl.BlockSpec((1, 1, ncp, dh), lambda bi, gi, i: (bi, gi, 0, 0)),
                  pl.BlockSpec((1, 1, dh, ncp), lambda bi, gi, i: (bi, gi, 0, 0)),
                  tok_spec, feat_spec, tok_spec, feat_spec,
                  pl.BlockSpec((1, ATT_TQ, LANE), lambda bi, gi, i: (bi, i, gi)),
                  pl.BlockSpec((nblk, ncp), lambda bi, gi, i: (0, 0))],
        out_specs=pl.BlockSpec((1, ATT_TQ, NSA_REP * dh), lambda bi, gi, i: (bi, i, gi)),
        out_shape=jax.ShapeDtypeStruct((b, s, NSA_HEADS * dh), BF16),
        scratch_shapes=[pltpu.VMEM((nblk, ATT_TQ), F32),
                        pltpu.VMEM((ATT_TK, NSA_REP * ATT_TQ), F32),
                        pltpu.VMEM((ATT_TK, NSA_REP * ATT_TQ), F32)],
        compiler_params=_cparams(("parallel", "parallel", "arbitrary")),
        name="nsa_attention",
    )(qh, kc, vc, kslc, vslc, kwin, vwin, gates, ov)


def _ssd_kernel(z_ref, xbc_ref, dt_ref, cw_ref, cb_ref, dtb_ref, alog_ref, drow_ref, ng_ref, tril_ref, triu_ref,
                o_ref, carry_ref, state_ref, y_ref):
    cl = SSD_CHUNK
    hpg = SSD_HEADS // SSD_GROUPS

    @pl.when(pl.program_id(1) == 0)
    def _():
        carry_ref[...] = jnp.zeros_like(carry_ref)
        state_ref[...] = jnp.zeros_like(state_ref)

    xbc = xbc_ref[0]
    ext = jnp.concatenate([carry_ref[...], xbc], axis=0)
    act = _silu(_causal_conv(ext, cw_ref[...], CONV_HALO) + cb_ref[...])
    carry_ref[...] = xbc[cl - CONV_HALO:, :]

    xs = act[:, :SSD_INNER]
    dt = _softplus(dt_ref[0] + dtb_ref[...])
    a = dt * (-jnp.exp(alog_ref[...]))
    acs_c = _cumsum_cols(tril_ref[...], a)
    acs_r = _cumsum_rows_t(a, triu_ref[...])
    causal = lax.broadcasted_iota(jnp.int32, (cl, cl), 0) >= lax.broadcasted_iota(jnp.int32, (cl, cl), 1)

    for g in range(SSD_GROUPS):
        bg = act[:, SSD_INNER + g * SSD_N:SSD_INNER + (g + 1) * SSD_N].astype(BF16)
        cg = act[:, SSD_INNER + (SSD_GROUPS + g) * SSD_N:SSD_INNER + (SSD_GROUPS + g + 1) * SSD_N].astype(BF16)
        cb = _dot_nt(cg, bg)
        state = state_ref[g]
        y_off = _dot(cg, state.astype(BF16))
        weighted = []
        decay = []
        for hl in range(hpg):
            h = g * hpg + hl
            col = acs_c[:, h:h + 1]
            row = acs_r[h:h + 1, :]
            lmat = jnp.exp(jnp.where(causal, col - row, MASKED))
            xh = xs[:, h * SSD_P:(h + 1) * SSD_P]
            xdt = xh * dt[:, h:h + 1]
            y_diag = _dot((cb * lmat).astype(BF16), xdt.astype(BF16))
            y_ref[:, h * SSD_P:(h + 1) * SSD_P] = (y_diag + y_off[:, hl * SSD_P:(hl + 1) * SSD_P] * jnp.exp(col)
                                                   + drow_ref[:, h * SSD_P:(h + 1) * SSD_P] * xh)
            a_last = acs_c[cl - 1:cl, h:h + 1]
            weighted.append((xdt * jnp.exp(a_last - col)).astype(BF16))
            decay.append(jnp.broadcast_to(jnp.exp(a_last), (1, SSD_P)))
        contrib = _dot_tn(bg, jnp.concatenate(weighted, axis=1))
        state_ref[g] = state * jnp.concatenate(decay, axis=1) + contrib

    y = y_ref[...] * _silu(z_ref[0])
    gw = SSD_INNER // SSD_GROUPS
    for g in range(SSD_GROUPS):
        o_ref[0, :, g * gw:(g + 1) * gw] = _rms(y[:, g * gw:(g + 1) * gw],
                                                 ng_ref[:, g * gw:(g + 1) * gw]).astype(o_ref.dtype)


def _tri(n, upper):
    m = np.triu(np.ones((n, n), np.float32)) if upper else np.tril(np.ones((n, n), np.float32))
    return jnp.asarray(m, dtype=BF16)


def _lane_row(v, offset=0):
    row = jnp.zeros((1, LANE), F32)
    return row.at[0, offset:offset + v.shape[0]].set(v)


def ssd_mixer(z, xbc, dts, conv_w, conv_b, dt_bias, a_log, d_skip, norm_g):
    b, s, _ = z.shape
    cl = SSD_CHUNK
    hpg = SSD_HEADS // SSD_GROUPS

    def full(a):
        return pl.BlockSpec(a.shape, lambda bi, c: (0,) * a.ndim)

    args = (conv_w, conv_b.reshape(1, -1), _lane_row(dt_bias), _lane_row(a_log),
            jnp.repeat(d_skip, SSD_P).reshape(1, SSD_INNER), norm_g.reshape(1, SSD_INNER),
            _tri(cl, False), _tri(cl, True))
    return pl.pallas_call(
        _ssd_kernel,
        grid=(b, s // cl),
        in_specs=[pl.BlockSpec((1, cl, SSD_INNER), lambda bi, c: (bi, c, 0)),
                  pl.BlockSpec((1, cl, SSD_CONV_DIM), lambda bi, c: (bi, c, 0)),
                  pl.BlockSpec((1, cl, LANE), lambda bi, c: (bi, c, 0))] + [full(a) for a in args],
        out_specs=pl.BlockSpec((1, cl, SSD_INNER), lambda bi, c: (bi, c, 0)),
        out_shape=jax.ShapeDtypeStruct((b, s, SSD_INNER), BF16),
        scratch_shapes=[pltpu.VMEM((CONV_HALO, SSD_CONV_DIM), F32),
                        pltpu.VMEM((SSD_GROUPS, SSD_N, hpg * SSD_P), F32),
                        pltpu.VMEM((cl, SSD_INNER), F32)],
        compiler_params=_cparams(("parallel", "arbitrary")),
        name="ssd_mixer",
    )(z, xbc, dts, *args)


def _mm_split(a, b):
    return _dot(a[0], b[0]) + _dot(a[0], b[1]) + _dot(a[1], b[0])


def _unit_lower_inverses(a_list, row, col):
    eye = jnp.where(row == col, 1.0, 0.0)
    same16 = (row // 16) == (col // 16)
    same32 = (row // 32) == (col // 32)
    n = [_split2(jnp.where(same16, -a, 0.0)) for a in a_list]
    p = [eye + jnp.where(same16, -a, 0.0) for a in a_list]
    for _ in range(3):
        n = [_split2(_mm_split(x, x)) for x in n]
        p = [x + _mm_split(_split2(x), y) for x, y in zip(p, n)]
    for level_mask in (jnp.where(same16, 0.0, jnp.where(same32, 1.0, 0.0)), jnp.where(same32, 0.0, 1.0)):
        off = [_split2(a * level_mask) for a in a_list]
        ps = [_split2(x) for x in p]
        t = [_split2(_mm_split(x, y)) for x, y in zip(ps, off)]
        p = [x - _mm_split(y, z) for x, y, z in zip(p, t, ps)]
    return p


def _gdn_kernel(qkv_ref, z_ref, sm_ref, cw_ref, dtb_ref, alog_ref, ng_ref, tril_ref, triu_ref,
                o_ref, carry_ref, state_ref):
    cl = GDN_CHUNK
    heads = range(GDN_HEADS)

    @pl.when(pl.program_id(1) == 0)
    def _():
        carry_ref[...] = jnp.zeros_like(carry_ref)
        state_ref[...] = jnp.zeros_like(state_ref)

    qkv = qkv_ref[0]
    ext = jnp.concatenate([carry_ref[...], qkv], axis=0)
    act = _silu(_causal_conv(ext, cw_ref[...], CONV_HALO))
    carry_ref[...] = qkv[cl - CONV_HALO:, :]

    sm = sm_ref[0]
    beta_all = jax.nn.sigmoid(sm)
    g_all = -jnp.exp(alog_ref[...]) * _softplus(sm + dtb_ref[...])
    gcs_c = _cumsum_cols(tril_ref[...], g_all)
    gcs_r = _cumsum_rows_t(g_all, triu_ref[...])
    row = lax.broadcasted_iota(jnp.int32, (cl, cl), 0)
    col = lax.broadcasted_iota(jnp.int32, (cl, cl), 1)
    incl = row >= col

    def l2n(x):
        return x * lax.rsqrt(jnp.sum(x * x, axis=-1, keepdims=True) + NORM_EPS)

    qn = [l2n(act[:, h * GDN_DK:(h + 1) * GDN_DK]) * GDN_DK ** -0.5 for h in heads]
    kn = [l2n(act[:, GDN_QK_W + h * GDN_DK:GDN_QK_W + (h + 1) * GDN_DK]) for h in heads]
    v = [act[:, 2 * GDN_QK_W + h * GDN_DV:2 * GDN_QK_W + (h + 1) * GDN_DV] for h in heads]
    beta = [beta_all[:, h:h + 1] for h in heads]
    gc = [gcs_c[:, GDN_HEADS + h:GDN_HEADS + h + 1] for h in heads]
    dec = [jnp.exp(jnp.where(incl, gc[h] - gcs_r[GDN_HEADS + h:GDN_HEADS + h + 1, :], MASKED)) for h in heads]
    eg = [jnp.exp(g) for g in gc]
    kb = [kn[h] * beta[h] for h in heads]
    knb = [x.astype(BF16) for x in kn]
    a_kk = [jnp.where(row > col, _dot_nt(kb[h].astype(BF16), knb[h]) * dec[h], 0.0) for h in heads]
    a_qk = [(_dot_nt(qn[h].astype(BF16), knb[h]) * dec[h]).astype(BF16) for h in heads]
    t_inv = [t.astype(BF16) for t in _unit_lower_inverses(a_kk, row, col)]
    uw = [_dot(t_inv[h], jnp.concatenate([v[h] * beta[h], kb[h] * eg[h]], axis=1).astype(BF16)) for h in heads]

    state = [state_ref[h] for h in heads]
    sb = [s.astype(BF16) for s in state]
    ws = [_dot(jnp.concatenate([uw[h][:, GDN_DV:], qn[h] * eg[h]], axis=0).astype(BF16), sb[h]) for h in heads]
    vb = [(uw[h][:, :GDN_DV] - ws[h][:cl]).astype(BF16) for h in heads]
    o = [ws[h][cl:] + _dot(a_qk[h], vb[h]) for h in heads]
    g_last = [g[cl - 1:cl, :] for g in gc]
    for h in heads:
        state_ref[h] = (state[h] * jnp.exp(g_last[h])
                        + _dot_tn((kn[h] * jnp.exp(g_last[h] - gc[h])).astype(BF16), vb[h]))
    for h in heads:
        out = _rms(o[h], ng_ref[...]) * _silu(z_ref[0, :, h * GDN_DV:(h + 1) * GDN_DV])
        o_ref[0, :, h * GDN_DV:(h + 1) * GDN_DV] = out.astype(o_ref.dtype)


def gdn_mixer(qkv, z, sm, conv_w, dt_bias, a_log, norm_g):
    b, s, _ = qkv.shape
    cl = GDN_CHUNK

    def full(a):
        return pl.BlockSpec(a.shape, lambda bi, c: (0,) * a.ndim)

    args = (conv_w, _lane_row(dt_bias, GDN_HEADS), _lane_row(a_log, GDN_HEADS), norm_g.reshape(1, GDN_DV),
            _tri(cl, False), _tri(cl, True))
    return pl.pallas_call(
        _gdn_kernel,
        grid=(b, s // cl),
        in_specs=[pl.BlockSpec((1, cl, GDN_CONV_DIM), lambda bi, c: (bi, c, 0)),
                  pl.BlockSpec((1, cl, GDN_V_W), lambda bi, c: (bi, c, 0)),
                  pl.BlockSpec((1, cl, LANE), lambda bi, c: (bi, c, 0))] + [full(a) for a in args],
        out_specs=pl.BlockSpec((1, cl, GDN_V_W), lambda bi, c: (bi, c, 0)),
        out_shape=jax.ShapeDtypeStruct((b, s, GDN_V_W), BF16),
        scratch_shapes=[pltpu.VMEM((CONV_HALO, GDN_CONV_DIM), F32),
                        pltpu.VMEM((GDN_HEADS, GDN_DK, GDN_DV), F32)],
        compiler_params=_cparams(("parallel", "arbitrary")),
        name="gdn_mixer",
    )(qkv, z, sm, *args)


def _pad_cols(w, width=LANE):
    return jnp.pad(w, ((0, 0), (0, width - w.shape[1])))


HY_Q_W = NSA_HEADS * NSA_DH
HY_KV_W = 6 * NSA_GROUPS * NSA_DH
HY_GATE_W = NSA_HEADS * 3


def hybrid_layer(x2d, b, s, positions, g_pre, g_post, w_in, w_out, pe_k, pe_v, ck_w1, ck_b1, ck_w2,
                 cv_w1, cv_b1, cv_w2, conv_w, conv_b, dt_bias, a_log, d_skip, norm_g):
    o = 0
    cols = {}
    for name, width in (("q", HY_Q_W), ("kv", HY_KV_W), ("gate", HY_GATE_W), ("z", SSD_INNER),
                        ("xbc", SSD_CONV_DIM), ("dt", SSD_HEADS)):
        cols[name] = w_in[:, o:o + width]
        o += width
    gpg = HY_GATE_W // NSA_GROUPS
    w_cat = jnp.concatenate(
        [cols["q"], cols["kv"], cols["z"], cols["xbc"]]
        + [_pad_cols(cols["gate"][:, g * gpg:(g + 1) * gpg]) for g in range(NSA_GROUPS)]
        + [_pad_cols(cols["dt"])], axis=1).astype(BF16)
    splits = (HY_Q_W, HY_KV_W, SSD_INNER, SSD_CONV_DIM, NSA_GROUPS * LANE, LANE)
    q, kv, z, xbc, gates, dts = norm_matmul(x2d, g_pre, w_cat, splits)

    qh, kcmp, vcmp, kslc, vslc, kwin, vwin = nsa_prep(q.reshape(b, s, -1), kv.reshape(b, s, -1), positions)
    kc, vc = nsa_compress(kcmp, vcmp, pe_k, pe_v, ck_w1, ck_b1, ck_w2, cv_w1, cv_b1, cv_w2)
    o_nsa = nsa_attention(qh, kc, vc, kslc, vslc, kwin, vwin, gates.reshape(b, s, -1))
    y = ssd_mixer(z.reshape(b, s, -1), xbc.reshape(b, s, -1), dts.reshape(b, s, -1),
                  conv_w, conv_b, dt_bias, a_log, d_skip, norm_g)
    return outproj_residual([o_nsa.reshape(b * s, -1), y.reshape(b * s, -1)], w_out.astype(BF16), x2d, g_post)


def gdn_layer(x2d, b, s, g_pre, g_post, w_in, conv_w, dt_bias, a_log, norm_g, w_out):
    w_cat = jnp.concatenate([w_in[:, :GDN_CONV_DIM + GDN_V_W], _pad_cols(w_in[:, GDN_CONV_DIM + GDN_V_W:])],
                            axis=1).astype(BF16)
    qkv, z, sm = norm_matmul(x2d, g_pre, w_cat, (GDN_CONV_DIM, GDN_V_W, LANE))
    o = gdn_mixer(qkv.reshape(b, s, -1), z.reshape(b, s, -1), sm.reshape(b, s, -1), conv_w, dt_bias, a_log, norm_g)
    return outproj_residual([o.reshape(b * s, -1)], w_out.astype(BF16), x2d, g_post)


def kernel(x, positions, norm_mix_pre, norm_mix_post, norm_ffn_pre, norm_ffn_post, hy_w_in, hy_w_out, nsa_pe_k, nsa_pe_v, nsa_ck_w1, nsa_ck_b1, nsa_ck_w2, nsa_cv_w1, nsa_cv_b1, nsa_cv_w2, ssd_conv_w, ssd_conv_b, ssd_dt_bias, ssd_a_log, ssd_d, ssd_norm, gdn_w_in, gdn_conv_w, gdn_dt_bias, gdn_a_log, gdn_norm, gdn_w_out, ffn_w_up, ffn_conv_w, ffn_conv_b, ffn_w_down):
    b, s, d = x.shape
    x2d = x.reshape(b * s, d)
    depth = norm_mix_pre.shape[0]
    for layer in range(depth):
        e = layer // 2
        if layer % 2 == 0:
            x2d = hybrid_layer(x2d, b, s, positions, norm_mix_pre[layer], norm_mix_post[layer], hy_w_in[e],
                               hy_w_out[e], nsa_pe_k[e], nsa_pe_v[e], nsa_ck_w1[e], nsa_ck_b1[e], nsa_ck_w2[e],
                               nsa_cv_w1[e], nsa_cv_b1[e], nsa_cv_w2[e], ssd_conv_w[e], ssd_conv_b[e],
                               ssd_dt_bias[e], ssd_a_log[e], ssd_d[e], ssd_norm[e])
        else:
            x2d = gdn_layer(x2d, b, s, norm_mix_pre[layer], norm_mix_post[layer], gdn_w_in[e], gdn_conv_w[e],
                            gdn_dt_bias[e], gdn_a_log[e], gdn_norm[e], gdn_w_out[e])
        x2d = conv_ffn(x2d, s, norm_ffn_pre[layer], ffn_w_up[layer].astype(BF16), ffn_conv_w[layer],
                       ffn_conv_b[layer], ffn_w_down[layer].astype(BF16), norm_ffn_post[layer])
    return x2d.reshape(b, s, d)
```

```python
import functools
import math

import numpy as np
import jax
import jax.numpy as jnp
from jax import lax
from jax.experimental import pallas as pl
from jax.experimental.pallas import tpu as pltpu

F32 = jnp.float32
BF16 = jnp.bfloat16

D_MODEL = 1024
NORM_EPS = 1e-6
MASKED = -1e30

NSA_HEADS = 8
NSA_GROUPS = 2
NSA_REP = NSA_HEADS // NSA_GROUPS
NSA_DH = 64
CMP_BLOCK = 32
CMP_STRIDE = 16
SLC_BLOCK = 64
SLC_TOPK = 16
WINDOW = 512
ROPE_THETA = 500000.0
ROT_DIM = NSA_DH // 4
FORCED_SCORE = 1e9

SSD_HEADS = 8
SSD_P = 64
SSD_INNER = SSD_HEADS * SSD_P
SSD_GROUPS = 2
SSD_N = 128
SSD_CHUNK = 256
SSD_CONV_DIM = SSD_INNER + 2 * SSD_GROUPS * SSD_N

GDN_HEADS = 8
GDN_DK = 128
GDN_DV = 128
GDN_CHUNK = 64
GDN_QK_W = GDN_HEADS * GDN_DK
GDN_V_W = GDN_HEADS * GDN_DV
GDN_CONV_DIM = 2 * GDN_QK_W + GDN_V_W

FFN_DIM = 2816

LANE = 128
CONV_HALO = 8
FFN_HALO = 16
VMEM_LIMIT = 56 * 1024 * 1024


def _cparams(sem):
    return pltpu.CompilerParams(dimension_semantics=sem, vmem_limit_bytes=VMEM_LIMIT)


def _rms(x, g):
    return x * lax.rsqrt(jnp.mean(x * x, axis=-1, keepdims=True) + NORM_EPS) * g


def _silu(x):
    return x * jax.nn.sigmoid(x)


def _softplus(x):
    return jnp.maximum(x, 0.0) + jnp.log1p(jnp.exp(-jnp.abs(x)))


def _dot(a, b):
    return jnp.dot(a, b, preferred_element_type=F32)


def _dot_nt(a, b):
    return lax.dot_general(a, b, (((1,), (1,)), ((), ())), preferred_element_type=F32)


def _dot_tn(a, b):
    return lax.dot_general(a, b, (((0,), (0,)), ((), ())), preferred_element_type=F32)


def _split3(x):
    x1 = x.astype(BF16)
    r1 = x - x1.astype(F32)
    x2 = r1.astype(BF16)
    x3 = (r1 - x2.astype(F32)).astype(BF16)
    return x1, x2, x3


def _split2(x):
    x1 = x.astype(BF16)
    x2 = (x - x1.astype(F32)).astype(BF16)
    return x1, x2


def _cumsum_cols(tril, a):
    a1, a2, a3 = _split3(a)
    return _dot(tril, a1) + _dot(tril, a2) + _dot(tril, a3)


def _cumsum_rows_t(a, triu):
    a1, a2, a3 = _split3(a)
    return _dot_tn(a1, triu) + _dot_tn(a2, triu) + _dot_tn(a3, triu)


def _dot_f32(a, b):
    a1, a2 = _split2(a)
    b1, b2 = _split2(b)
    return _dot(a1, b1) + _dot(a1, b2) + _dot(a2, b1)


def _causal_conv(ext, cw, halo):
    width = cw.shape[0]
    y = ext[halo:] * cw[width - 1:width]
    for k in range(1, width):
        y = y + pltpu.roll(ext, k, 0)[halo:] * cw[width - 1 - k:width - k]
    return y


def _inproj_kernel(x_ref, g_ref, w_ref, *out_refs, splits):
    hn = _rms(x_ref[...], g_ref[...]).astype(BF16)
    y = _dot(hn, w_ref[...])
    off = 0
    for o_ref, n in zip(out_refs, splits):
        o_ref[...] = y[:, off:off + n].astype(o_ref.dtype)
        off += n


def norm_matmul(x2d, g, w, splits, tm=512):
    t = x2d.shape[0]
    n = w.shape[1]
    assert sum(splits) == n and t % tm == 0
    return pl.pallas_call(
        functools.partial(_inproj_kernel, splits=splits),
        grid=(t // tm,),
        in_specs=[pl.BlockSpec((tm, D_MODEL), lambda i: (i, 0)),
                  pl.BlockSpec((1, D_MODEL), lambda i: (0, 0)),
                  pl.BlockSpec((D_MODEL, n), lambda i: (0, 0))],
        out_specs=[pl.BlockSpec((tm, s), lambda i: (i, 0)) for s in splits],
        out_shape=[jax.ShapeDtypeStruct((t, s), F32) for s in splits],
        compiler_params=_cparams(("parallel",)),
        name="norm_matmul",
    )(x2d, g.reshape(1, D_MODEL), w)


def _outproj_kernel(*refs, n_in):
    a_refs = refs[:n_in]
    w_ref, x_ref, g_ref, o_ref = refs[n_in:]
    acc = None
    off = 0
    for a_ref in a_refs:
        k = a_ref.shape[-1]
        part = _dot(a_ref[...], w_ref[off:off + k, :])
        acc = part if acc is None else acc + part
        off += k
    o_ref[...] = x_ref[...] + _rms(acc, g_ref[...])


def outproj_residual(parts, w, x2d, g, tm=512):
    t = x2d.shape[0]
    return pl.pallas_call(
        functools.partial(_outproj_kernel, n_in=len(parts)),
        grid=(t // tm,),
        in_specs=[pl.BlockSpec((tm, p.shape[1]), lambda i: (i, 0)) for p in parts]
        + [pl.BlockSpec(w.shape, lambda i: (0, 0)),
           pl.BlockSpec((tm, D_MODEL), lambda i: (i, 0)),
           pl.BlockSpec((1, D_MODEL), lambda i: (0, 0))],
        out_specs=pl.BlockSpec((tm, D_MODEL), lambda i: (i, 0)),
        out_shape=jax.ShapeDtypeStruct((t, D_MODEL), F32),
        compiler_params=_cparams(("parallel",)),
        name="outproj_residual",
    )(*parts, w, x2d, g.reshape(1, D_MODEL))


FFN_CHUNK = 256
FFN_DOWN_CHUNKS = 2


def _ffn_kernel(x_ref, halo_ref, gpre_ref, wup_ref, cw_ref, cb_ref, wd_ref, gpost_ref, o_ref, hn_ref, *, tm, seq):
    keep = ((pl.program_id(0) * tm) % seq != 0).astype(F32)
    hn_ref[:FFN_HALO, :] = (_rms(halo_ref[...], gpre_ref[...]) * keep).astype(BF16)
    hn_ref[FFN_HALO:, :] = _rms(x_ref[...], gpre_ref[...]).astype(BF16)
    hn = hn_ref[...]

    def branch(lo):
        u = _dot(hn, wup_ref[:, lo:lo + FFN_CHUNK])
        return _causal_conv(u, cw_ref[:, lo:lo + FFN_CHUNK], FFN_HALO) + cb_ref[:, lo:lo + FFN_CHUNK]

    acc = None
    pending = []
    n_chunks = FFN_DIM // FFN_CHUNK
    for c in range(n_chunks):
        lo = c * FFN_CHUNK
        pending.append((_silu(branch(lo)) * branch(FFN_DIM + lo)).astype(BF16))
        if len(pending) == FFN_DOWN_CHUNKS or c == n_chunks - 1:
            width = len(pending) * FFN_CHUNK
            h = pending[0] if len(pending) == 1 else jnp.concatenate(pending, axis=1)
            part = _dot(h, wd_ref[lo + FFN_CHUNK - width:lo + FFN_CHUNK, :])
            acc = part if acc is None else acc + part
            pending = []
    o_ref[...] = x_ref[...] + _rms(acc, gpost_ref[...])


def conv_ffn(x2d, seq, gpre, w_up, conv_w, conv_b, w_down, gpost, tm=512):
    t = x2d.shape[0]
    assert FFN_DIM % FFN_CHUNK == 0 and t % tm == 0 and seq % tm == 0 and tm % FFN_HALO == 0
    hb = tm // FFN_HALO

    def resident(a):
        return pl.BlockSpec(a.shape, lambda i: (0,) * a.ndim, pipeline_mode=pl.Buffered(1))

    consts = (gpre.reshape(1, D_MODEL), w_up, conv_w, conv_b.reshape(1, 2 * FFN_DIM), w_down,
              gpost.reshape(1, D_MODEL))
    return pl.pallas_call(
        functools.partial(_ffn_kernel, tm=tm, seq=seq),
        grid=(t // tm,),
        in_specs=[pl.BlockSpec((tm, D_MODEL), lambda i: (i, 0)),
                  pl.BlockSpec((FFN_HALO, D_MODEL), lambda i: (jnp.maximum(i * hb - 1, 0), 0))]
        + [resident(a) for a in consts],
        out_specs=pl.BlockSpec((tm, D_MODEL), lambda i: (i, 0)),
        out_shape=jax.ShapeDtypeStruct((t, D_MODEL), F32),
        scratch_shapes=[pltpu.VMEM((tm + FFN_HALO, D_MODEL), BF16)],
        compiler_params=_cparams(("parallel",)),
        name="conv_ffn",
    )(x2d, x2d, *consts)


def _nsa_prep_kernel(q_ref, kv_ref, pos_ref, freq_ref, sign_ref,
                     qh_ref, kcmp_ref, vcmp_ref, kslc_ref, vslc_ref, kwin_ref, vwin_ref):
    pos = pos_ref[0].astype(F32)
    ang = pos * freq_ref[...]
    cs = jnp.cos(ang)
    sn = jnp.sin(ang) * sign_ref[...]
    lane = lax.broadcasted_iota(jnp.int32, (1, LANE), 1) % NSA_DH
    first_half = lane < ROT_DIM // 2

    def rope(x):
        partner = jnp.where(first_half, pltpu.roll(x, LANE - ROT_DIM // 2, 1), pltpu.roll(x, ROT_DIM // 2, 1))
        return x * cs + partner * sn

    ts = q_ref.shape[1]
    aug_w = kslc_ref.shape[-1]
    low_half = lax.broadcasted_iota(jnp.int32, (1, LANE), 1) < NSA_DH

    def widen(x, upper, fill):
        low = jnp.where(low_half, pltpu.roll(x, NSA_DH, 1) if upper else x, fill[:, :LANE])
        return low if aug_w == LANE else jnp.concatenate([low, fill[:, LANE:]], axis=1)

    scale = NSA_DH ** -0.5 * math.log2(math.e)
    for j in range(NSA_HEADS // 2):
        t = (rope(q_ref[0, :, j * LANE:(j + 1) * LANE]) * scale).T
        qh_ref[0, 2 * j] = t[:NSA_DH].astype(BF16)
        qh_ref[0, 2 * j + 1] = t[NSA_DH:].astype(BF16)

    tok = pl.program_id(1) * ts + lax.broadcasted_iota(jnp.int32, (ts, 1), 0)
    block_onehot = jnp.where(lax.broadcasted_iota(jnp.int32, (1, aug_w), 1) - NSA_DH == tok // SLC_BLOCK, 1.0, 0.0)
    t = rope(kv_ref[0, :, 2 * LANE:3 * LANE])
    for g in range(NSA_GROUPS):
        kslc_ref[0, g] = widen(t, g == 1, block_onehot).astype(BF16)

    for i, o_ref in ((0, kcmp_ref), (4, kwin_ref), (1, vcmp_ref)):
        t = kv_ref[0, :, i * LANE:(i + 1) * LANE]
        if i % 2 == 0:
            t = rope(t)
        for g in range(NSA_GROUPS):
            o_ref[0, g] = t[:, g * NSA_DH:(g + 1) * NSA_DH].astype(o_ref.dtype)
    for i, o_ref in ((3, vslc_ref), (5, vwin_ref)):
        t = kv_ref[0, :, i * LANE:(i + 1) * LANE].T
        for g in range(NSA_GROUPS):
            o_ref[0, g] = t[g * NSA_DH:(g + 1) * NSA_DH, :].astype(o_ref.dtype)


def _aug_width(seq):
    return -(-(NSA_DH + seq // SLC_BLOCK) // LANE) * LANE


def nsa_prep(q, kv, positions, ts=256):
    b, s, _ = q.shape
    inv_freq = ROPE_THETA ** (-jnp.arange(0, ROT_DIM, 2, dtype=F32) / ROT_DIM)
    head_freq = jnp.concatenate([inv_freq, inv_freq, jnp.zeros((NSA_DH - ROT_DIM,), F32)])
    freq = jnp.tile(head_freq, LANE // NSA_DH).reshape(1, LANE)
    head_sign = np.zeros((NSA_DH,), np.float32)
    head_sign[:ROT_DIM // 2] = -1.0
    head_sign[ROT_DIM // 2:ROT_DIM] = 1.0
    sign = jnp.asarray(np.tile(head_sign, LANE // NSA_DH).reshape(1, LANE))
    tok_shape = (b, NSA_GROUPS, s, NSA_DH)
    tok_spec = pl.BlockSpec((1, NSA_GROUPS, ts, NSA_DH), lambda bi, i: (bi, 0, i, 0))
    feat_shape = (b, NSA_GROUPS, NSA_DH, s)
    feat_spec = pl.BlockSpec((1, NSA_GROUPS, NSA_DH, ts), lambda bi, i: (bi, 0, 0, i))
    aug_w = _aug_width(s)
    return pl.pallas_call(
        _nsa_prep_kernel,
        grid=(b, s // ts),
        in_specs=[pl.BlockSpec((1, ts, q.shape[2]), lambda bi, i: (bi, i, 0)),
                  pl.BlockSpec((1, ts, kv.shape[2]), lambda bi, i: (bi, i, 0)),
                  pl.BlockSpec((1, ts, 1), lambda bi, i: (bi, i, 0)),
                  pl.BlockSpec((1, LANE), lambda bi, i: (0, 0)),
                  pl.BlockSpec((1, LANE), lambda bi, i: (0, 0))],
        out_specs=[pl.BlockSpec((1, NSA_HEADS, NSA_DH, ts), lambda bi, i: (bi, 0, 0, i)),
                   tok_spec, tok_spec,
                   pl.BlockSpec((1, NSA_GROUPS, ts, aug_w), lambda bi, i: (bi, 0, i, 0)),
                   feat_spec, tok_spec, feat_spec],
        out_shape=[jax.ShapeDtypeStruct((b, NSA_HEADS, NSA_DH, s), BF16),
                   jax.ShapeDtypeStruct(tok_shape, F32), jax.ShapeDtypeStruct(tok_shape, F32),
                   jax.ShapeDtypeStruct((b, NSA_GROUPS, s, aug_w), BF16), jax.ShapeDtypeStruct(feat_shape, BF16),
                   jax.ShapeDtypeStruct(tok_shape, BF16), jax.ShapeDtypeStruct(feat_shape, BF16)],
        compiler_params=_cparams(("parallel", "parallel")),
        name="nsa_prep",
    )(q, kv, positions.reshape(b, s, 1), freq, sign)


def _compress_kernel(k_ref, v_ref, pek_ref, pev_ref, kw1_ref, vw1_ref, kb1_ref, vb1_ref, kw2_ref, vw2_ref,
                     kc_ref, vc_ref):
    half = CMP_STRIDE * NSA_DH

    def mlp(t_ref, pe_ref, w1_ref, b1_ref, w2_ref):
        a = t_ref[0, 0]
        n = a.shape[0]
        h_lo = _dot((a + pe_ref[:, :half]).astype(BF16), w1_ref[:half, :])
        h_hi = _dot((a + pe_ref[:, half:]).astype(BF16), w1_ref[half:, :])
        hid = _silu(h_lo + pltpu.roll(h_hi, n - 1, 0) + b1_ref[...])
        return _dot(hid.astype(BF16), w2_ref[...])

    kc_ref[0, 0] = mlp(k_ref, pek_ref, kw1_ref, kb1_ref, kw2_ref).astype(kc_ref.dtype)
    vc_ref[0, 0] = mlp(v_ref, pev_ref, vw1_ref, vb1_ref, vw2_ref).astype(vc_ref.dtype).T


def nsa_compress(kcmp, vcmp, pe_k, pe_v, ck_w1, ck_b1, ck_w2, cv_w1, cv_b1, cv_w2):
    b, g, s, dh = kcmp.shape
    n = s // CMP_STRIDE
    wide = CMP_STRIDE * dh
    assert CMP_BLOCK == 2 * CMP_STRIDE
    kr = kcmp.reshape(b, g, n, wide)
    vr = vcmp.reshape(b, g, n, wide)
    blk = pl.BlockSpec((1, 1, n, wide), lambda bi, gi: (bi, gi, 0, 0))

    def full(a):
        return pl.BlockSpec(a.shape, lambda bi, gi: (0,) * a.ndim)

    args = (pe_k.reshape(1, 2 * wide), pe_v.reshape(1, 2 * wide), ck_w1.astype(BF16), cv_w1.astype(BF16),
            ck_b1.reshape(1, dh), cv_b1.reshape(1, dh), ck_w2.astype(BF16), cv_w2.astype(BF16))
    return pl.pallas_call(
        _compress_kernel,
        grid=(b, g),
        in_specs=[blk, blk] + [full(a) for a in args],
        out_specs=[pl.BlockSpec((1, 1, n, dh), lambda bi, gi: (bi, gi, 0, 0)),
                   pl.BlockSpec((1, 1, dh, n), lambda bi, gi: (bi, gi, 0, 0))],
        out_shape=[jax.ShapeDtypeStruct((b, g, n, dh), BF16), jax.ShapeDtypeStruct((b, g, dh, n), BF16)],
        compiler_params=_cparams(("parallel", "parallel")),
        name="nsa_compress",
    )(kr, vr, *args)


ATT_TQ = 128
ATT_TK = 256
ATT_RING = 4


def _att_kernel(q_ref, kc_ref, vc_ref, ks_ref, vs_ref, kw_ref, vw_ref, gate_ref, ov_ref, o_ref,
                *s_refs, seq):
    tq, tk, rep, dh = ATT_TQ, ATT_TK, NSA_REP, NSA_DH
    cols = rep * tq
    nblk = seq // SLC_BLOCK
    ncmp = kc_ref.shape[2]
    aug_w = ks_ref.shape[-1]
    t0 = pl.program_id(2) * tq
    q = jnp.concatenate([q_ref[0, r] for r in range(rep)], axis=1)
    tpos = t0 + lax.broadcasted_iota(jnp.int32, (1, tq), 1)

    def per_head(x):
        return jnp.concatenate([x] * rep, axis=1)


    cmp_end = lax.broadcasted_iota(jnp.int32, (ncmp, 1), 0) * CMP_STRIDE + (CMP_BLOCK - 1)
    s_c = _dot(kc_ref[0, 0], q) + per_head(jnp.where(cmp_end <= tpos, 0.0, MASKED))
    p_c = jnp.exp2(s_c - jnp.max(s_c, axis=0, keepdims=True))
    l_c = jnp.sum(p_c, axis=0, keepdims=True)
    p_c = p_c * (per_head(jnp.where(tpos >= CMP_BLOCK - 1, 1.0, 0.0)) / l_c)
    o_c = _dot(vc_ref[0, 0], p_c.astype(BF16))

    p_sum = p_c[:, :tq]
    for r in range(1, rep):
        p_sum = p_sum + p_c[:, r * tq:(r + 1) * tq]
    p_hi, p_lo = _split2(p_sum)
    imp = _dot(ov_ref[...], p_hi) + _dot(ov_ref[...], p_lo)
    blk = lax.broadcasted_iota(jnp.int32, (nblk, 1), 0)
    cur = tpos // SLC_BLOCK
    forced = (blk == 0) | (blk == cur) | (blk == cur - 1)
    imp = jnp.where(forced, FORCED_SCORE, jnp.where(blk <= cur, imp, -1.0))
    sel = jnp.full((nblk, tq), MASKED, F32)
    for _ in range(min(SLC_TOPK, nblk)):
        best = jnp.max(imp, axis=0, keepdims=True)
        first = jnp.min(jnp.where(imp == best, blk, nblk), axis=0, keepdims=True)
        hit = blk == first
        sel = jnp.where(hit, 0.0, sel)
        imp = jnp.where(hit, -jnp.inf, imp)

    mask_rows = jnp.concatenate([sel.astype(BF16), jnp.zeros((aug_w - dh - nblk, tq), BF16)], axis=0)
    q_aug = jnp.concatenate([q, per_head(mask_rows)], axis=0)
    kidx = lax.broadcasted_iota(jnp.int32, (tk, 1), 0)

    def produce(kt, s_ref):
        k0 = pl.multiple_of(kt * tk, tk)
        s = _dot(ks_ref[0, 0, pl.ds(k0, tk), :], q_aug)
        s_ref[...] = s
        return jnp.max(s, axis=0, keepdims=True)

    def consume(kt, s_ref, tile_max, carry, causal):
        m, l, acc = carry
        k0 = pl.multiple_of(kt * tk, tk)
        v = vs_ref[0, 0, :, pl.ds(k0, tk)]
        s = s_ref[...]
        if causal:
            s = s + per_head(jnp.where(k0 + kidx <= tpos, 0.0, MASKED))
            tile_max = jnp.max(s, axis=0, keepdims=True)
        m_new = jnp.maximum(m, tile_max)
        alpha = jnp.exp2(m - m_new)
        p = jnp.exp2(s - m_new)
        l = alpha * l + jnp.sum(p, axis=0, keepdims=True)
        return m_new, l, alpha * acc + _dot(v, p.astype(BF16))

    nbuf = len(s_refs)

    def slc_trip(j, carry):
        maxes, state = list(carry[:nbuf]), carry[nbuf:]
        for i, s_ref in enumerate(s_refs):
            state = consume(nbuf * j + i, s_ref, maxes[i], state, False)
            maxes[i] = produce(nbuf * (j + 1) + i, s_ref)
        return tuple(maxes) + state

    n_full = t0 // (nbuf * tk)
    init = tuple(produce(i, s_ref) for i, s_ref in enumerate(s_refs)) + (
        jnp.full((1, cols), MASKED, F32), jnp.zeros((1, cols), F32), jnp.zeros((dh, cols), F32))
    state = lax.fori_loop(0, n_full, slc_trip, init)[nbuf:]
    for i, s_ref in enumerate(s_refs):
        state = consume(nbuf * n_full + i, s_ref, None, state, True)
    _, l_s, acc_s = state
    o_s = acc_s / l_s

    band = WINDOW + tq
    start = pl.multiple_of(jnp.maximum(t0 - WINDOW, 0), tq)
    dlt = tpos - (start + lax.broadcasted_iota(jnp.int32, (band, 1), 0))
    bias_w = jnp.where((dlt >= 0) & (dlt < WINDOW), 0.0, MASKED)
    s_w = _dot(kw_ref[0, 0, pl.ds(start, band), :], q) + per_head(bias_w)
    p_w = jnp.exp2(s_w - jnp.max(s_w, axis=0, keepdims=True))
    l_w = jnp.sum(p_w, axis=0, keepdims=True)
    o_w = _dot(vw_ref[0, 0, :, pl.ds(start, band)], p_w.astype(BF16)) / l_w

    gates = jax.nn.sigmoid(gate_ref[0]).T
    merged = []
    for r in range(rep):
        sl = slice(r * tq, (r + 1) * tq)
        merged.append(gates[3 * r:3 * r + 1] * o_c[:, sl] + gates[3 * r + 1:3 * r + 2] * o_s[:, sl]
                      + gates[3 * r + 2:3 * r + 3] * o_w[:, sl])
    o_ref[0] = jnp.concatenate(merged, axis=0).T.astype(o_ref.dtype)


def _overlap_matrix(seq):
    ncp = seq // CMP_STRIDE
    n_cmp = (seq - CMP_BLOCK) // CMP_STRIDE + 1
    nblk = seq // SLC_BLOCK
    cs = np.arange(ncp) * CMP_STRIDE
    ss = np.arange(nblk) * SLC_BLOCK
    ov = ((cs[None, :] <= ss[:, None] + SLC_BLOCK - 1) & (cs[None, :] + CMP_BLOCK - 1 >= ss[:, None])
          & (np.arange(ncp)[None, :] < n_cmp))
    return jnp.asarray(ov.astype(np.float32), dtype=BF16)


def nsa_attention(qh, kc, vc, kslc, vslc, kwin, vwin, gates):
    b, _, dh, s = qh.shape
    aug_w = kslc.shape[-1]
    g = NSA_GROUPS
    ncp = kc.shape[2]
    nblk = s // SLC_BLOCK
    assert s % (ATT_RING * ATT_TK) == 0 and s >= WINDOW + ATT_TQ and aug_w == _aug_width(s)
    ov = _overlap_matrix(s)
    tok_spec = pl.BlockSpec((1, 1, s, dh), lambda bi, gi, i: (bi, gi, 0, 0))
    feat_spec = pl.BlockSpec((1, 1, dh, s), lambda bi, gi, i: (bi, gi, 0, 0))
    return pl.pallas_call(
        functools.partial(_att_kernel, seq=s),
        grid=(b, g, s // ATT_TQ),
        in_specs=[pl.BlockSpec((1, NSA_REP, dh, ATT_TQ), lambda bi, gi, i: (bi, gi, 0, i)),
                  pl.BlockSpec((1, 1, ncp, dh), lambda bi, gi, i: (bi, gi, 0, 0)),
                  pl.BlockSpec((1, 1, dh, ncp), lambda bi, gi, i: (bi, gi, 0, 0)),
                  pl.BlockSpec((1, 1, s, aug_w), lambda bi, gi, i: (bi, gi, 0, 0)),
                  feat_spec, tok_spec, feat_spec,
                  pl.BlockSpec((1, ATT_TQ, LANE), lambda bi, gi, i: (bi, i, gi)),
                  pl.BlockSpec((nblk, ncp), lambda bi, gi, i: (0, 0))],
        out_specs=pl.BlockSpec((1, ATT_TQ, NSA_REP * dh), lambda bi, gi, i: (bi, i, gi)),
        out_shape=jax.ShapeDtypeStruct((b, s, NSA_HEADS * dh), BF16),
        scratch_shapes=[pltpu.VMEM((ATT_TK, NSA_REP * ATT_TQ), F32)] * ATT_RING,
        compiler_params=_cparams(("parallel", "parallel", "arbitrary")),
        name="nsa_attention",
    )(qh, kc, vc, kslc, vslc, kwin, vwin, gates, ov)


def _ssd_kernel(z_ref, xbc_ref, dt_ref, cw_ref, cb_ref, dtb_ref, alog_ref, drow_ref, ng_ref, tril_ref, triu_ref,
                o_ref, carry_ref, state_ref, y_ref):
    cl = SSD_CHUNK
    hpg = SSD_HEADS // SSD_GROUPS

    @pl.when(pl.program_id(1) == 0)
    def _():
        carry_ref[...] = jnp.zeros_like(carry_ref)
        state_ref[...] = jnp.zeros_like(state_ref)

    xbc = xbc_ref[0]
    ext = jnp.concatenate([carry_ref[...], xbc], axis=0)
    act = _silu(_causal_conv(ext, cw_ref[...], CONV_HALO) + cb_ref[...])
    carry_ref[...] = xbc[cl - CONV_HALO:, :]

    xs = act[:, :SSD_INNER]
    dt = _softplus(dt_ref[0] + dtb_ref[...])
    a = dt * (-jnp.exp(alog_ref[...]))
    acs_c = _cumsum_cols(tril_ref[...], a)
    acs_r = _cumsum_rows_t(a, triu_ref[...])
    causal = lax.broadcasted_iota(jnp.int32, (cl, cl), 0) >= lax.broadcasted_iota(jnp.int32, (cl, cl), 1)

    for g in range(SSD_GROUPS):
        bg = act[:, SSD_INNER + g * SSD_N:SSD_INNER + (g + 1) * SSD_N].astype(BF16)
        cg = act[:, SSD_INNER + (SSD_GROUPS + g) * SSD_N:SSD_INNER + (SSD_GROUPS + g + 1) * SSD_N].astype(BF16)
        cb = _dot_nt(cg, bg)
        state = state_ref[g]
        y_off = _dot(cg, state.astype(BF16))
        weighted = []
        decay = []
        for hl in range(hpg):
            h = g * hpg + hl
            col = acs_c[:, h:h + 1]
            row = acs_r[h:h + 1, :]
            lmat = jnp.exp(jnp.where(causal, col - row, MASKED))
            xh = xs[:, h * SSD_P:(h + 1) * SSD_P]
            xdt = xh * dt[:, h:h + 1]
            y_diag = _dot((cb * lmat).astype(BF16), xdt.astype(BF16))
            y_ref[:, h * SSD_P:(h + 1) * SSD_P] = (y_diag + y_off[:, hl * SSD_P:(hl + 1) * SSD_P] * jnp.exp(col)
                                                   + drow_ref[:, h * SSD_P:(h + 1) * SSD_P] * xh)
            a_last = acs_c[cl - 1:cl, h:h + 1]
            weighted.append((xdt * jnp.exp(a_last - col)).astype(BF16))
            decay.append(jnp.broadcast_to(jnp.exp(a_last), (1, SSD_P)))
        contrib = _dot_tn(bg, jnp.concatenate(weighted, axis=1))
        state_ref[g] = state * jnp.concatenate(decay, axis=1) + contrib

    y = y_ref[...] * _silu(z_ref[0])
    gw = SSD_INNER // SSD_GROUPS
    for g in range(SSD_GROUPS):
        o_ref[0, :, g * gw:(g + 1) * gw] = _rms(y[:, g * gw:(g + 1) * gw],
                                                 ng_ref[:, g * gw:(g + 1) * gw]).astype(o_ref.dtype)


def _tri(n, upper):
    m = np.triu(np.ones((n, n), np.float32)) if upper else np.tril(np.ones((n, n), np.float32))
    return jnp.asarray(m, dtype=BF16)


def _lane_row(v, offset=0):
    row = jnp.zeros((1, LANE), F32)
    return row.at[0, offset:offset + v.shape[0]].set(v)


def ssd_mixer(z, xbc, dts, conv_w, conv_b, dt_bias, a_log, d_skip, norm_g):
    b, s, _ = z.shape
    cl = SSD_CHUNK
    hpg = SSD_HEADS // SSD_GROUPS

    def full(a):
        return pl.BlockSpec(a.shape, lambda bi, c: (0,) * a.ndim)

    args = (conv_w, conv_b.reshape(1, -1), _lane_row(dt_bias), _lane_row(a_log),
            jnp.repeat(d_skip, SSD_P).reshape(1, SSD_INNER), norm_g.reshape(1, SSD_INNER),
            _tri(cl, False), _tri(cl, True))
    return pl.pallas_call(
        _ssd_kernel,
        grid=(b, s // cl),
        in_specs=[pl.BlockSpec((1, cl, SSD_INNER), lambda bi, c: (bi, c, 0)),
                  pl.BlockSpec((1, cl, SSD_CONV_DIM), lambda bi, c: (bi, c, 0)),
                  pl.BlockSpec((1, cl, LANE), lambda bi, c: (bi, c, 0))] + [full(a) for a in args],
        out_specs=pl.BlockSpec((1, cl, SSD_INNER), lambda bi, c: (bi, c, 0)),
        out_shape=jax.ShapeDtypeStruct((b, s, SSD_INNER), BF16),
        scratch_shapes=[pltpu.VMEM((CONV_HALO, SSD_CONV_DIM), F32),
                        pltpu.VMEM((SSD_GROUPS, SSD_N, hpg * SSD_P), F32),
                        pltpu.VMEM((cl, SSD_INNER), F32)],
        compiler_params=_cparams(("parallel", "arbitrary")),
        name="ssd_mixer",
    )(z, xbc, dts, *args)


def _mm_split(a, b):
    return _dot(a[0], b[0]) + _dot(a[0], b[1]) + _dot(a[1], b[0])


def _unit_lower_inverses(a_list, row, col):
    eye = jnp.where(row == col, 1.0, 0.0)
    same16 = (row // 16) == (col // 16)
    same32 = (row // 32) == (col // 32)
    n = [_split2(jnp.where(same16, -a, 0.0)) for a in a_list]
    p = [eye + jnp.where(same16, -a, 0.0) for a in a_list]
    for _ in range(3):
        n = [_split2(_mm_split(x, x)) for x in n]
        p = [x + _mm_split(_split2(x), y) for x, y in zip(p, n)]
    for level_mask in (jnp.where(same16, 0.0, jnp.where(same32, 1.0, 0.0)), jnp.where(same32, 0.0, 1.0)):
        off = [_split2(a * level_mask) for a in a_list]
        ps = [_split2(x) for x in p]
        t = [_split2(_mm_split(x, y)) for x, y in zip(ps, off)]
        p = [x - _mm_split(y, z) for x, y, z in zip(p, t, ps)]
    return p


def _gdn_kernel(qkv_ref, z_ref, sm_ref, cw_ref, dtb_ref, alog_ref, ng_ref, tril_ref, triu_ref,
                o_ref, carry_ref, state_ref):
    cl = GDN_CHUNK
    heads = range(GDN_HEADS)

    @pl.when(pl.program_id(1) == 0)
    def _():
        carry_ref[...] = jnp.zeros_like(carry_ref)
        state_ref[...] = jnp.zeros_like(state_ref)

    qkv = qkv_ref[0]
    ext = jnp.concatenate([carry_ref[...], qkv], axis=0)
    act = _silu(_causal_conv(ext, cw_ref[...], CONV_HALO))
    carry_ref[...] = qkv[cl - CONV_HALO:, :]

    sm = sm_ref[0]
    beta_all = jax.nn.sigmoid(sm)
    g_all = -jnp.exp(alog_ref[...]) * _softplus(sm + dtb_ref[...])
    gcs_c = _cumsum_cols(tril_ref[...], g_all)
    gcs_r = _cumsum_rows_t(g_all, triu_ref[...])
    row = lax.broadcasted_iota(jnp.int32, (cl, cl), 0)
    col = lax.broadcasted_iota(jnp.int32, (cl, cl), 1)
    incl = row >= col

    def l2n(x):
        return x * lax.rsqrt(jnp.sum(x * x, axis=-1, keepdims=True) + NORM_EPS)

    qn = [l2n(act[:, h * GDN_DK:(h + 1) * GDN_DK]) * GDN_DK ** -0.5 for h in heads]
    kn = [l2n(act[:, GDN_QK_W + h * GDN_DK:GDN_QK_W + (h + 1) * GDN_DK]) for h in heads]
    v = [act[:, 2 * GDN_QK_W + h * GDN_DV:2 * GDN_QK_W + (h + 1) * GDN_DV] for h in heads]
    beta = [beta_all[:, h:h + 1] for h in heads]
    gc = [gcs_c[:, GDN_HEADS + h:GDN_HEADS + h + 1] for h in heads]
    dec = [jnp.exp(jnp.where(incl, gc[h] - gcs_r[GDN_HEADS + h:GDN_HEADS + h + 1, :], MASKED)) for h in heads]
    eg = [jnp.exp(g) for g in gc]
    kb = [kn[h] * beta[h] for h in heads]
    knb = [x.astype(BF16) for x in kn]
    a_kk = [jnp.where(row > col, _dot_nt(kb[h].astype(BF16), knb[h]) * dec[h], 0.0) for h in heads]
    a_qk = [(_dot_nt(qn[h].astype(BF16), knb[h]) * dec[h]).astype(BF16) for h in heads]
    t_inv = [t.astype(BF16) for t in _unit_lower_inverses(a_kk, row, col)]
    uw = [_dot(t_inv[h], jnp.concatenate([v[h] * beta[h], kb[h] * eg[h]], axis=1).astype(BF16)) for h in heads]

    state = [state_ref[h] for h in heads]
    sb = [s.astype(BF16) for s in state]
    ws = [_dot(jnp.concatenate([uw[h][:, GDN_DV:], qn[h] * eg[h]], axis=0).astype(BF16), sb[h]) for h in heads]
    vb = [(uw[h][:, :GDN_DV] - ws[h][:cl]).astype(BF16) for h in heads]
    o = [ws[h][cl:] + _dot(a_qk[h], vb[h]) for h in heads]
    g_last = [g[cl - 1:cl, :] for g in gc]
    for h in heads:
        state_ref[h] = (state[h] * jnp.exp(g_last[h])
                        + _dot_tn((kn[h] * jnp.exp(g_last[h] - gc[h])).astype(BF16), vb[h]))
    for h in heads:
        out = _rms(o[h], ng_ref[...]) * _silu(z_ref[0, :, h * GDN_DV:(h + 1) * GDN_DV])
        o_ref[0, :, h * GDN_DV:(h + 1) * GDN_DV] = out.astype(o_ref.dtype)


def gdn_mixer(qkv, z, sm, conv_w, dt_bias, a_log, norm_g):
    b, s, _ = qkv.shape
    cl = GDN_CHUNK

    def full(a):
        return pl.BlockSpec(a.shape, lambda bi, c: (0,) * a.ndim)

    args = (conv_w, _lane_row(dt_bias, GDN_HEADS), _lane_row(a_log, GDN_HEADS), norm_g.reshape(1, GDN_DV),
            _tri(cl, False), _tri(cl, True))
    return pl.pallas_call(
        _gdn_kernel,
        grid=(b, s // cl),
        in_specs=[pl.BlockSpec((1, cl, GDN_CONV_DIM), lambda bi, c: (bi, c, 0)),
                  pl.BlockSpec((1, cl, GDN_V_W), lambda bi, c: (bi, c, 0)),
                  pl.BlockSpec((1, cl, LANE), lambda bi, c: (bi, c, 0))] + [full(a) for a in args],
        out_specs=pl.BlockSpec((1, cl, GDN_V_W), lambda bi, c: (bi, c, 0)),
        out_shape=jax.ShapeDtypeStruct((b, s, GDN_V_W), BF16),
        scratch_shapes=[pltpu.VMEM((CONV_HALO, GDN_CONV_DIM), F32),
                        pltpu.VMEM((GDN_HEADS, GDN_DK, GDN_DV), F32)],
        compiler_params=_cparams(("parallel", "arbitrary")),
        name="gdn_mixer",
    )(qkv, z, sm, *args)


def _pad_cols(w, width=LANE):
    return jnp.pad(w, ((0, 0), (0, width - w.shape[1])))


HY_Q_W = NSA_HEADS * NSA_DH
HY_KV_W = 6 * NSA_GROUPS * NSA_DH
HY_GATE_W = NSA_HEADS * 3


def hybrid_layer(x2d, b, s, positions, g_pre, g_post, w_in, w_out, pe_k, pe_v, ck_w1, ck_b1, ck_w2,
                 cv_w1, cv_b1, cv_w2, conv_w, conv_b, dt_bias, a_log, d_skip, norm_g):
    o = 0
    cols = {}
    for name, width in (("q", HY_Q_W), ("kv", HY_KV_W), ("gate", HY_GATE_W), ("z", SSD_INNER),
                        ("xbc", SSD_CONV_DIM), ("dt", SSD_HEADS)):
        cols[name] = w_in[:, o:o + width]
        o += width
    gpg = HY_GATE_W // NSA_GROUPS
    w_cat = jnp.concatenate(
        [cols["q"], cols["kv"], cols["z"], cols["xbc"]]
        + [_pad_cols(cols["gate"][:, g * gpg:(g + 1) * gpg]) for g in range(NSA_GROUPS)]
        + [_pad_cols(cols["dt"])], axis=1).astype(BF16)
    splits = (HY_Q_W, HY_KV_W, SSD_INNER, SSD_CONV_DIM, NSA_GROUPS * LANE, LANE)
    q, kv, z, xbc, gates, dts = norm_matmul(x2d, g_pre, w_cat, splits)

    qh, kcmp, vcmp, kslc, vslc, kwin, vwin = nsa_prep(q.reshape(b, s, -1), kv.reshape(b, s, -1), positions)
    kc, vc = nsa_compress(kcmp, vcmp, pe_k, pe_v, ck_w1, ck_b1, ck_w2, cv_w1, cv_b1, cv_w2)
    o_nsa = nsa_attention(qh, kc, vc, kslc, vslc, kwin, vwin, gates.reshape(b, s, -1))
    y = ssd_mixer(z.reshape(b, s, -1), xbc.reshape(b, s, -1), dts.reshape(b, s, -1),
                  conv_w, conv_b, dt_bias, a_log, d_skip, norm_g)
    return outproj_residual([o_nsa.reshape(b * s, -1), y.reshape(b * s, -1)], w_out.astype(BF16), x2d, g_post)


def gdn_layer(x2d, b, s, g_pre, g_post, w_in, conv_w, dt_bias, a_log, norm_g, w_out):
    w_cat = jnp.concatenate([w_in[:, :GDN_CONV_DIM + GDN_V_W], _pad_cols(w_in[:, GDN_CONV_DIM + GDN_V_W:])],
                            axis=1).astype(BF16)
    qkv, z, sm = norm_matmul(x2d, g_pre, w_cat, (GDN_CONV_DIM, GDN_V_W, LANE))
    o = gdn_mixer(qkv.reshape(b, s, -1), z.reshape(b, s, -1), sm.reshape(b, s, -1), conv_w, dt_bias, a_log, norm_g)
    return outproj_residual([o.reshape(b * s, -1)], w_out.astype(BF16), x2d, g_post)


def kernel(x, positions, norm_mix_pre, norm_mix_post, norm_ffn_pre, norm_ffn_post, hy_w_in, hy_w_out, nsa_pe_k, nsa_pe_v, nsa_ck_w1, nsa_ck_b1, nsa_ck_w2, nsa_cv_w1, nsa_cv_b1, nsa_cv_w2, ssd_conv_w, ssd_conv_b, ssd_dt_bias, ssd_a_log, ssd_d, ssd_norm, gdn_w_in, gdn_conv_w, gdn_dt_bias, gdn_a_log, gdn_norm, gdn_w_out, ffn_w_up, ffn_conv_w, ffn_conv_b, ffn_w_down):
    b, s, d = x.shape
    x2d = x.reshape(b * s, d)
    depth = norm_mix_pre.shape[0]
    for layer in range(depth):
        e = layer // 2
        if layer % 2 == 0:
            x2d = hybrid_layer(x2d, b, s, positions, norm_mix_pre[layer], norm_mix_post[layer], hy_w_in[e],
                               hy_w_out[e], nsa_pe_k[e], nsa_pe_v[e], nsa_ck_w1[e], nsa_ck_b1[e], nsa_ck_w2[e],
                               nsa_cv_w1[e], nsa_cv_b1[e], nsa_cv_w2[e], ssd_conv_w[e], ssd_conv_b[e],
                               ssd_dt_bias[e], ssd_a_log[e], ssd_d[e], ssd_norm[e])
        else:
            x2d = gdn_layer(x2d, b, s, norm_mix_pre[layer], norm_mix_post[layer], gdn_w_in[e], gdn_conv_w[e],
                            gdn_dt_bias[e], gdn_a_log[e], gdn_norm[e], gdn_w_out[e])
        x2d = conv_ffn(x2d, s, norm_ffn_pre[layer], ffn_w_up[layer].astype(BF16), ffn_conv_w[layer],
                       ffn_conv_b[layer], ffn_w_down[layer].astype(BF16), norm_ffn_post[layer])
    return x2d.reshape(b, s, d)
```

```python
import functools
import math

import numpy as np
import jax
import jax.numpy as jnp
from jax import lax
from jax.experimental import pallas as pl
from jax.experimental.pallas import tpu as pltpu

F32 = jnp.float32
BF16 = jnp.bfloat16

D_MODEL = 1024
NORM_EPS = 1e-6
MASKED = -1e30

NSA_HEADS = 8
NSA_GROUPS = 2
NSA_REP = NSA_HEADS // NSA_GROUPS
NSA_DH = 64
CMP_BLOCK = 32
CMP_STRIDE = 16
SLC_BLOCK = 64
SLC_TOPK = 16
WINDOW = 512
ROPE_THETA = 500000.0
ROT_DIM = NSA_DH // 4
FORCED_SCORE = 1e9

SSD_HEADS = 8
SSD_P = 64
SSD_INNER = SSD_HEADS * SSD_P
SSD_GROUPS = 2
SSD_N = 128
SSD_CHUNK = 256
SSD_CONV_DIM = SSD_INNER + 2 * SSD_GROUPS * SSD_N

GDN_HEADS = 8
GDN_DK = 128
GDN_DV = 128
GDN_CHUNK = 64
GDN_QK_W = GDN_HEADS * GDN_DK
GDN_V_W = GDN_HEADS * GDN_DV
GDN_CONV_DIM = 2 * GDN_QK_W + GDN_V_W

FFN_DIM = 2816

LANE = 128
CONV_HALO = 8
FFN_HALO = 16
VMEM_LIMIT = 56 * 1024 * 1024


def _cparams(sem):
    return pltpu.CompilerParams(dimension_semantics=sem, vmem_limit_bytes=VMEM_LIMIT)


def _rms(x, g):
    return x * lax.rsqrt(jnp.mean(x * x, axis=-1, keepdims=True) + NORM_EPS) * g


def _silu(x):
    return x * jax.nn.sigmoid(x)


def _softplus(x):
    return jnp.maximum(x, 0.0) + jnp.log1p(jnp.exp(-jnp.abs(x)))


def _dot(a, b):
    return jnp.dot(a, b, preferred_element_type=F32)


def _dot_nt(a, b):
    return lax.dot_general(a, b, (((1,), (1,)), ((), ())), preferred_element_type=F32)


def _dot_tn(a, b):
    return lax.dot_general(a, b, (((0,), (0,)), ((), ())), preferred_element_type=F32)


def _split3(x):
    x1 = x.astype(BF16)
    r1 = x - x1.astype(F32)
    x2 = r1.astype(BF16)
    x3 = (r1 - x2.astype(F32)).astype(BF16)
    return x1, x2, x3


def _split2(x):
    x1 = x.astype(BF16)
    x2 = (x - x1.astype(F32)).astype(BF16)
    return x1, x2


def _cumsum_cols(tril, a):
    a1, a2, a3 = _split3(a)
    return _dot(tril, a1) + _dot(tril, a2) + _dot(tril, a3)


def _cumsum_rows_t(a, triu):
    a1, a2, a3 = _split3(a)
    return _dot_tn(a1, triu) + _dot_tn(a2, triu) + _dot_tn(a3, triu)


def _dot_f32(a, b):
    a1, a2 = _split2(a)
    b1, b2 = _split2(b)
    return _dot(a1, b1) + _dot(a1, b2) + _dot(a2, b1)


def _causal_conv(ext, cw, halo):
    width = cw.shape[0]
    y = ext[halo:] * cw[width - 1:width]
    for k in range(1, width):
        y = y + pltpu.roll(ext, k, 0)[halo:] * cw[width - 1 - k:width - k]
    return y


def _inproj_kernel(x_ref, g_ref, w_ref, *out_refs, splits):
    hn = _rms(x_ref[...], g_ref[...]).astype(BF16)
    y = _dot(hn, w_ref[...])
    off = 0
    for o_ref, n in zip(out_refs, splits):
        o_ref[...] = y[:, off:off + n].astype(o_ref.dtype)
        off += n


def norm_matmul(x2d, g, w, splits, tm=512):
    t = x2d.shape[0]
    n = w.shape[1]
    assert sum(splits) == n and t % tm == 0
    return pl.pallas_call(
        functools.partial(_inproj_kernel, splits=splits),
        grid=(t // tm,),
        in_specs=[pl.BlockSpec((tm, D_MODEL), lambda i: (i, 0)),
                  pl.BlockSpec((1, D_MODEL), lambda i: (0, 0)),
                  pl.BlockSpec((D_MODEL, n), lambda i: (0, 0))],
        out_specs=[pl.BlockSpec((tm, s), lambda i: (i, 0)) for s in splits],
        out_shape=[jax.ShapeDtypeStruct((t, s), F32) for s in splits],
        compiler_params=_cparams(("parallel",)),
        name="norm_matmul",
    )(x2d, g.reshape(1, D_MODEL), w)


def _outproj_kernel(*refs, n_in):
    a_refs = refs[:n_in]
    w_ref, x_ref, g_ref, o_ref = refs[n_in:]
    acc = None
    off = 0
    for a_ref in a_refs:
        k = a_ref.shape[-1]
        part = _dot(a_ref[...], w_ref[off:off + k, :])
        acc = part if acc is None else acc + part
        off += k
    o_ref[...] = x_ref[...] + _rms(acc, g_ref[...])


def outproj_residual(parts, w, x2d, g, tm=512):
    t = x2d.shape[0]
    return pl.pallas_call(
        functools.partial(_outproj_kernel, n_in=len(parts)),
        grid=(t // tm,),
        in_specs=[pl.BlockSpec((tm, p.shape[1]), lambda i: (i, 0)) for p in parts]
        + [pl.BlockSpec(w.shape, lambda i: (0, 0)),
           pl.BlockSpec((tm, D_MODEL), lambda i: (i, 0)),
           pl.BlockSpec((1, D_MODEL), lambda i: (0, 0))],
        out_specs=pl.BlockSpec((tm, D_MODEL), lambda i: (i, 0)),
        out_shape=jax.ShapeDtypeStruct((t, D_MODEL), F32),
        compiler_params=_cparams(("parallel",)),
        name="outproj_residual",
    )(*parts, w, x2d, g.reshape(1, D_MODEL))


FFN_CHUNK = 256
FFN_DOWN_CHUNKS = 2


def _ffn_kernel(x_ref, halo_ref, gpre_ref, wup_ref, cw_ref, cb_ref, wd_ref, gpost_ref, o_ref, hn_ref, *, tm, seq):
    keep = ((pl.program_id(0) * tm) % seq != 0).astype(F32)
    hn_ref[:FFN_HALO, :] = (_rms(halo_ref[...], gpre_ref[...]) * keep).astype(BF16)
    hn_ref[FFN_HALO:, :] = _rms(x_ref[...], gpre_ref[...]).astype(BF16)
    hn = hn_ref[...]

    def branch(lo):
        u = _dot(hn, wup_ref[:, lo:lo + FFN_CHUNK])
        return _causal_conv(u, cw_ref[:, lo:lo + FFN_CHUNK], FFN_HALO) + cb_ref[:, lo:lo + FFN_CHUNK]

    acc = None
    pending = []
    n_chunks = FFN_DIM // FFN_CHUNK
    for c in range(n_chunks):
        lo = c * FFN_CHUNK
        pending.append((_silu(branch(lo)) * branch(FFN_DIM + lo)).astype(BF16))
        if len(pending) == FFN_DOWN_CHUNKS or c == n_chunks - 1:
            width = len(pending) * FFN_CHUNK
            h = pending[0] if len(pending) == 1 else jnp.concatenate(pending, axis=1)
            part = _dot(h, wd_ref[lo + FFN_CHUNK - width:lo + FFN_CHUNK, :])
            acc = part if acc is None else acc + part
            pending = []
    o_ref[...] = x_ref[...] + _rms(acc, gpost_ref[...])


def conv_ffn(x2d, seq, gpre, w_up, conv_w, conv_b, w_down, gpost, tm=512):
    t = x2d.shape[0]
    assert FFN_DIM % FFN_CHUNK == 0 and t % tm == 0 and seq % tm == 0 and tm % FFN_HALO == 0
    hb = tm // FFN_HALO

    def resident(a):
        return pl.BlockSpec(a.shape, lambda i: (0,) * a.ndim, pipeline_mode=pl.Buffered(1))

    consts = (gpre.reshape(1, D_MODEL), w_up, conv_w, conv_b.reshape(1, 2 * FFN_DIM), w_down,
              gpost.reshape(1, D_MODEL))
    return pl.pallas_call(
        functools.partial(_ffn_kernel, tm=tm, seq=seq),
        grid=(t // tm,),
        in_specs=[pl.BlockSpec((tm, D_MODEL), lambda i: (i, 0)),
                  pl.BlockSpec((FFN_HALO, D_MODEL), lambda i: (jnp.maximum(i * hb - 1, 0), 0))]
        + [resident(a) for a in consts],
        out_specs=pl.BlockSpec((tm, D_MODEL), lambda i: (i, 0)),
        out_shape=jax.ShapeDtypeStruct((t, D_MODEL), F32),
        scratch_shapes=[pltpu.VMEM((tm + FFN_HALO, D_MODEL), BF16)],
        compiler_params=_cparams(("parallel",)),
        name="conv_ffn",
    )(x2d, x2d, *consts)


def _nsa_prep_kernel(q_ref, kv_ref, pos_ref, freq_ref, sign_ref,
                     qh_ref, kcmp_ref, vcmp_ref, kslc_ref, vslc_ref, kwin_ref, vwin_ref):
    pos = pos_ref[0].astype(F32)
    ang = pos * freq_ref[...]
    cs = jnp.cos(ang)
    sn = jnp.sin(ang) * sign_ref[...]
    lane = lax.broadcasted_iota(jnp.int32, (1, LANE), 1) % NSA_DH
    first_half = lane < ROT_DIM // 2

    def rope(x):
        partner = jnp.where(first_half, pltpu.roll(x, LANE - ROT_DIM // 2, 1), pltpu.roll(x, ROT_DIM // 2, 1))
        return x * cs + partner * sn

    ts = q_ref.shape[1]
    aug_w = kslc_ref.shape[-1]
    low_half = lax.broadcasted_iota(jnp.int32, (1, LANE), 1) < NSA_DH

    def widen(x, upper, fill):
        low = jnp.where(low_half, pltpu.roll(x, NSA_DH, 1) if upper else x, fill[:, :LANE])
        return low if aug_w == LANE else jnp.concatenate([low, fill[:, LANE:]], axis=1)

    scale = NSA_DH ** -0.5 * math.log2(math.e)
    for j in range(NSA_HEADS // 2):
        t = (rope(q_ref[0, :, j * LANE:(j + 1) * LANE]) * scale).T
        qh_ref[0, 2 * j] = t[:NSA_DH].astype(BF16)
        qh_ref[0, 2 * j + 1] = t[NSA_DH:].astype(BF16)

    tok = pl.program_id(1) * ts + lax.broadcasted_iota(jnp.int32, (ts, 1), 0)
    block_onehot = jnp.where(lax.broadcasted_iota(jnp.int32, (1, aug_w), 1) - NSA_DH == tok // SLC_BLOCK, 1.0, 0.0)
    t = rope(kv_ref[0, :, 2 * LANE:3 * LANE])
    for g in range(NSA_GROUPS):
        kslc_ref[0, g] = widen(t, g == 1, block_onehot).astype(BF16)

    for i, o_ref in ((0, kcmp_ref), (4, kwin_ref), (1, vcmp_ref)):
        t = kv_ref[0, :, i * LANE:(i + 1) * LANE]
        if i % 2 == 0:
            t = rope(t)
        for g in range(NSA_GROUPS):
            o_ref[0, g] = t[:, g * NSA_DH:(g + 1) * NSA_DH].astype(o_ref.dtype)
    for i, o_ref in ((3, vslc_ref), (5, vwin_ref)):
        t = kv_ref[0, :, i * LANE:(i + 1) * LANE].T
        for g in range(NSA_GROUPS):
            o_ref[0, g] = t[g * NSA_DH:(g + 1) * NSA_DH, :].astype(o_ref.dtype)


def _aug_width(seq):
    return -(-(NSA_DH + seq // SLC_BLOCK) // LANE) * LANE


def nsa_prep(q, kv, positions, ts=256):
    b, s, _ = q.shape
    inv_freq = ROPE_THETA ** (-jnp.arange(0, ROT_DIM, 2, dtype=F32) / ROT_DIM)
    head_freq = jnp.concatenate([inv_freq, inv_freq, jnp.zeros((NSA_DH - ROT_DIM,), F32)])
    freq = jnp.tile(head_freq, LANE // NSA_DH).reshape(1, LANE)
    head_sign = np.zeros((NSA_DH,), np.float32)
    head_sign[:ROT_DIM // 2] = -1.0
    head_sign[ROT_DIM // 2:ROT_DIM] = 1.0
    sign = jnp.asarray(np.tile(head_sign, LANE // NSA_DH).reshape(1, LANE))
    tok_shape = (b, NSA_GROUPS, s, NSA_DH)
    tok_spec = pl.BlockSpec((1, NSA_GROUPS, ts, NSA_DH), lambda bi, i: (bi, 0, i, 0))
    feat_shape = (b, NSA_GROUPS, NSA_DH, s)
    feat_spec = pl.BlockSpec((1, NSA_GROUPS, NSA_DH, ts), lambda bi, i: (bi, 0, 0, i))
    aug_w = _aug_width(s)
    return pl.pallas_call(
        _nsa_prep_kernel,
        grid=(b, s // ts),
        in_specs=[pl.BlockSpec((1, ts, q.shape[2]), lambda bi, i: (bi, i, 0)),
                  pl.BlockSpec((1, ts, kv.shape[2]), lambda bi, i: (bi, i, 0)),
                  pl.BlockSpec((1, ts, 1), lambda bi, i: (bi, i, 0)),
                  pl.BlockSpec((1, LANE), lambda bi, i: (0, 0)),
                  pl.BlockSpec((1, LANE), lambda bi, i: (0, 0))],
        out_specs=[pl.BlockSpec((1, NSA_HEADS, NSA_DH, ts), lambda bi, i: (bi, 0, 0, i)),
                   tok_spec, tok_spec,
                   pl.BlockSpec((1, NSA_GROUPS, ts, aug_w), lambda bi, i: (bi, 0, i, 0)),
                   feat_spec, tok_spec, feat_spec],
        out_shape=[jax.ShapeDtypeStruct((b, NSA_HEADS, NSA_DH, s), BF16),
                   jax.ShapeDtypeStruct(tok_shape, F32), jax.ShapeDtypeStruct(tok_shape, F32),
                   jax.ShapeDtypeStruct((b, NSA_GROUPS, s, aug_w), BF16), jax.ShapeDtypeStruct(feat_shape, BF16),
                   jax.ShapeDtypeStruct(tok_shape, BF16), jax.ShapeDtypeStruct(feat_shape, BF16)],
        compiler_params=_cparams(("parallel", "parallel")),
        name="nsa_prep",
    )(q, kv, positions.reshape(b, s, 1), freq, sign)


def _compress_kernel(k_ref, v_ref, pek_ref, pev_ref, kw1_ref, vw1_ref, kb1_ref, vb1_ref, kw2_ref, vw2_ref,
                     kc_ref, vc_ref):
    half = CMP_STRIDE * NSA_DH

    def mlp(t_ref, pe_ref, w1_ref, b1_ref, w2_ref):
        a = t_ref[0, 0]
        n = a.shape[0]
        h_lo = _dot((a + pe_ref[:, :half]).astype(BF16), w1_ref[:half, :])
        h_hi = _dot((a + pe_ref[:, half:]).astype(BF16), w1_ref[half:, :])
        hid = _silu(h_lo + pltpu.roll(h_hi, n - 1, 0) + b1_ref[...])
        return _dot(hid.astype(BF16), w2_ref[...])

    kc_ref[0, 0] = mlp(k_ref, pek_ref, kw1_ref, kb1_ref, kw2_ref).astype(kc_ref.dtype)
    vc_ref[0, 0] = mlp(v_ref, pev_ref, vw1_ref, vb1_ref, vw2_ref).astype(vc_ref.dtype).T


def nsa_compress(kcmp, vcmp, pe_k, pe_v, ck_w1, ck_b1, ck_w2, cv_w1, cv_b1, cv_w2):
    b, g, s, dh = kcmp.shape
    n = s // CMP_STRIDE
    wide = CMP_STRIDE * dh
    assert CMP_BLOCK == 2 * CMP_STRIDE
    kr = kcmp.reshape(b, g, n, wide)
    vr = vcmp.reshape(b, g, n, wide)
    blk = pl.BlockSpec((1, 1, n, wide), lambda bi, gi: (bi, gi, 0, 0))

    def full(a):
        return pl.BlockSpec(a.shape, lambda bi, gi: (0,) * a.ndim)

    args = (pe_k.reshape(1, 2 * wide), pe_v.reshape(1, 2 * wide), ck_w1.astype(BF16), cv_w1.astype(BF16),
            ck_b1.reshape(1, dh), cv_b1.reshape(1, dh), ck_w2.astype(BF16), cv_w2.astype(BF16))
    return pl.pallas_call(
        _compress_kernel,
        grid=(b, g),
        in_specs=[blk, blk] + [full(a) for a in args],
        out_specs=[pl.BlockSpec((1, 1, n, dh), lambda bi, gi: (bi, gi, 0, 0)),
                   pl.BlockSpec((1, 1, dh, n), lambda bi, gi: (bi, gi, 0, 0))],
        out_shape=[jax.ShapeDtypeStruct((b, g, n, dh), BF16), jax.ShapeDtypeStruct((b, g, dh, n), BF16)],
        compiler_params=_cparams(("parallel", "parallel")),
        name="nsa_compress",
    )(kr, vr, *args)


ATT_TQ = 256
ATT_TK = 256
ATT_RING = 4


def _att_kernel(q_ref, kc_ref, vc_ref, ks_ref, vs_ref, kw_ref, vw_ref, gate_ref, ov_ref, o_ref,
                *s_refs, seq):
    tq, tk, rep, dh = ATT_TQ, ATT_TK, NSA_REP, NSA_DH
    cols = rep * tq
    nblk = seq // SLC_BLOCK
    ncmp = kc_ref.shape[2]
    aug_w = ks_ref.shape[-1]
    t0 = pl.program_id(2) * tq
    q = jnp.concatenate([q_ref[0, r] for r in range(rep)], axis=1)
    tpos = t0 + lax.broadcasted_iota(jnp.int32, (1, tq), 1)

    def per_head(x):
        return jnp.concatenate([x] * rep, axis=1)


    cmp_end = lax.broadcasted_iota(jnp.int32, (ncmp, 1), 0) * CMP_STRIDE + (CMP_BLOCK - 1)
    s_c = _dot(kc_ref[0, 0], q) + per_head(jnp.where(cmp_end <= tpos, 0.0, MASKED))
    p_c = jnp.exp2(s_c - jnp.max(s_c, axis=0, keepdims=True))
    l_c = jnp.sum(p_c, axis=0, keepdims=True)
    p_c = p_c * (per_head(jnp.where(tpos >= CMP_BLOCK - 1, 1.0, 0.0)) / l_c)
    o_c = _dot(vc_ref[0, 0], p_c.astype(BF16))

    p_sum = p_c[:, :tq]
    for r in range(1, rep):
        p_sum = p_sum + p_c[:, r * tq:(r + 1) * tq]
    p_hi, p_lo = _split2(p_sum)
    imp = _dot(ov_ref[...], p_hi) + _dot(ov_ref[...], p_lo)
    blk = lax.broadcasted_iota(jnp.int32, (nblk, 1), 0)
    cur = tpos // SLC_BLOCK
    forced = (blk == 0) | (blk == cur) | (blk == cur - 1)
    imp = jnp.where(forced, FORCED_SCORE, jnp.where(blk <= cur, imp, -1.0))
    sel = jnp.full((nblk, tq), MASKED, F32)
    for _ in range(min(SLC_TOPK, nblk)):
        best = jnp.max(imp, axis=0, keepdims=True)
        first = jnp.min(jnp.where(imp == best, blk, nblk), axis=0, keepdims=True)
        hit = blk == first
        sel = jnp.where(hit, 0.0, sel)
        imp = jnp.where(hit, -jnp.inf, imp)

    mask_rows = jnp.concatenate([sel.astype(BF16), jnp.zeros((aug_w - dh - nblk, tq), BF16)], axis=0)
    q_aug = jnp.concatenate([q, per_head(mask_rows)], axis=0)
    kidx = lax.broadcasted_iota(jnp.int32, (tk, 1), 0)

    def produce(kt, s_ref):
        k0 = pl.multiple_of(kt * tk, tk)
        s = _dot(ks_ref[0, 0, pl.ds(k0, tk), :], q_aug)
        s_ref[...] = s
        return jnp.max(s, axis=0, keepdims=True)

    def consume(kt, s_ref, tile_max, carry, causal):
        m, l, acc = carry
        k0 = pl.multiple_of(kt * tk, tk)
        v = vs_ref[0, 0, :, pl.ds(k0, tk)]
        s = s_ref[...]
        if causal:
            s = s + per_head(jnp.where(k0 + kidx <= tpos, 0.0, MASKED))
            tile_max = jnp.max(s, axis=0, keepdims=True)
        m_new = jnp.maximum(m, tile_max)
        alpha = jnp.exp2(m - m_new)
        p = jnp.exp2(s - m_new)
        l = alpha * l + jnp.sum(p, axis=0, keepdims=True)
        return m_new, l, alpha * acc + _dot(v, p.astype(BF16))

    nbuf = len(s_refs)

    def slc_trip(j, carry):
        maxes, state = list(carry[:nbuf]), carry[nbuf:]
        for i, s_ref in enumerate(s_refs):
            state = consume(nbuf * j + i, s_ref, maxes[i], state, False)
            maxes[i] = produce(nbuf * (j + 1) + i, s_ref)
        return tuple(maxes) + state

    n_full = t0 // (nbuf * tk)
    init = tuple(produce(i, s_ref) for i, s_ref in enumerate(s_refs)) + (
        jnp.full((1, cols), MASKED, F32), jnp.zeros((1, cols), F32), jnp.zeros((dh, cols), F32))
    carry = lax.fori_loop(0, n_full, slc_trip, init)
    maxes, state = carry[:nbuf], carry[nbuf:]
    own = (t0 - n_full * nbuf * tk) // tk
    for i, s_ref in enumerate(s_refs):
        kt = nbuf * n_full + i
        state = lax.cond(
            i < own,
            lambda st, kt=kt, s_ref=s_ref, mx=maxes[i]: consume(kt, s_ref, mx, st, False),
            lambda st, kt=kt, s_ref=s_ref: lax.cond(
                i == own, lambda st2: consume(kt, s_ref, None, st2, True), lambda st2: st2, st),
            state)
    _, l_s, acc_s = state
    o_s = acc_s / l_s

    band = WINDOW + tq
    start = pl.multiple_of(jnp.maximum(t0 - WINDOW, 0), tq)
    dlt = tpos - (start + lax.broadcasted_iota(jnp.int32, (band, 1), 0))
    bias_w = jnp.where((dlt >= 0) & (dlt < WINDOW), 0.0, MASKED)
    s_w = _dot(kw_ref[0, 0, pl.ds(start, band), :], q) + per_head(bias_w)
    p_w = jnp.exp2(s_w - jnp.max(s_w, axis=0, keepdims=True))
    l_w = jnp.sum(p_w, axis=0, keepdims=True)
    o_w = _dot(vw_ref[0, 0, :, pl.ds(start, band)], p_w.astype(BF16)) / l_w

    gates = jax.nn.sigmoid(gate_ref[0]).T
    merged = []
    for r in range(rep):
        sl = slice(r * tq, (r + 1) * tq)
        merged.append(gates[3 * r:3 * r + 1] * o_c[:, sl] + gates[3 * r + 1:3 * r + 2] * o_s[:, sl]
                      + gates[3 * r + 2:3 * r + 3] * o_w[:, sl])
    o_ref[0] = jnp.concatenate(merged, axis=0).T.astype(o_ref.dtype)


def _overlap_matrix(seq):
    ncp = seq // CMP_STRIDE
    n_cmp = (seq - CMP_BLOCK) // CMP_STRIDE + 1
    nblk = seq // SLC_BLOCK
    cs = np.arange(ncp) * CMP_STRIDE
    ss = np.arange(nblk) * SLC_BLOCK
    ov = ((cs[None, :] <= ss[:, None] + SLC_BLOCK - 1) & (cs[None, :] + CMP_BLOCK - 1 >= ss[:, None])
          & (np.arange(ncp)[None, :] < n_cmp))
    return jnp.asarray(ov.astype(np.float32), dtype=BF16)


def nsa_attention(qh, kc, vc, kslc, vslc, kwin, vwin, gates):
    b, _, dh, s = qh.shape
    aug_w = kslc.shape[-1]
    g = NSA_GROUPS
    ncp = kc.shape[2]
    nblk = s // SLC_BLOCK
    assert s % (ATT_RING * ATT_TK) == 0 and s >= WINDOW + ATT_TQ and aug_w == _aug_width(s)
    ov = _overlap_matrix(s)
    tok_spec = pl.BlockSpec((1, 1, s, dh), lambda bi, gi, i: (bi, gi, 0, 0))
    feat_spec = pl.BlockSpec((1, 1, dh, s), lambda bi, gi, i: (bi, gi, 0, 0))
    return pl.pallas_call(
        functools.partial(_att_kernel, seq=s),
        grid=(b, g, s // ATT_TQ),
        in_specs=[pl.BlockSpec((1, NSA_REP, dh, ATT_TQ), lambda bi, gi, i: (bi, gi, 0, i)),
                  pl.BlockSpec((1, 1, ncp, dh), lambda bi, gi, i: (bi, gi, 0, 0)),
                  pl.BlockSpec((1, 1, dh, ncp), lambda bi, gi, i: (bi, gi, 0, 0)),
                  pl.BlockSpec((1, 1, s, aug_w), lambda bi, gi, i: (bi, gi, 0, 0)),
                  feat_spec, tok_spec, feat_spec,
                  pl.BlockSpec((1, ATT_TQ, LANE), lambda bi, gi, i: (bi, i, gi)),
                  pl.BlockSpec((nblk, ncp), lambda bi, gi, i: (0, 0))],
        out_specs=pl.BlockSpec((1, ATT_TQ, NSA_REP * dh), lambda bi, gi, i: (bi, i, gi)),
        out_shape=jax.ShapeDtypeStruct((b, s, NSA_HEADS * dh), BF16),
        scratch_shapes=[pltpu.VMEM((ATT_TK, NSA_REP * ATT_TQ), F32)] * ATT_RING,
        compiler_params=_cparams(("parallel", "parallel", "arbitrary")),
        name="nsa_attention",
    )(qh, kc, vc, kslc, vslc, kwin, vwin, gates, ov)


def _ssd_kernel(z_ref, xbc_ref, dt_ref, cw_ref, cb_ref, dtb_ref, alog_ref, drow_ref, ng_ref, tril_ref, triu_ref,
                o_ref, carry_ref, state_ref, y_ref):
    cl = SSD_CHUNK
    hpg = SSD_HEADS // SSD_GROUPS

    @pl.when(pl.program_id(1) == 0)
    def _():
        carry_ref[...] = jnp.zeros_like(carry_ref)
        state_ref[...] = jnp.zeros_like(state_ref)

    xbc = xbc_ref[0]
    ext = jnp.concatenate([carry_ref[...], xbc], axis=0)
    act = _silu(_causal_conv(ext, cw_ref[...], CONV_HALO) + cb_ref[...])
    carry_ref[...] = xbc[cl - CONV_HALO:, :]

    xs = act[:, :SSD_INNER]
    dt = _softplus(dt_ref[0] + dtb_ref[...])
    a = dt * (-jnp.exp(alog_ref[...]))
    acs_c = _cumsum_cols(tril_ref[...], a)
    acs_r = _cumsum_rows_t(a, triu_ref[...])
    causal = lax.broadcasted_iota(jnp.int32, (cl, cl), 0) >= lax.broadcasted_iota(jnp.int32, (cl, cl), 1)

    for g in range(SSD_GROUPS):
        bg = act[:, SSD_INNER + g * SSD_N:SSD_INNER + (g + 1) * SSD_N].astype(BF16)
        cg = act[:, SSD_INNER + (SSD_GROUPS + g) * SSD_N:SSD_INNER + (SSD_GROUPS + g + 1) * SSD_N].astype(BF16)
        cb = _dot_nt(cg, bg)
        state = state_ref[g]
        y_off = _dot(cg, state.astype(BF16))
        weighted = []
        decay = []
        for hl in range(hpg):
            h = g * hpg + hl
            col = acs_c[:, h:h + 1]
            row = acs_r[h:h + 1, :]
            lmat = jnp.exp(jnp.where(causal, col - row, MASKED))
            xh = xs[:, h * SSD_P:(h + 1) * SSD_P]
            xdt = xh * dt[:, h:h + 1]
            y_diag = _dot((cb * lmat).astype(BF16), xdt.astype(BF16))
            y_ref[:, h * SSD_P:(h + 1) * SSD_P] = (y_diag + y_off[:, hl * SSD_P:(hl + 1) * SSD_P] * jnp.exp(col)
                                                   + drow_ref[:, h * SSD_P:(h + 1) * SSD_P] * xh)
            a_last = acs_c[cl - 1:cl, h:h + 1]
            weighted.append((xdt * jnp.exp(a_last - col)).astype(BF16))
            decay.append(jnp.broadcast_to(jnp.exp(a_last), (1, SSD_P)))
        contrib = _dot_tn(bg, jnp.concatenate(weighted, axis=1))
        state_ref[g] = state * jnp.concatenate(decay, axis=1) + contrib

    y = y_ref[...] * _silu(z_ref[0])
    gw = SSD_INNER // SSD_GROUPS
    for g in range(SSD_GROUPS):
        o_ref[0, :, g * gw:(g + 1) * gw] = _rms(y[:, g * gw:(g + 1) * gw],
                                                 ng_ref[:, g * gw:(g + 1) * gw]).astype(o_ref.dtype)


def _tri(n, upper):
    m = np.triu(np.ones((n, n), np.float32)) if upper else np.tril(np.ones((n, n), np.float32))
    return jnp.asarray(m, dtype=BF16)


def _lane_row(v, offset=0):
    row = jnp.zeros((1, LANE), F32)
    return row.at[0, offset:offset + v.shape[0]].set(v)


def ssd_mixer(z, xbc, dts, conv_w, conv_b, dt_bias, a_log, d_skip, norm_g):
    b, s, _ = z.shape
    cl = SSD_CHUNK
    hpg = SSD_HEADS // SSD_GROUPS

    def full(a):
        return pl.BlockSpec(a.shape, lambda bi, c: (0,) * a.ndim)

    args = (conv_w, conv_b.reshape(1, -1), _lane_row(dt_bias), _lane_row(a_log),
            jnp.repeat(d_skip, SSD_P).reshape(1, SSD_INNER), norm_g.reshape(1, SSD_INNER),
            _tri(cl, False), _tri(cl, True))
    return pl.pallas_call(
        _ssd_kernel,
        grid=(b, s // cl),
        in_specs=[pl.BlockSpec((1, cl, SSD_INNER), lambda bi, c: (bi, c, 0)),
                  pl.BlockSpec((1, cl, SSD_CONV_DIM), lambda bi, c: (bi, c, 0)),
                  pl.BlockSpec((1, cl, LANE), lambda bi, c: (bi, c, 0))] + [full(a) for a in args],
        out_specs=pl.BlockSpec((1, cl, SSD_INNER), lambda bi, c: (bi, c, 0)),
        out_shape=jax.ShapeDtypeStruct((b, s, SSD_INNER), BF16),
        scratch_shapes=[pltpu.VMEM((CONV_HALO, SSD_CONV_DIM), F32),
                        pltpu.VMEM((SSD_GROUPS, SSD_N, hpg * SSD_P), F32),
                        pltpu.VMEM((cl, SSD_INNER), F32)],
        compiler_params=_cparams(("parallel", "arbitrary")),
        name="ssd_mixer",
    )(z, xbc, dts, *args)


GDN_STEP_CHUNKS = 4


def _mm_split(a, b):
    return _dot(a[0], b[0]) + _dot(a[0], b[1]) + _dot(a[1], b[0])


def _unit_lower_inverses(a_list, row, col):
    eye = jnp.where(row == col, 1.0, 0.0)
    same16 = (row // 16) == (col // 16)
    same32 = (row // 32) == (col // 32)
    n = [_split2(jnp.where(same16, -a, 0.0)) for a in a_list]
    p = [eye + jnp.where(same16, -a, 0.0) for a in a_list]
    for _ in range(3):
        n = [_split2(_mm_split(x, x)) for x in n]
        p = [x + _mm_split(_split2(x), y) for x, y in zip(p, n)]
    for level_mask in (jnp.where(same16, 0.0, jnp.where(same32, 1.0, 0.0)), jnp.where(same32, 0.0, 1.0)):
        off = [_split2(a * level_mask) for a in a_list]
        ps = [_split2(x) for x in p]
        t = [_split2(_mm_split(x, y)) for x, y in zip(ps, off)]
        p = [x - _mm_split(y, z) for x, y, z in zip(p, t, ps)]
    return p


def _gdn_kernel(qkv_ref, z_ref, sm_ref, cw_ref, dtb_ref, alog_ref, ng_ref, tril_ref, triu_ref,
                o_ref, carry_ref, state_ref):
    cl = GDN_CHUNK
    rows = qkv_ref.shape[1]
    heads = range(GDN_HEADS)
    chunks = range(rows // cl)
    chains = [(c, h) for c in chunks for h in heads]

    @pl.when(pl.program_id(1) == 0)
    def _():
        carry_ref[...] = jnp.zeros_like(carry_ref)
        state_ref[...] = jnp.zeros_like(state_ref)

    qkv = qkv_ref[0]
    ext = jnp.concatenate([carry_ref[...], qkv], axis=0)
    act = _silu(_causal_conv(ext, cw_ref[...], CONV_HALO))
    carry_ref[...] = qkv[rows - CONV_HALO:, :]

    sm = sm_ref[0]
    beta_all = jax.nn.sigmoid(sm)
    g_all = -jnp.exp(alog_ref[...]) * _softplus(sm + dtb_ref[...])
    row = lax.broadcasted_iota(jnp.int32, (cl, cl), 0)
    col = lax.broadcasted_iota(jnp.int32, (cl, cl), 1)
    incl = row >= col

    def l2n(x):
        return x * lax.rsqrt(jnp.sum(x * x, axis=-1, keepdims=True) + NORM_EPS)

    def chunk(x, c):
        return x[c * cl:(c + 1) * cl]

    qn = [l2n(act[:, h * GDN_DK:(h + 1) * GDN_DK]) * GDN_DK ** -0.5 for h in heads]
    kn = [l2n(act[:, GDN_QK_W + h * GDN_DK:GDN_QK_W + (h + 1) * GDN_DK]) for h in heads]
    v = [act[:, 2 * GDN_QK_W + h * GDN_DV:2 * GDN_QK_W + (h + 1) * GDN_DV] for h in heads]
    beta = [beta_all[:, h:h + 1] for h in heads]
    kb = [kn[h] * beta[h] for h in heads]
    knb = [x.astype(BF16) for x in kn]
    gcs_c = [_cumsum_cols(tril_ref[...], chunk(g_all, c)) for c in chunks]
    gcs_r = [_cumsum_rows_t(chunk(g_all, c), triu_ref[...]) for c in chunks]
    gc = {(c, h): gcs_c[c][:, GDN_HEADS + h:GDN_HEADS + h + 1] for c, h in chains}
    dec = {(c, h): jnp.exp(jnp.where(incl, gc[c, h] - gcs_r[c][GDN_HEADS + h:GDN_HEADS + h + 1, :], MASKED))
           for c, h in chains}
    eg = {k: jnp.exp(g) for k, g in gc.items()}
    a_kk = [jnp.where(row > col, _dot_nt(chunk(kb[h], c).astype(BF16), chunk(knb[h], c)) * dec[c, h], 0.0)
            for c, h in chains]
    a_qk = {(c, h): (_dot_nt(chunk(qn[h], c).astype(BF16), chunk(knb[h], c)) * dec[c, h]).astype(BF16)
            for c, h in chains}
    t_inv = [t.astype(BF16) for t in _unit_lower_inverses(a_kk, row, col)]
    uw = {(c, h): _dot(t, jnp.concatenate([chunk(v[h], c) * chunk(beta[h], c), chunk(kb[h], c) * eg[c, h]],
                                          axis=1).astype(BF16))
          for (c, h), t in zip(chains, t_inv)}

    state = [state_ref[h] for h in heads]
    for c in chunks:
        sb = [s.astype(BF16) for s in state]
        ws = [_dot(jnp.concatenate([uw[c, h][:, GDN_DV:], chunk(qn[h], c) * eg[c, h]], axis=0).astype(BF16), sb[h])
              for h in heads]
        vb = [(uw[c, h][:, :GDN_DV] - ws[h][:cl]).astype(BF16) for h in heads]
        o = [ws[h][cl:] + _dot(a_qk[c, h], vb[h]) for h in heads]
        g_last = [gc[c, h][cl - 1:cl, :] for h in heads]
        state = [state[h] * jnp.exp(g_last[h])
                 + _dot_tn((chunk(kn[h], c) * jnp.exp(g_last[h] - gc[c, h])).astype(BF16), vb[h]) for h in heads]
        for h in heads:
            out = _rms(o[h], ng_ref[...]) * _silu(z_ref[0, c * cl:(c + 1) * cl, h * GDN_DV:(h + 1) * GDN_DV])
            o_ref[0, c * cl:(c + 1) * cl, h * GDN_DV:(h + 1) * GDN_DV] = out.astype(o_ref.dtype)
    for h in heads:
        state_ref[h] = state[h]


def gdn_mixer(qkv, z, sm, conv_w, dt_bias, a_log, norm_g):
    b, s, _ = qkv.shape
    cl = GDN_CHUNK

    def full(a):
        return pl.BlockSpec(a.shape, lambda bi, c: (0,) * a.ndim)

    args = (conv_w, _lane_row(dt_bias, GDN_HEADS), _lane_row(a_log, GDN_HEADS), norm_g.reshape(1, GDN_DV),
            _tri(cl, False), _tri(cl, True))
    rows = GDN_STEP_CHUNKS * cl
    assert s % rows == 0
    return pl.pallas_call(
        _gdn_kernel,
        grid=(b, s // rows),
        in_specs=[pl.BlockSpec((1, rows, GDN_CONV_DIM), lambda bi, c: (bi, c, 0)),
                  pl.BlockSpec((1, rows, GDN_V_W), lambda bi, c: (bi, c, 0)),
                  pl.BlockSpec((1, rows, LANE), lambda bi, c: (bi, c, 0))] + [full(a) for a in args],
        out_specs=pl.BlockSpec((1, rows, GDN_V_W), lambda bi, c: (bi, c, 0)),
        out_shape=jax.ShapeDtypeStruct((b, s, GDN_V_W), BF16),
        scratch_shapes=[pltpu.VMEM((CONV_HALO, GDN_CONV_DIM), F32),
                        pltpu.VMEM((GDN_HEADS, GDN_DK, GDN_DV), F32)],
        compiler_params=_cparams(("parallel", "arbitrary")),
        name="gdn_mixer",
    )(qkv, z, sm, *args)


def _pad_cols(w, width=LANE):
    return jnp.pad(w, ((0, 0), (0, width - w.shape[1])))


HY_Q_W = NSA_HEADS * NSA_DH
HY_KV_W = 6 * NSA_GROUPS * NSA_DH
HY_GATE_W = NSA_HEADS * 3


def hybrid_layer(x2d, b, s, positions, g_pre, g_post, w_in, w_out, pe_k, pe_v, ck_w1, ck_b1, ck_w2,
                 cv_w1, cv_b1, cv_w2, conv_w, conv_b, dt_bias, a_log, d_skip, norm_g):
    o = 0
    cols = {}
    for name, width in (("q", HY_Q_W), ("kv", HY_KV_W), ("gate", HY_GATE_W), ("z", SSD_INNER),
                        ("xbc", SSD_CONV_DIM), ("dt", SSD_HEADS)):
        cols[name] = w_in[:, o:o + width]
        o += width
    gpg = HY_GATE_W // NSA_GROUPS
    w_cat = jnp.concatenate(
        [cols["q"], cols["kv"], cols["z"], cols["xbc"]]
        + [_pad_cols(cols["gate"][:, g * gpg:(g + 1) * gpg]) for g in range(NSA_GROUPS)]
        + [_pad_cols(cols["dt"])], axis=1).astype(BF16)
    splits = (HY_Q_W, HY_KV_W, SSD_INNER, SSD_CONV_DIM, NSA_GROUPS * LANE, LANE)
    q, kv, z, xbc, gates, dts = norm_matmul(x2d, g_pre, w_cat, splits)

    qh, kcmp, vcmp, kslc, vslc, kwin, vwin = nsa_prep(q.reshape(b, s, -1), kv.reshape(b, s, -1), positions)
    kc, vc = nsa_compress(kcmp, vcmp, pe_k, pe_v, ck_w1, ck_b1, ck_w2, cv_w1, cv_b1, cv_w2)
    o_nsa = nsa_attention(qh, kc, vc, kslc, vslc, kwin, vwin, gates.reshape(b, s, -1))
    y = ssd_mixer(z.reshape(b, s, -1), xbc.reshape(b, s, -1), dts.reshape(b, s, -1),
                  conv_w, conv_b, dt_bias, a_log, d_skip, norm_g)
    return outproj_residual([o_nsa.reshape(b * s, -1), y.reshape(b * s, -1)], w_out.astype(BF16), x2d, g_post)


def gdn_layer(x2d, b, s, g_pre, g_post, w_in, conv_w, dt_bias, a_log, norm_g, w_out):
    w_cat = jnp.concatenate([w_in[:, :GDN_CONV_DIM + GDN_V_W], _pad_cols(w_in[:, GDN_CONV_DIM + GDN_V_W:])],
                            axis=1).astype(BF16)
    qkv, z, sm = norm_matmul(x2d, g_pre, w_cat, (GDN_CONV_DIM, GDN_V_W, LANE))
    o = gdn_mixer(qkv.reshape(b, s, -1), z.reshape(b, s, -1), sm.reshape(b, s, -1), conv_w, dt_bias, a_log, norm_g)
    return outproj_residual([o.reshape(b * s, -1)], w_out.astype(BF16), x2d, g_post)


def kernel(x, positions, norm_mix_pre, norm_mix_post, norm_ffn_pre, norm_ffn_post, hy_w_in, hy_w_out, nsa_pe_k, nsa_pe_v, nsa_ck_w1, nsa_ck_b1, nsa_ck_w2, nsa_cv_w1, nsa_cv_b1, nsa_cv_w2, ssd_conv_w, ssd_conv_b, ssd_dt_bias, ssd_a_log, ssd_d, ssd_norm, gdn_w_in, gdn_conv_w, gdn_dt_bias, gdn_a_log, gdn_norm, gdn_w_out, ffn_w_up, ffn_conv_w, ffn_conv_b, ffn_w_down):
    b, s, d = x.shape
    x2d = x.reshape(b * s, d)
    depth = norm_mix_pre.shape[0]
    for layer in range(depth):
        e = layer // 2
        if layer % 2 == 0:
            x2d = hybrid_layer(x2d, b, s, positions, norm_mix_pre[layer], norm_mix_post[layer], hy_w_in[e],
                               hy_w_out[e], nsa_pe_k[e], nsa_pe_v[e], nsa_ck_w1[e], nsa_ck_b1[e], nsa_ck_w2[e],
                               nsa_cv_w1[e], nsa_cv_b1[e], nsa_cv_w2[e], ssd_conv_w[e], ssd_conv_b[e],
                               ssd_dt_bias[e], ssd_a_log[e], ssd_d[e], ssd_norm[e])
        else:
            x2d = gdn_layer(x2d, b, s, norm_mix_pre[layer], norm_mix_post[layer], gdn_w_in[e], gdn_conv_w[e],
                            gdn_dt_bias[e], gdn_a_log[e], gdn_norm[e], gdn_w_out[e])
        x2d = conv_ffn(x2d, s, norm_ffn_pre[layer], ffn_w_up[layer].astype(BF16), ffn_conv_w[layer],
                       ffn_conv_b[layer], ffn_w_down[layer].astype(BF16), norm_ffn_post[layer])
    return x2d.reshape(b, s, d)
```

```python
import functools
import math

import numpy as np
import jax
import jax.numpy as jnp
from jax import lax
from jax.experimental import pallas as pl
from jax.experimental.pallas import tpu as pltpu

F32 = jnp.float32
BF16 = jnp.bfloat16

D_MODEL = 1024
NORM_EPS = 1e-6
MASKED = -1e30

NSA_HEADS = 8
NSA_GROUPS = 2
NSA_REP = NSA_HEADS // NSA_GROUPS
NSA_DH = 64
CMP_BLOCK = 32
CMP_STRIDE = 16
SLC_BLOCK = 64
SLC_TOPK = 16
WINDOW = 512
ROPE_THETA = 500000.0
ROT_DIM = NSA_DH // 4
FORCED_SCORE = 1e9

SSD_HEADS = 8
SSD_P = 64
SSD_INNER = SSD_HEADS * SSD_P
SSD_GROUPS = 2
SSD_N = 128
SSD_CHUNK = 256
SSD_CONV_DIM = SSD_INNER + 2 * SSD_GROUPS * SSD_N

GDN_HEADS = 8
GDN_DK = 128
GDN_DV = 128
GDN_CHUNK = 64
GDN_QK_W = GDN_HEADS * GDN_DK
GDN_V_W = GDN_HEADS * GDN_DV
GDN_CONV_DIM = 2 * GDN_QK_W + GDN_V_W

FFN_DIM = 2816

LANE = 128
CONV_HALO = 8
FFN_HALO = 16
VMEM_LIMIT = 56 * 1024 * 1024


def _cparams(sem):
    return pltpu.CompilerParams(dimension_semantics=sem, vmem_limit_bytes=VMEM_LIMIT)


def _rms(x, g):
    return x * lax.rsqrt(jnp.mean(x * x, axis=-1, keepdims=True) + NORM_EPS) * g


def _silu(x):
    return x * jax.nn.sigmoid(x)


def _softplus(x):
    return jnp.maximum(x, 0.0) + jnp.log1p(jnp.exp(-jnp.abs(x)))


def _dot(a, b):
    return jnp.dot(a, b, preferred_element_type=F32)


def _dot_nt(a, b):
    return lax.dot_general(a, b, (((1,), (1,)), ((), ())), preferred_element_type=F32)


def _dot_tn(a, b):
    return lax.dot_general(a, b, (((0,), (0,)), ((), ())), preferred_element_type=F32)


def _split3(x):
    x1 = x.astype(BF16)
    r1 = x - x1.astype(F32)
    x2 = r1.astype(BF16)
    x3 = (r1 - x2.astype(F32)).astype(BF16)
    return x1, x2, x3


def _split2(x):
    x1 = x.astype(BF16)
    x2 = (x - x1.astype(F32)).astype(BF16)
    return x1, x2


def _cumsum_cols(tril, a):
    a1, a2, a3 = _split3(a)
    return _dot(tril, a1) + _dot(tril, a2) + _dot(tril, a3)


def _cumsum_rows_t(a, triu):
    a1, a2, a3 = _split3(a)
    return _dot_tn(a1, triu) + _dot_tn(a2, triu) + _dot_tn(a3, triu)


def _dot_f32(a, b):
    a1, a2 = _split2(a)
    b1, b2 = _split2(b)
    return _dot(a1, b1) + _dot(a1, b2) + _dot(a2, b1)


def _causal_conv(ext, cw, halo):
    width = cw.shape[0]
    y = ext[halo:] * cw[width - 1:width]
    for k in range(1, width):
        y = y + pltpu.roll(ext, k, 0)[halo:] * cw[width - 1 - k:width - k]
    return y


def _inproj_kernel(x_ref, g_ref, w_ref, *out_refs, splits):
    hn = _rms(x_ref[...], g_ref[...]).astype(BF16)
    y = _dot(hn, w_ref[...])
    off = 0
    for o_ref, n in zip(out_refs, splits):
        o_ref[...] = y[:, off:off + n].astype(o_ref.dtype)
        off += n


def norm_matmul(x2d, g, w, splits, tm=512):
    t = x2d.shape[0]
    n = w.shape[1]
    assert sum(splits) == n and t % tm == 0
    return pl.pallas_call(
        functools.partial(_inproj_kernel, splits=splits),
        grid=(t // tm,),
        in_specs=[pl.BlockSpec((tm, D_MODEL), lambda i: (i, 0)),
                  pl.BlockSpec((1, D_MODEL), lambda i: (0, 0)),
                  pl.BlockSpec((D_MODEL, n), lambda i: (0, 0))],
        out_specs=[pl.BlockSpec((tm, s), lambda i: (i, 0)) for s in splits],
        out_shape=[jax.ShapeDtypeStruct((t, s), F32) for s in splits],
        compiler_params=_cparams(("parallel",)),
        name="norm_matmul",
    )(x2d, g.reshape(1, D_MODEL), w)


def _outproj_kernel(*refs, n_in):
    a_refs = refs[:n_in]
    w_ref, x_ref, g_ref, o_ref = refs[n_in:]
    acc = None
    off = 0
    for a_ref in a_refs:
        k = a_ref.shape[-1]
        part = _dot(a_ref[...], w_ref[off:off + k, :])
        acc = part if acc is None else acc + part
        off += k
    o_ref[...] = x_ref[...] + _rms(acc, g_ref[...])


def outproj_residual(parts, w, x2d, g, tm=512):
    t = x2d.shape[0]
    return pl.pallas_call(
        functools.partial(_outproj_kernel, n_in=len(parts)),
        grid=(t // tm,),
        in_specs=[pl.BlockSpec((tm, p.shape[1]), lambda i: (i, 0)) for p in parts]
        + [pl.BlockSpec(w.shape, lambda i: (0, 0)),
           pl.BlockSpec((tm, D_MODEL), lambda i: (i, 0)),
           pl.BlockSpec((1, D_MODEL), lambda i: (0, 0))],
        out_specs=pl.BlockSpec((tm, D_MODEL), lambda i: (i, 0)),
        out_shape=jax.ShapeDtypeStruct((t, D_MODEL), F32),
        compiler_params=_cparams(("parallel",)),
        name="outproj_residual",
    )(*parts, w, x2d, g.reshape(1, D_MODEL))


FFN_CHUNK = 256
FFN_DOWN_CHUNKS = 2


def _ffn_kernel(x_ref, halo_ref, gpre_ref, wup_ref, cw_ref, cb_ref, wd_ref, gpost_ref, o_ref, hn_ref, *, tm, seq):
    keep = ((pl.program_id(0) * tm) % seq != 0).astype(F32)
    hn_ref[:FFN_HALO, :] = (_rms(halo_ref[...], gpre_ref[...]) * keep).astype(BF16)
    hn_ref[FFN_HALO:, :] = _rms(x_ref[...], gpre_ref[...]).astype(BF16)
    hn = hn_ref[...]

    def up(lo):
        return _dot(hn, wup_ref[:, lo:lo + FFN_CHUNK]), _dot(hn, wup_ref[:, FFN_DIM + lo:FFN_DIM + lo + FFN_CHUNK])

    def conv(u, lo):
        return _causal_conv(u, cw_ref[:, lo:lo + FFN_CHUNK], FFN_HALO) + cb_ref[:, lo:lo + FFN_CHUNK]

    acc = None
    pending = []
    n_chunks = FFN_DIM // FFN_CHUNK
    ahead = up(0)
    for c in range(n_chunks):
        lo = c * FFN_CHUNK
        u_gate, u_val = ahead
        if c + 1 < n_chunks:
            ahead = up(lo + FFN_CHUNK)
        pending.append((_silu(conv(u_gate, lo)) * conv(u_val, FFN_DIM + lo)).astype(BF16))
        if len(pending) == FFN_DOWN_CHUNKS or c == n_chunks - 1:
            width = len(pending) * FFN_CHUNK
            h = pending[0] if len(pending) == 1 else jnp.concatenate(pending, axis=1)
            part = _dot(h, wd_ref[lo + FFN_CHUNK - width:lo + FFN_CHUNK, :])
            acc = part if acc is None else acc + part
            pending = []
    o_ref[...] = x_ref[...] + _rms(acc, gpost_ref[...])


def conv_ffn(x2d, seq, gpre, w_up, conv_w, conv_b, w_down, gpost, tm=512):
    t = x2d.shape[0]
    assert FFN_DIM % FFN_CHUNK == 0 and t % tm == 0 and seq % tm == 0 and tm % FFN_HALO == 0
    hb = tm // FFN_HALO

    def resident(a):
        return pl.BlockSpec(a.shape, lambda i: (0,) * a.ndim, pipeline_mode=pl.Buffered(1))

    consts = (gpre.reshape(1, D_MODEL), w_up, conv_w, conv_b.reshape(1, 2 * FFN_DIM), w_down,
              gpost.reshape(1, D_MODEL))
    return pl.pallas_call(
        functools.partial(_ffn_kernel, tm=tm, seq=seq),
        grid=(t // tm,),
        in_specs=[pl.BlockSpec((tm, D_MODEL), lambda i: (i, 0)),
                  pl.BlockSpec((FFN_HALO, D_MODEL), lambda i: (jnp.maximum(i * hb - 1, 0), 0))]
        + [resident(a) for a in consts],
        out_specs=pl.BlockSpec((tm, D_MODEL), lambda i: (i, 0)),
        out_shape=jax.ShapeDtypeStruct((t, D_MODEL), F32),
        scratch_shapes=[pltpu.VMEM((tm + FFN_HALO, D_MODEL), BF16)],
        compiler_params=_cparams(("parallel",)),
        name="conv_ffn",
    )(x2d, x2d, *consts)


def _nsa_prep_kernel(q_ref, kv_ref, pos_ref, freq_ref, ecos_ref, esin_ref, rest_ref,
                     qh_ref, kcmp_ref, vcmp_ref, kslc_ref, vslc_ref, kwin_ref, vwin_ref):
    ang = freq_ref[...] * pos_ref[0].astype(F32)

    def spread(t, e_ref):
        t1, t2, t3 = _split3(t)
        e = e_ref[...]
        return _dot_tn(t1, e) + _dot_tn(t2, e) + _dot_tn(t3, e)

    cs = spread(jnp.cos(ang), ecos_ref) + rest_ref[...]
    sn = spread(jnp.sin(ang), esin_ref)
    lane = lax.broadcasted_iota(jnp.int32, (1, LANE), 1) % NSA_DH
    first_half = lane < ROT_DIM // 2

    def rope(x):
        partner = jnp.where(first_half, pltpu.roll(x, LANE - ROT_DIM // 2, 1), pltpu.roll(x, ROT_DIM // 2, 1))
        return x * cs + partner * sn

    ts = q_ref.shape[1]
    aug_w = kslc_ref.shape[-1]
    low_half = lax.broadcasted_iota(jnp.int32, (1, LANE), 1) < NSA_DH

    def widen(x, upper, fill):
        low = jnp.where(low_half, pltpu.roll(x, NSA_DH, 1) if upper else x, fill[:, :LANE])
        return low if aug_w == LANE else jnp.concatenate([low, fill[:, LANE:]], axis=1)

    scale = NSA_DH ** -0.5 * math.log2(math.e)
    for j in range(NSA_HEADS // 2):
        t = (rope(q_ref[0, :, j * LANE:(j + 1) * LANE]) * scale).T
        qh_ref[0, 2 * j] = t[:NSA_DH].astype(BF16)
        qh_ref[0, 2 * j + 1] = t[NSA_DH:].astype(BF16)

    tok = pl.program_id(1) * ts + lax.broadcasted_iota(jnp.int32, (ts, 1), 0)
    block_onehot = jnp.where(lax.broadcasted_iota(jnp.int32, (1, aug_w), 1) - NSA_DH == tok // SLC_BLOCK, 1.0, 0.0)
    t = rope(kv_ref[0, :, 2 * LANE:3 * LANE])
    for g in range(NSA_GROUPS):
        kslc_ref[0, g] = widen(t, g == 1, block_onehot).astype(BF16)

    for i, o_ref in ((0, kcmp_ref), (4, kwin_ref), (1, vcmp_ref)):
        t = kv_ref[0, :, i * LANE:(i + 1) * LANE]
        if i % 2 == 0:
            t = rope(t)
        for g in range(NSA_GROUPS):
            o_ref[0, g] = t[:, g * NSA_DH:(g + 1) * NSA_DH].astype(o_ref.dtype)
    ones_row = jnp.where(lax.broadcasted_iota(jnp.int32, (NSA_VROWS - NSA_DH, ts), 0) == 0, 1.0, 0.0)
    for i, o_ref in ((3, vslc_ref), (5, vwin_ref)):
        t = kv_ref[0, :, i * LANE:(i + 1) * LANE].T
        for g in range(NSA_GROUPS):
            o_ref[0, g] = jnp.concatenate([t[g * NSA_DH:(g + 1) * NSA_DH, :], ones_row], axis=0).astype(o_ref.dtype)


NSA_VROWS = NSA_DH + 16


def _aug_width(seq):
    return -(-(NSA_DH + seq // SLC_BLOCK) // LANE) * LANE


def nsa_prep(q, kv, positions, ts=256):
    b, s, _ = q.shape
    half = ROT_DIM // 2
    freq_rows = 16
    inv_freq = ROPE_THETA ** (-jnp.arange(0, ROT_DIM, 2, dtype=F32) / ROT_DIM)
    freq = jnp.zeros((freq_rows, 1), F32).at[:half, 0].set(inv_freq)
    d = np.arange(LANE) % NSA_DH
    j = np.arange(freq_rows)[:, None]
    e_cos = ((d[None, :] < ROT_DIM) & (d[None, :] % half == j)).astype(np.float32)
    e_sin = e_cos * np.where(d[None, :] < half, -1.0, 1.0)
    rest = (d >= ROT_DIM).astype(np.float32).reshape(1, LANE)
    tables = (jnp.asarray(e_cos, dtype=BF16), jnp.asarray(e_sin, dtype=BF16), jnp.asarray(rest))
    tok_shape = (b, NSA_GROUPS, s, NSA_DH)
    tok_spec = pl.BlockSpec((1, NSA_GROUPS, ts, NSA_DH), lambda bi, i: (bi, 0, i, 0))
    feat_shape = (b, NSA_GROUPS, NSA_VROWS, s)
    feat_spec = pl.BlockSpec((1, NSA_GROUPS, NSA_VROWS, ts), lambda bi, i: (bi, 0, 0, i))
    aug_w = _aug_width(s)
    return pl.pallas_call(
        _nsa_prep_kernel,
        grid=(b, s // ts),
        in_specs=[pl.BlockSpec((1, ts, q.shape[2]), lambda bi, i: (bi, i, 0)),
                  pl.BlockSpec((1, ts, kv.shape[2]), lambda bi, i: (bi, i, 0)),
                  pl.BlockSpec((1, 1, ts), lambda bi, i: (bi, 0, i)),
                  pl.BlockSpec((freq_rows, 1), lambda bi, i: (0, 0))]
        + [pl.BlockSpec(t.shape, lambda bi, i: (0, 0)) for t in tables],
        out_specs=[pl.BlockSpec((1, NSA_HEADS, NSA_DH, ts), lambda bi, i: (bi, 0, 0, i)),
                   tok_spec, tok_spec,
                   pl.BlockSpec((1, NSA_GROUPS, ts, aug_w), lambda bi, i: (bi, 0, i, 0)),
                   feat_spec, tok_spec, feat_spec],
        out_shape=[jax.ShapeDtypeStruct((b, NSA_HEADS, NSA_DH, s), BF16),
                   jax.ShapeDtypeStruct(tok_shape, F32), jax.ShapeDtypeStruct(tok_shape, F32),
                   jax.ShapeDtypeStruct((b, NSA_GROUPS, s, aug_w), BF16), jax.ShapeDtypeStruct(feat_shape, BF16),
                   jax.ShapeDtypeStruct(tok_shape, BF16), jax.ShapeDtypeStruct(feat_shape, BF16)],
        compiler_params=_cparams(("parallel", "parallel")),
        name="nsa_prep",
    )(q, kv, positions.reshape(b, 1, s), freq, *tables)


def _compress_kernel(k_ref, v_ref, pek_ref, pev_ref, kw1_ref, vw1_ref, kb1_ref, vb1_ref, kw2_ref, vw2_ref,
                     kc_ref, vc_ref):
    half = CMP_STRIDE * NSA_DH

    def mlp(t_ref, pe_ref, w1_ref, b1_ref, w2_ref):
        a = t_ref[0, 0]
        n = a.shape[0]
        h_lo = _dot((a + pe_ref[:, :half]).astype(BF16), w1_ref[:half, :])
        h_hi = _dot((a + pe_ref[:, half:]).astype(BF16), w1_ref[half:, :])
        hid = _silu(h_lo + pltpu.roll(h_hi, n - 1, 0) + b1_ref[...])
        return _dot(hid.astype(BF16), w2_ref[...])

    kc_ref[0, 0] = mlp(k_ref, pek_ref, kw1_ref, kb1_ref, kw2_ref).astype(kc_ref.dtype)
    vc_ref[0, 0] = mlp(v_ref, pev_ref, vw1_ref, vb1_ref, vw2_ref).astype(vc_ref.dtype).T


def nsa_compress(kcmp, vcmp, pe_k, pe_v, ck_w1, ck_b1, ck_w2, cv_w1, cv_b1, cv_w2):
    b, g, s, dh = kcmp.shape
    n = s // CMP_STRIDE
    wide = CMP_STRIDE * dh
    assert CMP_BLOCK == 2 * CMP_STRIDE
    kr = kcmp.reshape(b, g, n, wide)
    vr = vcmp.reshape(b, g, n, wide)
    blk = pl.BlockSpec((1, 1, n, wide), lambda bi, gi: (bi, gi, 0, 0))

    def full(a):
        return pl.BlockSpec(a.shape, lambda bi, gi: (0,) * a.ndim)

    args = (pe_k.reshape(1, 2 * wide), pe_v.reshape(1, 2 * wide), ck_w1.astype(BF16), cv_w1.astype(BF16),
            ck_b1.reshape(1, dh), cv_b1.reshape(1, dh), ck_w2.astype(BF16), cv_w2.astype(BF16))
    return pl.pallas_call(
        _compress_kernel,
        grid=(b, g),
        in_specs=[blk, blk] + [full(a) for a in args],
        out_specs=[pl.BlockSpec((1, 1, n, dh), lambda bi, gi: (bi, gi, 0, 0)),
                   pl.BlockSpec((1, 1, dh, n), lambda bi, gi: (bi, gi, 0, 0))],
        out_shape=[jax.ShapeDtypeStruct((b, g, n, dh), BF16), jax.ShapeDtypeStruct((b, g, dh, n), BF16)],
        compiler_params=_cparams(("parallel", "parallel")),
        name="nsa_compress",
    )(kr, vr, *args)


ATT_TQ = 256
ATT_TK = 256
ATT_RING = 4


def _att_kernel(q_ref, kc_ref, vc_ref, ks_ref, vs_ref, kw_ref, vw_ref, gate_ref, ov_ref, o_ref,
                *s_refs, seq):
    tq, tk, rep, dh = ATT_TQ, ATT_TK, NSA_REP, NSA_DH
    cols = rep * tq
    nblk = seq // SLC_BLOCK
    ncmp = kc_ref.shape[2]
    aug_w = ks_ref.shape[-1]
    t0 = pl.program_id(2) * tq
    q = jnp.concatenate([q_ref[0, r] for r in range(rep)], axis=1)
    tpos = t0 + lax.broadcasted_iota(jnp.int32, (1, tq), 1)

    def per_head(x):
        return jnp.concatenate([x] * rep, axis=1)


    band = WINDOW + tq
    start = pl.multiple_of(jnp.maximum(t0 - WINDOW, 0), tq)
    dlt = tpos - (start + lax.broadcasted_iota(jnp.int32, (band, 1), 0))
    bias_w = jnp.where((dlt >= 0) & (dlt < WINDOW), 0.0, MASKED)
    s_w = _dot(kw_ref[0, 0, pl.ds(start, band), :], q) + per_head(bias_w)

    cmp_end = lax.broadcasted_iota(jnp.int32, (ncmp, 1), 0) * CMP_STRIDE + (CMP_BLOCK - 1)
    s_c = _dot(kc_ref[0, 0], q) + per_head(jnp.where(cmp_end <= tpos, 0.0, MASKED))
    p_c = jnp.exp2(s_c - jnp.max(s_c, axis=0, keepdims=True))
    l_c = jnp.sum(p_c, axis=0, keepdims=True)
    p_c = p_c * (per_head(jnp.where(tpos >= CMP_BLOCK - 1, 1.0, 0.0)) / l_c)
    o_c = _dot(vc_ref[0, 0], p_c.astype(BF16))

    p_sum = p_c[:, :tq]
    for r in range(1, rep):
        p_sum = p_sum + p_c[:, r * tq:(r + 1) * tq]
    p_hi, p_lo = _split2(p_sum)
    imp = _dot(ov_ref[...], p_hi) + _dot(ov_ref[...], p_lo)
    blk = lax.broadcasted_iota(jnp.int32, (nblk, 1), 0)
    cur = tpos // SLC_BLOCK
    forced = (blk == 0) | (blk == cur) | (blk == cur - 1)
    imp = jnp.where(forced, FORCED_SCORE, jnp.where(blk <= cur, imp, -1.0))
    sel = jnp.full((nblk, tq), MASKED, F32)
    for _ in range(min(SLC_TOPK, nblk)):
        best = jnp.max(imp, axis=0, keepdims=True)
        first = jnp.min(jnp.where(imp == best, blk, nblk), axis=0, keepdims=True)
        hit = blk == first
        sel = jnp.where(hit, 0.0, sel)
        imp = jnp.where(hit, -jnp.inf, imp)

    mask_rows = jnp.concatenate([sel.astype(BF16), jnp.zeros((aug_w - dh - nblk, tq), BF16)], axis=0)
    q_aug = jnp.concatenate([q, per_head(mask_rows)], axis=0)
    kidx = lax.broadcasted_iota(jnp.int32, (tk, 1), 0)

    def produce(kt, s_ref):
        k0 = pl.multiple_of(kt * tk, tk)
        s = _dot(ks_ref[0, 0, pl.ds(k0, tk), :], q_aug)
        s_ref[...] = s
        return jnp.max(s, axis=0, keepdims=True)

    def consume(kt, s_ref, tile_max, carry, causal):
        m, acc = carry
        k0 = pl.multiple_of(kt * tk, tk)
        v = vs_ref[0, 0, :, pl.ds(k0, tk)]
        s = s_ref[...]
        if causal:
            s = s + per_head(jnp.where(k0 + kidx <= tpos, 0.0, MASKED))
            tile_max = jnp.max(s, axis=0, keepdims=True)
        m_new = jnp.maximum(m, tile_max)
        alpha = jnp.exp2(m - m_new)
        p = jnp.exp2(s - m_new)
        return m_new, alpha * acc + _dot(v, p.astype(BF16))

    nbuf = len(s_refs)

    def slc_trip(j, carry):
        maxes, state = list(carry[:nbuf]), carry[nbuf:]
        for i, s_ref in enumerate(s_refs):
            state = consume(nbuf * j + i, s_ref, maxes[i], state, False)
            maxes[i] = produce(nbuf * (j + 1) + i, s_ref)
        return tuple(maxes) + state

    n_full = t0 // (nbuf * tk)
    init = tuple(produce(i, s_ref) for i, s_ref in enumerate(s_refs)) + (
        jnp.full((1, cols), MASKED, F32), jnp.zeros((NSA_VROWS, cols), F32))

    p_w = jnp.exp2(s_w - jnp.max(s_w, axis=0, keepdims=True))
    o_w = _dot(vw_ref[0, 0, :, pl.ds(start, band)], p_w.astype(BF16))
    o_w = o_w[:dh] / o_w[dh:dh + 1]

    carry = lax.fori_loop(0, n_full, slc_trip, init)
    maxes, state = carry[:nbuf], carry[nbuf:]
    own = (t0 - n_full * nbuf * tk) // tk
    for i, s_ref in enumerate(s_refs):
        kt = nbuf * n_full + i
        state = lax.cond(
            i < own,
            lambda st, kt=kt, s_ref=s_ref, mx=maxes[i]: consume(kt, s_ref, mx, st, False),
            lambda st, kt=kt, s_ref=s_ref: lax.cond(
                i == own, lambda st2: consume(kt, s_ref, None, st2, True), lambda st2: st2, st),
            state)
    acc_s = state[1]
    o_s = acc_s[:dh] / acc_s[dh:dh + 1]

    gates = jax.nn.sigmoid(gate_ref[0]).T
    merged = []
    for r in range(rep):
        sl = slice(r * tq, (r + 1) * tq)
        merged.append(gates[3 * r:3 * r + 1] * o_c[:, sl] + gates[3 * r + 1:3 * r + 2] * o_s[:, sl]
                      + gates[3 * r + 2:3 * r + 3] * o_w[:, sl])
    o_ref[0] = jnp.concatenate(merged, axis=0).T.astype(o_ref.dtype)


def _overlap_matrix(seq):
    ncp = seq // CMP_STRIDE
    n_cmp = (seq - CMP_BLOCK) // CMP_STRIDE + 1
    nblk = seq // SLC_BLOCK
    cs = np.arange(ncp) * CMP_STRIDE
    ss = np.arange(nblk) * SLC_BLOCK
    ov = ((cs[None, :] <= ss[:, None] + SLC_BLOCK - 1) & (cs[None, :] + CMP_BLOCK - 1 >= ss[:, None])
          & (np.arange(ncp)[None, :] < n_cmp))
    return jnp.asarray(ov.astype(np.float32), dtype=BF16)


def nsa_attention(qh, kc, vc, kslc, vslc, kwin, vwin, gates):
    b, _, dh, s = qh.shape
    aug_w = kslc.shape[-1]
    g = NSA_GROUPS
    ncp = kc.shape[2]
    nblk = s // SLC_BLOCK
    assert s % (ATT_RING * ATT_TK) == 0 and s >= WINDOW + ATT_TQ and aug_w == _aug_width(s)
    ov = _overlap_matrix(s)
    tok_spec = pl.BlockSpec((1, 1, s, dh), lambda bi, gi, i: (bi, gi, 0, 0))
    feat_spec = pl.BlockSpec((1, 1, NSA_VROWS, s), lambda bi, gi, i: (bi, gi, 0, 0))
    return pl.pallas_call(
        functools.partial(_att_kernel, seq=s),
        grid=(b, g, s // ATT_TQ),
        in_specs=[pl.BlockSpec((1, NSA_REP, dh, ATT_TQ), lambda bi, gi, i: (bi, gi, 0, i)),
                  pl.BlockSpec((1, 1, ncp, dh), lambda bi, gi, i: (bi, gi, 0, 0)),
                  pl.BlockSpec((1, 1, dh, ncp), lambda bi, gi, i: (bi, gi, 0, 0)),
                  pl.BlockSpec((1, 1, s, aug_w), lambda bi, gi, i: (bi, gi, 0, 0)),
                  feat_spec, tok_spec, feat_spec,
                  pl.BlockSpec((1, ATT_TQ, LANE), lambda bi, gi, i: (bi, i, gi)),
                  pl.BlockSpec((nblk, ncp), lambda bi, gi, i: (0, 0))],
        out_specs=pl.BlockSpec((1, ATT_TQ, NSA_REP * dh), lambda bi, gi, i: (bi, i, gi)),
        out_shape=jax.ShapeDtypeStruct((b, s, NSA_HEADS * dh), BF16),
        scratch_shapes=[pltpu.VMEM((ATT_TK, NSA_REP * ATT_TQ), F32)] * ATT_RING,
        compiler_params=_cparams(("parallel", "parallel", "arbitrary")),
        name="nsa_attention",
    )(qh, kc, vc, kslc, vslc, kwin, vwin, gates, ov)


def _ssd_kernel(z_ref, xbc_ref, dt_ref, cw_ref, cb_ref, dtb_ref, alog_ref, drow_ref, ng_ref, tril_ref, triu_ref,
                o_ref, carry_ref, state_ref, y_ref):
    cl = SSD_CHUNK
    hpg = SSD_HEADS // SSD_GROUPS

    @pl.when(pl.program_id(1) == 0)
    def _():
        carry_ref[...] = jnp.zeros_like(carry_ref)
        state_ref[...] = jnp.zeros_like(state_ref)

    xbc = xbc_ref[0]
    ext = jnp.concatenate([carry_ref[...], xbc], axis=0)
    act = _silu(_causal_conv(ext, cw_ref[...], CONV_HALO) + cb_ref[...])
    carry_ref[...] = xbc[cl - CONV_HALO:, :]

    xs = act[:, :SSD_INNER]
    dt = _softplus(dt_ref[0] + dtb_ref[...])
    a = dt * (-jnp.exp(alog_ref[...]))
    acs_c = _cumsum_cols(tril_ref[...], a)
    acs_r = _cumsum_rows_t(a, triu_ref[...])
    causal = lax.broadcasted_iota(jnp.int32, (cl, cl), 0) >= lax.broadcasted_iota(jnp.int32, (cl, cl), 1)

    for g in range(SSD_GROUPS):
        bg = act[:, SSD_INNER + g * SSD_N:SSD_INNER + (g + 1) * SSD_N].astype(BF16)
        cg = act[:, SSD_INNER + (SSD_GROUPS + g) * SSD_N:SSD_INNER + (SSD_GROUPS + g + 1) * SSD_N].astype(BF16)
        cb = _dot_nt(cg, bg)
        state = state_ref[g]
        y_off = _dot(cg, state.astype(BF16))
        weighted = []
        decay = []
        for hl in range(hpg):
            h = g * hpg + hl
            col = acs_c[:, h:h + 1]
            row = acs_r[h:h + 1, :]
            lmat = jnp.exp(jnp.where(causal, col - row, MASKED))
            xh = xs[:, h * SSD_P:(h + 1) * SSD_P]
            xdt = xh * dt[:, h:h + 1]
            y_diag = _dot((cb * lmat).astype(BF16), xdt.astype(BF16))
            y_ref[:, h * SSD_P:(h + 1) * SSD_P] = (y_diag + y_off[:, hl * SSD_P:(hl + 1) * SSD_P] * jnp.exp(col)
                                                   + drow_ref[:, h * SSD_P:(h + 1) * SSD_P] * xh)
            a_last = acs_c[cl - 1:cl, h:h + 1]
            weighted.append((xdt * jnp.exp(a_last - col)).astype(BF16))
            decay.append(jnp.broadcast_to(jnp.exp(a_last), (1, SSD_P)))
        contrib = _dot_tn(bg, jnp.concatenate(weighted, axis=1))
        state_ref[g] = state * jnp.concatenate(decay, axis=1) + contrib

    y = y_ref[...] * _silu(z_ref[0])
    gw = SSD_INNER // SSD_GROUPS
    for g in range(SSD_GROUPS):
        o_ref[0, :, g * gw:(g + 1) * gw] = _rms(y[:, g * gw:(g + 1) * gw],
                                                 ng_ref[:, g * gw:(g + 1) * gw]).astype(o_ref.dtype)


def _tri(n, upper):
    m = np.triu(np.ones((n, n), np.float32)) if upper else np.tril(np.ones((n, n), np.float32))
    return jnp.asarray(m, dtype=BF16)


def _lane_row(v, offset=0):
    row = jnp.zeros((1, LANE), F32)
    return row.at[0, offset:offset + v.shape[0]].set(v)


def ssd_mixer(z, xbc, dts, conv_w, conv_b, dt_bias, a_log, d_skip, norm_g):
    b, s, _ = z.shape
    cl = SSD_CHUNK
    hpg = SSD_HEADS // SSD_GROUPS

    def full(a):
        return pl.BlockSpec(a.shape, lambda bi, c: (0,) * a.ndim)

    args = (conv_w, conv_b.reshape(1, -1), _lane_row(dt_bias), _lane_row(a_log),
            jnp.repeat(d_skip, SSD_P).reshape(1, SSD_INNER), norm_g.reshape(1, SSD_INNER),
            _tri(cl, False), _tri(cl, True))
    return pl.pallas_call(
        _ssd_kernel,
        grid=(b, s // cl),
        in_specs=[pl.BlockSpec((1, cl, SSD_INNER), lambda bi, c: (bi, c, 0)),
                  pl.BlockSpec((1, cl, SSD_CONV_DIM), lambda bi, c: (bi, c, 0)),
                  pl.BlockSpec((1, cl, LANE), lambda bi, c: (bi, c, 0))] + [full(a) for a in args],
        out_specs=pl.BlockSpec((1, cl, SSD_INNER), lambda bi, c: (bi, c, 0)),
        out_shape=jax.ShapeDtypeStruct((b, s, SSD_INNER), BF16),
        scratch_shapes=[pltpu.VMEM((CONV_HALO, SSD_CONV_DIM), F32),
                        pltpu.VMEM((SSD_GROUPS, SSD_N, hpg * SSD_P), F32),
                        pltpu.VMEM((cl, SSD_INNER), F32)],
        compiler_params=_cparams(("parallel", "arbitrary")),
        name="ssd_mixer",
    )(z, xbc, dts, *args)


GDN_STEP_CHUNKS = 4


def _mm_split(a, b):
    return _dot(a[0], b[0]) + _dot(a[0], b[1]) + _dot(a[1], b[0])


def _unit_lower_inverses(a_list, row, col):
    eye = jnp.where(row == col, 1.0, 0.0)
    same16 = (row // 16) == (col // 16)
    same32 = (row // 32) == (col // 32)
    n = [_split2(jnp.where(same16, -a, 0.0)) for a in a_list]
    p = [eye + jnp.where(same16, -a, 0.0) for a in a_list]
    for _ in range(3):
        n = [_split2(_mm_split(x, x)) for x in n]
        p = [x + _mm_split(_split2(x), y) for x, y in zip(p, n)]
    for level_mask in (jnp.where(same16, 0.0, jnp.where(same32, 1.0, 0.0)), jnp.where(same32, 0.0, 1.0)):
        off = [_split2(a * level_mask) for a in a_list]
        ps = [_split2(x) for x in p]
        t = [_split2(_mm_split(x, y)) for x, y in zip(ps, off)]
        p = [x - _mm_split(y, z) for x, y, z in zip(p, t, ps)]
    return p


def _gdn_kernel(qkv_ref, z_ref, sm_ref, cw_ref, dtb_ref, alog_ref, ng_ref, tril_ref, triu_ref,
                o_ref, carry_ref, state_ref):
    cl = GDN_CHUNK
    rows = qkv_ref.shape[1]
    heads = range(GDN_HEADS)
    chunks = range(rows // cl)
    chains = [(c, h) for c in chunks for h in heads]

    @pl.when(pl.program_id(1) == 0)
    def _():
        carry_ref[...] = jnp.zeros_like(carry_ref)
        state_ref[...] = jnp.zeros_like(state_ref)

    qkv = qkv_ref[0]
    ext = jnp.concatenate([carry_ref[...], qkv], axis=0)
    act = _silu(_causal_conv(ext, cw_ref[...], CONV_HALO))
    carry_ref[...] = qkv[rows - CONV_HALO:, :]

    sm = sm_ref[0]
    beta_all = jax.nn.sigmoid(sm)
    g_all = -jnp.exp(alog_ref[...]) * _softplus(sm + dtb_ref[...])
    row = lax.broadcasted_iota(jnp.int32, (cl, cl), 0)
    col = lax.broadcasted_iota(jnp.int32, (cl, cl), 1)
    incl = row >= col

    def l2n(x):
        return x * lax.rsqrt(jnp.sum(x * x, axis=-1, keepdims=True) + NORM_EPS)

    def chunk(x, c):
        return x[c * cl:(c + 1) * cl]

    qn = [l2n(act[:, h * GDN_DK:(h + 1) * GDN_DK]) * GDN_DK ** -0.5 for h in heads]
    kn = [l2n(act[:, GDN_QK_W + h * GDN_DK:GDN_QK_W + (h + 1) * GDN_DK]) for h in heads]
    v = [act[:, 2 * GDN_QK_W + h * GDN_DV:2 * GDN_QK_W + (h + 1) * GDN_DV] for h in heads]
    beta = [beta_all[:, h:h + 1] for h in heads]
    kb = [kn[h] * beta[h] for h in heads]
    knb = [x.astype(BF16) for x in kn]
    gcs_c = [_cumsum_cols(tril_ref[...], chunk(g_all, c)) for c in chunks]
    gcs_r = [_cumsum_rows_t(chunk(g_all, c), triu_ref[...]) for c in chunks]
    gc = {(c, h): gcs_c[c][:, GDN_HEADS + h:GDN_HEADS + h + 1] for c, h in chains}
    dec = {(c, h): jnp.exp(jnp.where(incl, gc[c, h] - gcs_r[c][GDN_HEADS + h:GDN_HEADS + h + 1, :], MASKED))
           for c, h in chains}
    eg = {k: jnp.exp(g) for k, g in gc.items()}
    a_kk = [jnp.where(row > col, _dot_nt(chunk(kb[h], c).astype(BF16), chunk(knb[h], c)) * dec[c, h], 0.0)
            for c, h in chains]
    a_qk = {(c, h): (_dot_nt(chunk(qn[h], c).astype(BF16), chunk(knb[h], c)) * dec[c, h]).astype(BF16)
            for c, h in chains}
    t_inv = [t.astype(BF16) for t in _unit_lower_inverses(a_kk, row, col)]
    uw = {(c, h): _dot(t, jnp.concatenate([chunk(v[h], c) * chunk(beta[h], c), chunk(kb[h], c) * eg[c, h]],
                                          axis=1).astype(BF16))
          for (c, h), t in zip(chains, t_inv)}

    state = [state_ref[h] for h in heads]
    for c in chunks:
        sb = [s.astype(BF16) for s in state]
        ws = [_dot(jnp.concatenate([uw[c, h][:, GDN_DV:], chunk(qn[h], c) * eg[c, h]], axis=0).astype(BF16), sb[h])
              for h in heads]
        vb = [(uw[c, h][:, :GDN_DV] - ws[h][:cl]).astype(BF16) for h in heads]
        o = [ws[h][cl:] + _dot(a_qk[c, h], vb[h]) for h in heads]
        g_last = [gc[c, h][cl - 1:cl, :] for h in heads]
        state = [state[h] * jnp.exp(g_last[h])
                 + _dot_tn((chunk(kn[h], c) * jnp.exp(g_last[h] - gc[c, h])).astype(BF16), vb[h]) for h in heads]
        for h in heads:
            out = _rms(o[h], ng_ref[...]) * _silu(z_ref[0, c * cl:(c + 1) * cl, h * GDN_DV:(h + 1) * GDN_DV])
            o_ref[0, c * cl:(c + 1) * cl, h * GDN_DV:(h + 1) * GDN_DV] = out.astype(o_ref.dtype)
    for h in heads:
        state_ref[h] = state[h]


def gdn_mixer(qkv, z, sm, conv_w, dt_bias, a_log, norm_g):
    b, s, _ = qkv.shape
    cl = GDN_CHUNK

    def full(a):
        return pl.BlockSpec(a.shape, lambda bi, c: (0,) * a.ndim)

    args = (conv_w, _lane_row(dt_bias, GDN_HEADS), _lane_row(a_log, GDN_HEADS), norm_g.reshape(1, GDN_DV),
            _tri(cl, False), _tri(cl, True))
    rows = GDN_STEP_CHUNKS * cl
    assert s % rows == 0
    return pl.pallas_call(
        _gdn_kernel,
        grid=(b, s // rows),
        in_specs=[pl.BlockSpec((1, rows, GDN_CONV_DIM), lambda bi, c: (bi, c, 0)),
                  pl.BlockSpec((1, rows, GDN_V_W), lambda bi, c: (bi, c, 0)),
                  pl.BlockSpec((1, rows, LANE), lambda bi, c: (bi, c, 0))] + [full(a) for a in args],
        out_specs=pl.BlockSpec((1, rows, GDN_V_W), lambda bi, c: (bi, c, 0)),
        out_shape=jax.ShapeDtypeStruct((b, s, GDN_V_W), BF16),
        scratch_shapes=[pltpu.VMEM((CONV_HALO, GDN_CONV_DIM), F32),
                        pltpu.VMEM((GDN_HEADS, GDN_DK, GDN_DV), F32)],
        compiler_params=_cparams(("parallel", "arbitrary")),
        name="gdn_mixer",
    )(qkv, z, sm, *args)


def _pad_cols(w, width=LANE):
    return jnp.pad(w, ((0, 0), (0, width - w.shape[1])))


HY_Q_W = NSA_HEADS * NSA_DH
HY_KV_W = 6 * NSA_GROUPS * NSA_DH
HY_GATE_W = NSA_HEADS * 3


def hybrid_layer(x2d, b, s, positions, g_pre, g_post, w_in, w_out, pe_k, pe_v, ck_w1, ck_b1, ck_w2,
                 cv_w1, cv_b1, cv_w2, conv_w, conv_b, dt_bias, a_log, d_skip, norm_g):
    o = 0
    cols = {}
    for name, width in (("q", HY_Q_W), ("kv", HY_KV_W), ("gate", HY_GATE_W), ("z", SSD_INNER),
                        ("xbc", SSD_CONV_DIM), ("dt", SSD_HEADS)):
        cols[name] = w_in[:, o:o + width]
        o += width
    gpg = HY_GATE_W // NSA_GROUPS
    w_cat = jnp.concatenate(
        [cols["q"], cols["kv"], cols["z"], cols["xbc"]]
        + [_pad_cols(cols["gate"][:, g * gpg:(g + 1) * gpg]) for g in range(NSA_GROUPS)]
        + [_pad_cols(cols["dt"])], axis=1).astype(BF16)
    splits = (HY_Q_W, HY_KV_W, SSD_INNER, SSD_CONV_DIM, NSA_GROUPS * LANE, LANE)
    q, kv, z, xbc, gates, dts = norm_matmul(x2d, g_pre, w_cat, splits)

    qh, kcmp, vcmp, kslc, vslc, kwin, vwin = nsa_prep(q.reshape(b, s, -1), kv.reshape(b, s, -1), positions)
    kc, vc = nsa_compress(kcmp, vcmp, pe_k, pe_v, ck_w1, ck_b1, ck_w2, cv_w1, cv_b1, cv_w2)
    o_nsa = nsa_attention(qh, kc, vc, kslc, vslc, kwin, vwin, gates.reshape(b, s, -1))
    y = ssd_mixer(z.reshape(b, s, -1), xbc.reshape(b, s, -1), dts.reshape(b, s, -1),
                  conv_w, conv_b, dt_bias, a_log, d_skip, norm_g)
    return outproj_residual([o_nsa.reshape(b * s, -1), y.reshape(b * s, -1)], w_out.astype(BF16), x2d, g_post)


def gdn_layer(x2d, b, s, g_pre, g_post, w_in, conv_w, dt_bias, a_log, norm_g, w_out):
    w_cat = jnp.concatenate([w_in[:, :GDN_CONV_DIM + GDN_V_W], _pad_cols(w_in[:, GDN_CONV_DIM + GDN_V_W:])],
                            axis=1).astype(BF16)
    qkv, z, sm = norm_matmul(x2d, g_pre, w_cat, (GDN_CONV_DIM, GDN_V_W, LANE))
    o = gdn_mixer(qkv.reshape(b, s, -1), z.reshape(b, s, -1), sm.reshape(b, s, -1), conv_w, dt_bias, a_log, norm_g)
    return outproj_residual([o.reshape(b * s, -1)], w_out.astype(BF16), x2d, g_post)


def kernel(x, positions, norm_mix_pre, norm_mix_post, norm_ffn_pre, norm_ffn_post, hy_w_in, hy_w_out, nsa_pe_k, nsa_pe_v, nsa_ck_w1, nsa_ck_b1, nsa_ck_w2, nsa_cv_w1, nsa_cv_b1, nsa_cv_w2, ssd_conv_w, ssd_conv_b, ssd_dt_bias, ssd_a_log, ssd_d, ssd_norm, gdn_w_in, gdn_conv_w, gdn_dt_bias, gdn_a_log, gdn_norm, gdn_w_out, ffn_w_up, ffn_conv_w, ffn_conv_b, ffn_w_down):
    b, s, d = x.shape
    x2d = x.reshape(b * s, d)
    depth = norm_mix_pre.shape[0]
    for layer in range(depth):
        e = layer // 2
        if layer % 2 == 0:
            x2d = hybrid_layer(x2d, b, s, positions, norm_mix_pre[layer], norm_mix_post[layer], hy_w_in[e],
                               hy_w_out[e], nsa_pe_k[e], nsa_pe_v[e], nsa_ck_w1[e], nsa_ck_b1[e], nsa_ck_w2[e],
                               nsa_cv_w1[e], nsa_cv_b1[e], nsa_cv_w2[e], ssd_conv_w[e], ssd_conv_b[e],
                               ssd_dt_bias[e], ssd_a_log[e], ssd_d[e], ssd_norm[e])
        else:
            x2d = gdn_layer(x2d, b, s, norm_mix_pre[layer], norm_mix_post[layer], gdn_w_in[e], gdn_conv_w[e],
                            gdn_dt_bias[e], gdn_a_log[e], gdn_norm[e], gdn_w_out[e])
        x2d = conv_ffn(x2d, s, norm_ffn_pre[layer], ffn_w_up[layer].astype(BF16), ffn_conv_w[layer],
                       ffn_conv_b[layer], ffn_w_down[layer].astype(BF16), norm_ffn_post[layer])
    return x2d.reshape(b, s, d)
```

```python
import functools
import math

import numpy as np
import jax
import jax.numpy as jnp
from jax import lax
from jax.experimental import pallas as pl
from jax.experimental.pallas import tpu as pltpu

F32 = jnp.float32
BF16 = jnp.bfloat16

D_MODEL = 1024
NORM_EPS = 1e-6
MASKED = -1e30

NSA_HEADS = 8
NSA_GROUPS = 2
NSA_REP = NSA_HEADS // NSA_GROUPS
NSA_DH = 64
CMP_BLOCK = 32
CMP_STRIDE = 16
SLC_BLOCK = 64
SLC_TOPK = 16
WINDOW = 512
ROPE_THETA = 500000.0
ROT_DIM = NSA_DH // 4
FORCED_SCORE = 1e9

SSD_HEADS = 8
SSD_P = 64
SSD_INNER = SSD_HEADS * SSD_P
SSD_GROUPS = 2
SSD_N = 128
SSD_CHUNK = 256
SSD_CONV_DIM = SSD_INNER + 2 * SSD_GROUPS * SSD_N

GDN_HEADS = 8
GDN_DK = 128
GDN_DV = 128
GDN_CHUNK = 64
GDN_QK_W = GDN_HEADS * GDN_DK
GDN_V_W = GDN_HEADS * GDN_DV
GDN_CONV_DIM = 2 * GDN_QK_W + GDN_V_W

FFN_DIM = 2816

LANE = 128
CONV_HALO = 8
FFN_HALO = 16
VMEM_LIMIT = 56 * 1024 * 1024


def _cparams(sem):
    return pltpu.CompilerParams(dimension_semantics=sem, vmem_limit_bytes=VMEM_LIMIT)


def _rms(x, g):
    return x * lax.rsqrt(jnp.mean(x * x, axis=-1, keepdims=True) + NORM_EPS) * g


def _silu(x):
    return x * jax.nn.sigmoid(x)


def _softplus(x):
    return jnp.maximum(x, 0.0) + jnp.log1p(jnp.exp(-jnp.abs(x)))


def _dot(a, b):
    return jnp.dot(a, b, preferred_element_type=F32)


def _dot_nt(a, b):
    return lax.dot_general(a, b, (((1,), (1,)), ((), ())), preferred_element_type=F32)


def _dot_tn(a, b):
    return lax.dot_general(a, b, (((0,), (0,)), ((), ())), preferred_element_type=F32)


def _split3(x):
    x1 = x.astype(BF16)
    r1 = x - x1.astype(F32)
    x2 = r1.astype(BF16)
    x3 = (r1 - x2.astype(F32)).astype(BF16)
    return x1, x2, x3


def _split2(x):
    x1 = x.astype(BF16)
    x2 = (x - x1.astype(F32)).astype(BF16)
    return x1, x2


def _cumsum_cols(tril, a):
    a1, a2, a3 = _split3(a)
    return _dot(tril, a1) + _dot(tril, a2) + _dot(tril, a3)


def _cumsum_rows_t(a, triu):
    a1, a2, a3 = _split3(a)
    return _dot_tn(a1, triu) + _dot_tn(a2, triu) + _dot_tn(a3, triu)


def _dot_f32(a, b):
    a1, a2 = _split2(a)
    b1, b2 = _split2(b)
    return _dot(a1, b1) + _dot(a1, b2) + _dot(a2, b1)


def _causal_conv(ext, cw, halo):
    width = cw.shape[0]
    y = ext[halo:] * cw[width - 1:width]
    for k in range(1, width):
        y = y + pltpu.roll(ext, k, 0)[halo:] * cw[width - 1 - k:width - k]
    return y


def _inproj_kernel(x_ref, g_ref, w_ref, *out_refs, splits):
    hn = _rms(x_ref[...], g_ref[...]).astype(BF16)
    y = _dot(hn, w_ref[...])
    off = 0
    for o_ref, n in zip(out_refs, splits):
        o_ref[...] = y[:, off:off + n].astype(o_ref.dtype)
        off += n


def norm_matmul(x2d, g, w, splits, tm=512):
    t = x2d.shape[0]
    n = w.shape[1]
    assert sum(splits) == n and t % tm == 0
    return pl.pallas_call(
        functools.partial(_inproj_kernel, splits=splits),
        grid=(t // tm,),
        in_specs=[pl.BlockSpec((tm, D_MODEL), lambda i: (i, 0)),
                  pl.BlockSpec((1, D_MODEL), lambda i: (0, 0)),
                  pl.BlockSpec((D_MODEL, n), lambda i: (0, 0))],
        out_specs=[pl.BlockSpec((tm, s), lambda i: (i, 0)) for s in splits],
        out_shape=[jax.ShapeDtypeStruct((t, s), F32) for s in splits],
        compiler_params=_cparams(("parallel",)),
        name="norm_matmul",
    )(x2d, g.reshape(1, D_MODEL), w)


FFN_CHUNK = 256
FFN_DOWN_CHUNKS = 2


def _ffn_kernel(*refs, n_parts, tm, seq):
    x_ref, halo_ref = refs[:2]
    part_refs = refs[2:2 + n_parts]
    part_halo_refs = refs[2 + n_parts:2 + 2 * n_parts]
    (wo_ref, gmix_ref, gpre_ref, wup_ref, cw_ref, cb_ref, wd_ref, gpost_ref, o_ref, hn_ref) = refs[2 + 2 * n_parts:]

    def after_mixer(x, parts):
        acc = None
        off = 0
        for p_ref in parts:
            k = p_ref.shape[-1]
            part = _dot(p_ref[...], wo_ref[off:off + k, :])
            acc = part if acc is None else acc + part
            off += k
        return x + _rms(acc, gmix_ref[...])

    o_ref[...] = after_mixer(x_ref[...], part_refs)
    keep = ((pl.program_id(0) * tm) % seq != 0).astype(F32)
    hn_ref[:FFN_HALO, :] = (_rms(after_mixer(halo_ref[...], part_halo_refs), gpre_ref[...]) * keep).astype(BF16)
    hn_ref[FFN_HALO:, :] = _rms(o_ref[...], gpre_ref[...]).astype(BF16)
    hn = hn_ref[...]

    def up(lo):
        return _dot(hn, wup_ref[:, lo:lo + FFN_CHUNK]), _dot(hn, wup_ref[:, FFN_DIM + lo:FFN_DIM + lo + FFN_CHUNK])

    def conv(u, lo):
        return _causal_conv(u, cw_ref[:, lo:lo + FFN_CHUNK], FFN_HALO) + cb_ref[:, lo:lo + FFN_CHUNK]

    acc = None
    pending = []
    n_chunks = FFN_DIM // FFN_CHUNK
    ahead = up(0)
    for c in range(n_chunks):
        lo = c * FFN_CHUNK
        u_gate, u_val = ahead
        if c + 1 < n_chunks:
            ahead = up(lo + FFN_CHUNK)
        pending.append((_silu(conv(u_gate, lo)) * conv(u_val, FFN_DIM + lo)).astype(BF16))
        if len(pending) == FFN_DOWN_CHUNKS or c == n_chunks - 1:
            width = len(pending) * FFN_CHUNK
            h = pending[0] if len(pending) == 1 else jnp.concatenate(pending, axis=1)
            part = _dot(h, wd_ref[lo + FFN_CHUNK - width:lo + FFN_CHUNK, :])
            acc = part if acc is None else acc + part
            pending = []
    o_ref[...] = o_ref[...] + _rms(acc, gpost_ref[...])


def mixer_out_ffn(x2d, parts, w_out, gmix, seq, gpre, w_up, conv_w, conv_b, w_down, gpost, tm=512):
    t = x2d.shape[0]
    assert FFN_DIM % FFN_CHUNK == 0 and t % tm == 0 and seq % tm == 0 and tm % FFN_HALO == 0
    hb = tm // FFN_HALO

    def resident(a):
        return pl.BlockSpec(a.shape, lambda i: (0,) * a.ndim, pipeline_mode=pl.Buffered(1))

    def halo(width):
        return pl.BlockSpec((FFN_HALO, width), lambda i: (jnp.maximum(i * hb - 1, 0), 0))

    consts = (w_out, gmix.reshape(1, D_MODEL), gpre.reshape(1, D_MODEL), w_up, conv_w,
              conv_b.reshape(1, 2 * FFN_DIM), w_down, gpost.reshape(1, D_MODEL))
    return pl.pallas_call(
        functools.partial(_ffn_kernel, n_parts=len(parts), tm=tm, seq=seq),
        grid=(t // tm,),
        in_specs=[pl.BlockSpec((tm, D_MODEL), lambda i: (i, 0)), halo(D_MODEL)]
        + [pl.BlockSpec((tm, p.shape[1]), lambda i: (i, 0)) for p in parts]
        + [halo(p.shape[1]) for p in parts]
        + [resident(a) for a in consts],
        out_specs=pl.BlockSpec((tm, D_MODEL), lambda i: (i, 0)),
        out_shape=jax.ShapeDtypeStruct((t, D_MODEL), F32),
        scratch_shapes=[pltpu.VMEM((tm + FFN_HALO, D_MODEL), BF16)],
        compiler_params=_cparams(("parallel",)),
        name="mixer_out_ffn",
    )(x2d, x2d, *parts, *parts, *consts)


def _nsa_prep_kernel(q_ref, kv_ref, pos_ref, freq_ref, ecos_ref, esin_ref, rest_ref,
                     qh_ref, kcmp_ref, vcmp_ref, kslc_ref, vslc_ref, kwin_ref, vwin_ref):
    ang = freq_ref[...] * pos_ref[0].astype(F32)

    def spread(t, e_ref):
        t1, t2, t3 = _split3(t)
        e = e_ref[...]
        return _dot_tn(t1, e) + _dot_tn(t2, e) + _dot_tn(t3, e)

    cs = spread(jnp.cos(ang), ecos_ref) + rest_ref[...]
    sn = spread(jnp.sin(ang), esin_ref)
    lane = lax.broadcasted_iota(jnp.int32, (1, LANE), 1) % NSA_DH
    first_half = lane < ROT_DIM // 2

    def rope(x):
        partner = jnp.where(first_half, pltpu.roll(x, LANE - ROT_DIM // 2, 1), pltpu.roll(x, ROT_DIM // 2, 1))
        return x * cs + partner * sn

    ts = q_ref.shape[1]
    aug_w = kslc_ref.shape[-1]
    low_half = lax.broadcasted_iota(jnp.int32, (1, LANE), 1) < NSA_DH

    def widen(x, upper, fill):
        low = jnp.where(low_half, pltpu.roll(x, NSA_DH, 1) if upper else x, fill[:, :LANE])
        return low if aug_w == LANE else jnp.concatenate([low, fill[:, LANE:]], axis=1)

    scale = NSA_DH ** -0.5 * math.log2(math.e)
    for j in range(NSA_HEADS // 2):
        t = (rope(q_ref[0, :, j * LANE:(j + 1) * LANE]) * scale).T
        qh_ref[0, 2 * j] = t[:NSA_DH].astype(BF16)
        qh_ref[0, 2 * j + 1] = t[NSA_DH:].astype(BF16)

    tok = pl.program_id(1) * ts + lax.broadcasted_iota(jnp.int32, (ts, 1), 0)
    block_onehot = jnp.where(lax.broadcasted_iota(jnp.int32, (1, aug_w), 1) - NSA_DH == tok // SLC_BLOCK, 1.0, 0.0)
    t = rope(kv_ref[0, :, 2 * LANE:3 * LANE])
    for g in range(NSA_GROUPS):
        kslc_ref[0, g] = widen(t, g == 1, block_onehot).astype(BF16)

    for i, o_ref in ((0, kcmp_ref), (4, kwin_ref), (1, vcmp_ref)):
        t = kv_ref[0, :, i * LANE:(i + 1) * LANE]
        if i % 2 == 0:
            t = rope(t)
        for g in range(NSA_GROUPS):
            o_ref[0, g] = t[:, g * NSA_DH:(g + 1) * NSA_DH].astype(o_ref.dtype)
    ones_row = jnp.where(lax.broadcasted_iota(jnp.int32, (NSA_VROWS - NSA_DH, ts), 0) == 0, 1.0, 0.0)
    for i, o_ref in ((3, vslc_ref), (5, vwin_ref)):
        t = kv_ref[0, :, i * LANE:(i + 1) * LANE].T
        for g in range(NSA_GROUPS):
            o_ref[0, g] = jnp.concatenate([t[g * NSA_DH:(g + 1) * NSA_DH, :], ones_row], axis=0).astype(o_ref.dtype)


NSA_VROWS = NSA_DH + 16


def _aug_width(seq):
    return -(-(NSA_DH + seq // SLC_BLOCK) // LANE) * LANE


def nsa_prep(q, kv, positions, ts=256):
    b, s, _ = q.shape
    half = ROT_DIM // 2
    freq_rows = 16
    inv_freq = ROPE_THETA ** (-jnp.arange(0, ROT_DIM, 2, dtype=F32) / ROT_DIM)
    freq = jnp.zeros((freq_rows, 1), F32).at[:half, 0].set(inv_freq)
    d = np.arange(LANE) % NSA_DH
    j = np.arange(freq_rows)[:, None]
    e_cos = ((d[None, :] < ROT_DIM) & (d[None, :] % half == j)).astype(np.float32)
    e_sin = e_cos * np.where(d[None, :] < half, -1.0, 1.0)
    rest = (d >= ROT_DIM).astype(np.float32).reshape(1, LANE)
    tables = (jnp.asarray(e_cos, dtype=BF16), jnp.asarray(e_sin, dtype=BF16), jnp.asarray(rest))
    tok_shape = (b, NSA_GROUPS, s, NSA_DH)
    tok_spec = pl.BlockSpec((1, NSA_GROUPS, ts, NSA_DH), lambda bi, i: (bi, 0, i, 0))
    feat_shape = (b, NSA_GROUPS, NSA_VROWS, s)
    feat_spec = pl.BlockSpec((1, NSA_GROUPS, NSA_VROWS, ts), lambda bi, i: (bi, 0, 0, i))
    aug_w = _aug_width(s)
    return pl.pallas_call(
        _nsa_prep_kernel,
        grid=(b, s // ts),
        in_specs=[pl.BlockSpec((1, ts, q.shape[2]), lambda bi, i: (bi, i, 0)),
                  pl.BlockSpec((1, ts, kv.shape[2]), lambda bi, i: (bi, i, 0)),
                  pl.BlockSpec((1, 1, ts), lambda bi, i: (bi, 0, i)),
                  pl.BlockSpec((freq_rows, 1), lambda bi, i: (0, 0))]
        + [pl.BlockSpec(t.shape, lambda bi, i: (0, 0)) for t in tables],
        out_specs=[pl.BlockSpec((1, NSA_HEADS, NSA_DH, ts), lambda bi, i: (bi, 0, 0, i)),
                   tok_spec, tok_spec,
                   pl.BlockSpec((1, NSA_GROUPS, ts, aug_w), lambda bi, i: (bi, 0, i, 0)),
                   feat_spec, tok_spec, feat_spec],
        out_shape=[jax.ShapeDtypeStruct((b, NSA_HEADS, NSA_DH, s), BF16),
                   jax.ShapeDtypeStruct(tok_shape, F32), jax.ShapeDtypeStruct(tok_shape, F32),
                   jax.ShapeDtypeStruct((b, NSA_GROUPS, s, aug_w), BF16), jax.ShapeDtypeStruct(feat_shape, BF16),
                   jax.ShapeDtypeStruct(tok_shape, BF16), jax.ShapeDtypeStruct(feat_shape, BF16)],
        compiler_params=_cparams(("parallel", "parallel")),
        name="nsa_prep",
    )(q, kv, positions.reshape(b, 1, s), freq, *tables)


def _compress_kernel(k_ref, v_ref, pek_ref, pev_ref, kw1_ref, vw1_ref, kb1_ref, vb1_ref, kw2_ref, vw2_ref,
                     kc_ref, vc_ref):
    half = CMP_STRIDE * NSA_DH

    def mlp(t_ref, pe_ref, w1_ref, b1_ref, w2_ref):
        a = t_ref[0, 0]
        n = a.shape[0]
        h_lo = _dot((a + pe_ref[:, :half]).astype(BF16), w1_ref[:half, :])
        h_hi = _dot((a + pe_ref[:, half:]).astype(BF16), w1_ref[half:, :])
        hid = _silu(h_lo + pltpu.roll(h_hi, n - 1, 0) + b1_ref[...])
        return _dot(hid.astype(BF16), w2_ref[...])

    kc_ref[0, 0] = mlp(k_ref, pek_ref, kw1_ref, kb1_ref, kw2_ref).astype(kc_ref.dtype)
    vc_ref[0, 0] = mlp(v_ref, pev_ref, vw1_ref, vb1_ref, vw2_ref).astype(vc_ref.dtype).T


def nsa_compress(kcmp, vcmp, pe_k, pe_v, ck_w1, ck_b1, ck_w2, cv_w1, cv_b1, cv_w2):
    b, g, s, dh = kcmp.shape
    n = s // CMP_STRIDE
    wide = CMP_STRIDE * dh
    assert CMP_BLOCK == 2 * CMP_STRIDE
    kr = kcmp.reshape(b, g, n, wide)
    vr = vcmp.reshape(b, g, n, wide)
    blk = pl.BlockSpec((1, 1, n, wide), lambda bi, gi: (bi, gi, 0, 0))

    def full(a):
        return pl.BlockSpec(a.shape, lambda bi, gi: (0,) * a.ndim)

    args = (pe_k.reshape(1, 2 * wide), pe_v.reshape(1, 2 * wide), ck_w1.astype(BF16), cv_w1.astype(BF16),
            ck_b1.reshape(1, dh), cv_b1.reshape(1, dh), ck_w2.astype(BF16), cv_w2.astype(BF16))
    return pl.pallas_call(
        _compress_kernel,
        grid=(b, g),
        in_specs=[blk, blk] + [full(a) for a in args],
        out_specs=[pl.BlockSpec((1, 1, n, dh), lambda bi, gi: (bi, gi, 0, 0)),
                   pl.BlockSpec((1, 1, dh, n), lambda bi, gi: (bi, gi, 0, 0))],
        out_shape=[jax.ShapeDtypeStruct((b, g, n, dh), BF16), jax.ShapeDtypeStruct((b, g, dh, n), BF16)],
        compiler_params=_cparams(("parallel", "parallel")),
        name="nsa_compress",
    )(kr, vr, *args)


ATT_TQ = 256
ATT_TK = 256
ATT_RING = 4


def _att_kernel(q_ref, kc_ref, vc_ref, ks_ref, vs_ref, kw_ref, vw_ref, gate_ref, ov_ref, o_ref,
                *s_refs, seq):
    tq, tk, rep, dh = ATT_TQ, ATT_TK, NSA_REP, NSA_DH
    cols = rep * tq
    nblk = seq // SLC_BLOCK
    ncmp = kc_ref.shape[2]
    aug_w = ks_ref.shape[-1]
    t0 = pl.program_id(2) * tq
    q = jnp.concatenate([q_ref[0, r] for r in range(rep)], axis=1)
    tpos = t0 + lax.broadcasted_iota(jnp.int32, (1, tq), 1)

    def per_head(x):
        return jnp.concatenate([x] * rep, axis=1)


    band = WINDOW + tq
    start = pl.multiple_of(jnp.maximum(t0 - WINDOW, 0), tq)
    dlt = tpos - (start + lax.broadcasted_iota(jnp.int32, (band, 1), 0))
    bias_w = jnp.where((dlt >= 0) & (dlt < WINDOW), 0.0, MASKED)
    s_w = _dot(kw_ref[0, 0, pl.ds(start, band), :], q) + per_head(bias_w)

    cmp_end = lax.broadcasted_iota(jnp.int32, (ncmp, 1), 0) * CMP_STRIDE + (CMP_BLOCK - 1)
    s_c = _dot(kc_ref[0, 0], q) + per_head(jnp.where(cmp_end <= tpos, 0.0, MASKED))
    p_c = jnp.exp2(s_c - jnp.max(s_c, axis=0, keepdims=True))
    l_c = jnp.sum(p_c, axis=0, keepdims=True)
    p_c = p_c * (per_head(jnp.where(tpos >= CMP_BLOCK - 1, 1.0, 0.0)) / l_c)
    o_c = _dot(vc_ref[0, 0], p_c.astype(BF16))

    p_sum = p_c[:, :tq]
    for r in range(1, rep):
        p_sum = p_sum + p_c[:, r * tq:(r + 1) * tq]
    p_hi, p_lo = _split2(p_sum)
    imp = _dot(ov_ref[...], p_hi) + _dot(ov_ref[...], p_lo)
    blk = lax.broadcasted_iota(jnp.int32, (nblk, 1), 0)
    cur = tpos // SLC_BLOCK
    forced = (blk == 0) | (blk == cur) | (blk == cur - 1)
    imp = jnp.where(forced, FORCED_SCORE, jnp.where(blk <= cur, imp, -1.0))
    sel = jnp.full((nblk, tq), MASKED, F32)
    for _ in range(min(SLC_TOPK, nblk)):
        best = jnp.max(imp, axis=0, keepdims=True)
        first = jnp.min(jnp.where(imp == best, blk, nblk), axis=0, keepdims=True)
        hit = blk == first
        sel = jnp.where(hit, 0.0, sel)
        imp = jnp.where(hit, -jnp.inf, imp)

    mask_rows = jnp.concatenate([sel.astype(BF16), jnp.zeros((aug_w - dh - nblk, tq), BF16)], axis=0)
    q_aug = jnp.concatenate([q, per_head(mask_rows)], axis=0)
    kidx = lax.broadcasted_iota(jnp.int32, (tk, 1), 0)

    def produce(kt, s_ref):
        k0 = pl.multiple_of(kt * tk, tk)
        s = _dot(ks_ref[0, 0, pl.ds(k0, tk), :], q_aug)
        s_ref[...] = s
        return jnp.max(s, axis=0, keepdims=True)

    def consume(kt, s_ref, tile_max, carry, causal):
        m, acc = carry
        k0 = pl.multiple_of(kt * tk, tk)
        v = vs_ref[0, 0, :, pl.ds(k0, tk)]
        s = s_ref[...]
        if causal:
            s = s + per_head(jnp.where(k0 + kidx <= tpos, 0.0, MASKED))
            tile_max = jnp.max(s, axis=0, keepdims=True)
        m_new = jnp.maximum(m, tile_max)
        alpha = jnp.exp2(m - m_new)
        p = jnp.exp2(s - m_new)
        return m_new, alpha * acc + _dot(v, p.astype(BF16))

    nbuf = len(s_refs)

    def slc_trip(j, carry):
        maxes, state = list(carry[:nbuf]), carry[nbuf:]
        for i, s_ref in enumerate(s_refs):
            state = consume(nbuf * j + i, s_ref, maxes[i], state, False)
            maxes[i] = produce(nbuf * (j + 1) + i, s_ref)
        return tuple(maxes) + state

    n_full = t0 // (nbuf * tk)
    init = tuple(produce(i, s_ref) for i, s_ref in enumerate(s_refs)) + (
        jnp.full((1, cols), MASKED, F32), jnp.zeros((NSA_VROWS, cols), F32))

    p_w = jnp.exp2(s_w - jnp.max(s_w, axis=0, keepdims=True))
    o_w = _dot(vw_ref[0, 0, :, pl.ds(start, band)], p_w.astype(BF16))
    o_w = o_w[:dh] / o_w[dh:dh + 1]

    carry = lax.fori_loop(0, n_full, slc_trip, init)
    maxes, state = carry[:nbuf], carry[nbuf:]
    own = (t0 - n_full * nbuf * tk) // tk
    for i, s_ref in enumerate(s_refs):
        kt = nbuf * n_full + i
        state = lax.cond(
            i < own,
            lambda st, kt=kt, s_ref=s_ref, mx=maxes[i]: consume(kt, s_ref, mx, st, False),
            lambda st, kt=kt, s_ref=s_ref: lax.cond(
                i == own, lambda st2: consume(kt, s_ref, None, st2, True), lambda st2: st2, st),
            state)
    acc_s = state[1]
    o_s = acc_s[:dh] / acc_s[dh:dh + 1]

    gates = jax.nn.sigmoid(gate_ref[0]).T
    merged = []
    for r in range(rep):
        sl = slice(r * tq, (r + 1) * tq)
        merged.append(gates[3 * r:3 * r + 1] * o_c[:, sl] + gates[3 * r + 1:3 * r + 2] * o_s[:, sl]
                      + gates[3 * r + 2:3 * r + 3] * o_w[:, sl])
    o_ref[0] = jnp.concatenate(merged, axis=0).T.astype(o_ref.dtype)


def _overlap_matrix(seq):
    ncp = seq // CMP_STRIDE
    n_cmp = (seq - CMP_BLOCK) // CMP_STRIDE + 1
    nblk = seq // SLC_BLOCK
    cs = np.arange(ncp) * CMP_STRIDE
    ss = np.arange(nblk) * SLC_BLOCK
    ov = ((cs[None, :] <= ss[:, None] + SLC_BLOCK - 1) & (cs[None, :] + CMP_BLOCK - 1 >= ss[:, None])
          & (np.arange(ncp)[None, :] < n_cmp))
    return jnp.asarray(ov.astype(np.float32), dtype=BF16)


def nsa_attention(qh, kc, vc, kslc, vslc, kwin, vwin, gates):
    b, _, dh, s = qh.shape
    aug_w = kslc.shape[-1]
    g = NSA_GROUPS
    ncp = kc.shape[2]
    nblk = s // SLC_BLOCK
    assert s % (ATT_RING * ATT_TK) == 0 and s >= WINDOW + ATT_TQ and aug_w == _aug_width(s)
    ov = _overlap_matrix(s)
    tok_spec = pl.BlockSpec((1, 1, s, dh), lambda bi, gi, i: (bi, gi, 0, 0))
    feat_spec = pl.BlockSpec((1, 1, NSA_VROWS, s), lambda bi, gi, i: (bi, gi, 0, 0))
    return pl.pallas_call(
        functools.partial(_att_kernel, seq=s),
        grid=(b, g, s // ATT_TQ),
        in_specs=[pl.BlockSpec((1, NSA_REP, dh, ATT_TQ), lambda bi, gi, i: (bi, gi, 0, i)),
                  pl.BlockSpec((1, 1, ncp, dh), lambda bi, gi, i: (bi, gi, 0, 0)),
                  pl.BlockSpec((1, 1, dh, ncp), lambda bi, gi, i: (bi, gi, 0, 0)),
                  pl.BlockSpec((1, 1, s, aug_w), lambda bi, gi, i: (bi, gi, 0, 0)),
                  feat_spec, tok_spec, feat_spec,
                  pl.BlockSpec((1, ATT_TQ, LANE), lambda bi, gi, i: (bi, i, gi)),
                  pl.BlockSpec((nblk, ncp), lambda bi, gi, i: (0, 0))],
        out_specs=pl.BlockSpec((1, ATT_TQ, NSA_REP * dh), lambda bi, gi, i: (bi, i, gi)),
        out_shape=jax.ShapeDtypeStruct((b, s, NSA_HEADS * dh), BF16),
        scratch_shapes=[pltpu.VMEM((ATT_TK, NSA_REP * ATT_TQ), F32)] * ATT_RING,
        compiler_params=_cparams(("parallel", "parallel", "arbitrary")),
        name="nsa_attention",
    )(qh, kc, vc, kslc, vslc, kwin, vwin, gates, ov)


def _ssd_kernel(z_ref, xbc_ref, dt_ref, cw_ref, cb_ref, dtb_ref, alog_ref, drow_ref, ng_ref, tril_ref, triu_ref,
                o_ref, carry_ref, state_ref, y_ref):
    cl = SSD_CHUNK
    hpg = SSD_HEADS // SSD_GROUPS

    @pl.when(pl.program_id(1) == 0)
    def _():
        carry_ref[...] = jnp.zeros_like(carry_ref)
        state_ref[...] = jnp.zeros_like(state_ref)

    xbc = xbc_ref[0]
    ext = jnp.concatenate([carry_ref[...], xbc], axis=0)
    act = _silu(_causal_conv(ext, cw_ref[...], CONV_HALO) + cb_ref[...])
    carry_ref[...] = xbc[cl - CONV_HALO:, :]

    xs = act[:, :SSD_INNER]
    dt = _softplus(dt_ref[0] + dtb_ref[...])
    a = dt * (-jnp.exp(alog_ref[...]))
    acs_c = _cumsum_cols(tril_ref[...], a)
    acs_r = _cumsum_rows_t(a, triu_ref[...])
    causal = lax.broadcasted_iota(jnp.int32, (cl, cl), 0) >= lax.broadcasted_iota(jnp.int32, (cl, cl), 1)

    for g in range(SSD_GROUPS):
        bg = act[:, SSD_INNER + g * SSD_N:SSD_INNER + (g + 1) * SSD_N].astype(BF16)
        cg = act[:, SSD_INNER + (SSD_GROUPS + g) * SSD_N:SSD_INNER + (SSD_GROUPS + g + 1) * SSD_N].astype(BF16)
        cb = _dot_nt(cg, bg)
        state = state_ref[g]
        y_off = _dot(cg, state.astype(BF16))
        weighted = []
        decay = []
        for hl in range(hpg):
            h = g * hpg + hl
            col = acs_c[:, h:h + 1]
            row = acs_r[h:h + 1, :]
            lmat = jnp.exp(jnp.where(causal, col - row, MASKED))
            xh = xs[:, h * SSD_P:(h + 1) * SSD_P]
            xdt = xh * dt[:, h:h + 1]
            y_diag = _dot((cb * lmat).astype(BF16), xdt.astype(BF16))
            y_ref[:, h * SSD_P:(h + 1) * SSD_P] = (y_diag + y_off[:, hl * SSD_P:(hl + 1) * SSD_P] * jnp.exp(col)
                                                   + drow_ref[:, h * SSD_P:(h + 1) * SSD_P] * xh)
            a_last = acs_c[cl - 1:cl, h:h + 1]
            weighted.append((xdt * jnp.exp(a_last - col)).astype(BF16))
            decay.append(jnp.broadcast_to(jnp.exp(a_last), (1, SSD_P)))
        contrib = _dot_tn(bg, jnp.concatenate(weighted, axis=1))
        state_ref[g] = state * jnp.concatenate(decay, axis=1) + contrib

    y = y_ref[...] * _silu(z_ref[0])
    gw = SSD_INNER // SSD_GROUPS
    for g in range(SSD_GROUPS):
        o_ref[0, :, g * gw:(g + 1) * gw] = _rms(y[:, g * gw:(g + 1) * gw],
                                                 ng_ref[:, g * gw:(g + 1) * gw]).astype(o_ref.dtype)


def _tri(n, upper):
    m = np.triu(np.ones((n, n), np.float32)) if upper else np.tril(np.ones((n, n), np.float32))
    return jnp.asarray(m, dtype=BF16)


def _lane_row(v, offset=0):
    row = jnp.zeros((1, LANE), F32)
    return row.at[0, offset:offset + v.shape[0]].set(v)


def ssd_mixer(z, xbc, dts, conv_w, conv_b, dt_bias, a_log, d_skip, norm_g):
    b, s, _ = z.shape
    cl = SSD_CHUNK
    hpg = SSD_HEADS // SSD_GROUPS

    def full(a):
        return pl.BlockSpec(a.shape, lambda bi, c: (0,) * a.ndim)

    args = (conv_w, conv_b.reshape(1, -1), _lane_row(dt_bias), _lane_row(a_log),
            jnp.repeat(d_skip, SSD_P).reshape(1, SSD_INNER), norm_g.reshape(1, SSD_INNER),
            _tri(cl, False), _tri(cl, True))
    return pl.pallas_call(
        _ssd_kernel,
        grid=(b, s // cl),
        in_specs=[pl.BlockSpec((1, cl, SSD_INNER), lambda bi, c: (bi, c, 0)),
                  pl.BlockSpec((1, cl, SSD_CONV_DIM), lambda bi, c: (bi, c, 0)),
                  pl.BlockSpec((1, cl, LANE), lambda bi, c: (bi, c, 0))] + [full(a) for a in args],
        out_specs=pl.BlockSpec((1, cl, SSD_INNER), lambda bi, c: (bi, c, 0)),
        out_shape=jax.ShapeDtypeStruct((b, s, SSD_INNER), BF16),
        scratch_shapes=[pltpu.VMEM((CONV_HALO, SSD_CONV_DIM), F32),
                        pltpu.VMEM((SSD_GROUPS, SSD_N, hpg * SSD_P), F32),
                        pltpu.VMEM((cl, SSD_INNER), F32)],
        compiler_params=_cparams(("parallel", "arbitrary")),
        name="ssd_mixer",
    )(z, xbc, dts, *args)


GDN_STEP_CHUNKS = 4


def _pair_diag(x):
    cl = x.shape[0]
    left = jnp.where(lax.broadcasted_iota(jnp.int32, x.shape, 1) < cl, 1.0, 0.0).astype(BF16)
    return jnp.concatenate([x * left, x * (1.0 - left).astype(BF16)], axis=0)


def _mm_pairs(a, b_diag):
    return _dot(a[0], b_diag[0]) + _dot(a[0], b_diag[1]) + _dot(a[1], b_diag[0])


def _unit_lower_inverses(a_list, row, col):
    eye = jnp.where(row == col, 1.0, 0.0)
    same16 = (row // 16) == (col // 16)
    same32 = (row // 32) == (col // 32)

    def diag_parts(x):
        hi, lo = _split2(x)
        return (hi, lo), (_pair_diag(hi), _pair_diag(lo))

    n1 = [jnp.where(same16, -a, 0.0) for a in a_list]
    p = [eye + x for x in n1]
    n = [diag_parts(x) for x in n1]
    for _ in range(3):
        n = [diag_parts(_mm_pairs(x, xd)) for x, xd in n]
        p = [x + _mm_pairs(_split2(x), yd) for x, (_, yd) in zip(p, n)]
    for level_mask in (jnp.where(same16, 0.0, jnp.where(same32, 1.0, 0.0)), jnp.where(same32, 0.0, 1.0)):
        off = [diag_parts(a * level_mask)[1] for a in a_list]
        ps = [diag_parts(x) for x in p]
        t = [_split2(_mm_pairs(x, y)) for (x, _), y in zip(ps, off)]
        p = [x - _mm_pairs(y, zd) for x, y, (_, zd) in zip(p, t, ps)]
    return p


def _gdn_kernel(qkv_ref, z_ref, sm_ref, cw_ref, dtb_ref, alog_ref, ng_ref, tril_ref, triu_ref,
                o_ref, carry_ref, state_ref):
    cl = GDN_CHUNK
    rows = qkv_ref.shape[1]
    heads = range(GDN_HEADS)
    chunks = range(rows // cl)
    pairs = [(c, j) for c in chunks for j in range(GDN_HEADS // 2)]

    @pl.when(pl.program_id(1) == 0)
    def _():
        carry_ref[...] = jnp.zeros_like(carry_ref)
        state_ref[...] = jnp.zeros_like(state_ref)

    qkv = qkv_ref[0]
    ext = jnp.concatenate([carry_ref[...], qkv], axis=0)
    act = _silu(_causal_conv(ext, cw_ref[...], CONV_HALO))
    carry_ref[...] = qkv[rows - CONV_HALO:, :]

    sm = sm_ref[0]
    beta_all = jax.nn.sigmoid(sm)
    g_all = -jnp.exp(alog_ref[...]) * _softplus(sm + dtb_ref[...])
    row = lax.broadcasted_iota(jnp.int32, (cl, 2 * cl), 0)
    lane = lax.broadcasted_iota(jnp.int32, (cl, 2 * cl), 1)
    first = lane < cl
    col = jnp.where(first, lane, lane - cl)
    incl = row >= col

    def l2n(x):
        return x * lax.rsqrt(jnp.sum(x * x, axis=-1, keepdims=True) + NORM_EPS)

    def chunk(x, c):
        return x[c * cl:(c + 1) * cl]

    qn = [l2n(act[:, h * GDN_DK:(h + 1) * GDN_DK]) * GDN_DK ** -0.5 for h in heads]
    kn = [l2n(act[:, GDN_QK_W + h * GDN_DK:GDN_QK_W + (h + 1) * GDN_DK]) for h in heads]
    v = [act[:, 2 * GDN_QK_W + h * GDN_DV:2 * GDN_QK_W + (h + 1) * GDN_DV] for h in heads]
    beta = [beta_all[:, h:h + 1] for h in heads]
    kb = [kn[h] * beta[h] for h in heads]
    knb = [x.astype(BF16) for x in kn]
    gcs_c = [_cumsum_cols(tril_ref[...], chunk(g_all, c)) for c in chunks]
    gcs_r = [_cumsum_rows_t(chunk(g_all, c), triu_ref[...]) for c in chunks]
    gc = {(c, h): gcs_c[c][:, GDN_HEADS + h:GDN_HEADS + h + 1] for c in chunks for h in heads}
    eg = {k: jnp.exp(g) for k, g in gc.items()}

    def decay(c, j):
        g_col = jnp.where(first, gc[c, 2 * j], gc[c, 2 * j + 1])
        g_row = jnp.concatenate([gcs_r[c][GDN_HEADS + 2 * j + i:GDN_HEADS + 2 * j + i + 1, :] for i in range(2)],
                                axis=1)
        return jnp.exp(jnp.where(incl, g_col - g_row, MASKED))

    dec = {k: decay(*k) for k in pairs}
    zeros_k = jnp.zeros((cl, GDN_DK), BF16)

    def against_keys(x, c, j):
        k0, k1 = chunk(knb[2 * j], c), chunk(knb[2 * j + 1], c)
        return (_dot_nt(chunk(x[2 * j], c).astype(BF16), jnp.concatenate([k0, zeros_k], axis=0))
                + _dot_nt(chunk(x[2 * j + 1], c).astype(BF16), jnp.concatenate([zeros_k, k1], axis=0)))

    a_kk = [jnp.where(row > col, against_keys(kb, c, j) * dec[c, j], 0.0) for c, j in pairs]
    a_qk = {(c, j): (against_keys(qn, c, j) * dec[c, j]).astype(BF16) for c, j in pairs}
    t_inv = [t.astype(BF16) for t in _unit_lower_inverses(a_kk, row, col)]

    def stacked(x0, x1):
        z = jnp.zeros_like(x0)
        return jnp.concatenate([jnp.concatenate([x0, z], axis=1), jnp.concatenate([z, x1], axis=1)], axis=0)

    def rhs(c, h):
        return jnp.concatenate([chunk(v[h], c) * chunk(beta[h], c), chunk(kb[h], c) * eg[c, h]], axis=1).astype(BF16)

    uw = {(c, j): _dot(t, stacked(rhs(c, 2 * j), rhs(c, 2 * j + 1))) for (c, j), t in zip(pairs, t_inv)}
    width = GDN_DV + GDN_DK

    state = [state_ref[h] for h in heads]
    for c in chunks:
        sb = [s.astype(BF16) for s in state]
        u = [uw[c, h // 2][:, (h % 2) * width:(h % 2) * width + GDN_DV] for h in heads]
        w = [uw[c, h // 2][:, (h % 2) * width + GDN_DV:(h % 2 + 1) * width] for h in heads]
        ws = [_dot(jnp.concatenate([w[h], chunk(qn[h], c) * eg[c, h]], axis=0).astype(BF16), sb[h]) for h in heads]
        vb = [(u[h] - ws[h][:cl]).astype(BF16) for h in heads]
        intra = [_dot(a_qk[c, j], stacked(vb[2 * j], vb[2 * j + 1])) for j in range(GDN_HEADS // 2)]
        o = [ws[h][cl:] + intra[h // 2][:, (h % 2) * GDN_DV:(h % 2 + 1) * GDN_DV] for h in heads]
        g_last = [gc[c, h][cl - 1:cl, :] for h in heads]
        state = [state[h] * jnp.exp(g_last[h])
                 + _dot_tn((chunk(kn[h], c) * jnp.exp(g_last[h] - gc[c, h])).astype(BF16), vb[h]) for h in heads]
        for h in heads:
            out = _rms(o[h], ng_ref[...]) * _silu(z_ref[0, c * cl:(c + 1) * cl, h * GDN_DV:(h + 1) * GDN_DV])
            o_ref[0, c * cl:(c + 1) * cl, h * GDN_DV:(h + 1) * GDN_DV] = out.astype(o_ref.dtype)
    for h in heads:
        state_ref[h] = state[h]


def gdn_mixer(qkv, z, sm, conv_w, dt_bias, a_log, norm_g):
    b, s, _ = qkv.shape
    cl = GDN_CHUNK

    def full(a):
        return pl.BlockSpec(a.shape, lambda bi, c: (0,) * a.ndim)

    args = (conv_w, _lane_row(dt_bias, GDN_HEADS), _lane_row(a_log, GDN_HEADS), norm_g.reshape(1, GDN_DV),
            _tri(cl, False), _tri(cl, True))
    rows = GDN_STEP_CHUNKS * cl
    assert s % rows == 0
    return pl.pallas_call(
        _gdn_kernel,
        grid=(b, s // rows),
        in_specs=[pl.BlockSpec((1, rows, GDN_CONV_DIM), lambda bi, c: (bi, c, 0)),
                  pl.BlockSpec((1, rows, GDN_V_W), lambda bi, c: (bi, c, 0)),
                  pl.BlockSpec((1, rows, LANE), lambda bi, c: (bi, c, 0))] + [full(a) for a in args],
        out_specs=pl.BlockSpec((1, rows, GDN_V_W), lambda bi, c: (bi, c, 0)),
        out_shape=jax.ShapeDtypeStruct((b, s, GDN_V_W), BF16),
        scratch_shapes=[pltpu.VMEM((CONV_HALO, GDN_CONV_DIM), F32),
                        pltpu.VMEM((GDN_HEADS, GDN_DK, GDN_DV), F32)],
        compiler_params=_cparams(("parallel", "arbitrary")),
        name="gdn_mixer",
    )(qkv, z, sm, *args)


def _pad_cols(w, width=LANE):
    return jnp.pad(w, ((0, 0), (0, width - w.shape[1])))


HY_Q_W = NSA_HEADS * NSA_DH
HY_KV_W = 6 * NSA_GROUPS * NSA_DH
HY_GATE_W = NSA_HEADS * 3


def hybrid_heads(x2d, b, s, positions, g_pre, w_in, pe_k, pe_v, ck_w1, ck_b1, ck_w2,
                 cv_w1, cv_b1, cv_w2, conv_w, conv_b, dt_bias, a_log, d_skip, norm_g):
    o = 0
    cols = {}
    for name, width in (("q", HY_Q_W), ("kv", HY_KV_W), ("gate", HY_GATE_W), ("z", SSD_INNER),
                        ("xbc", SSD_CONV_DIM), ("dt", SSD_HEADS)):
        cols[name] = w_in[:, o:o + width]
        o += width
    gpg = HY_GATE_W // NSA_GROUPS
    w_cat = jnp.concatenate(
        [cols["q"], cols["kv"], cols["z"], cols["xbc"]]
        + [_pad_cols(cols["gate"][:, g * gpg:(g + 1) * gpg]) for g in range(NSA_GROUPS)]
        + [_pad_cols(cols["dt"])], axis=1).astype(BF16)
    splits = (HY_Q_W, HY_KV_W, SSD_INNER, SSD_CONV_DIM, NSA_GROUPS * LANE, LANE)
    q, kv, z, xbc, gates, dts = norm_matmul(x2d, g_pre, w_cat, splits)

    qh, kcmp, vcmp, kslc, vslc, kwin, vwin = nsa_prep(q.reshape(b, s, -1), kv.reshape(b, s, -1), positions)
    kc, vc = nsa_compress(kcmp, vcmp, pe_k, pe_v, ck_w1, ck_b1, ck_w2, cv_w1, cv_b1, cv_w2)
    o_nsa = nsa_attention(qh, kc, vc, kslc, vslc, kwin, vwin, gates.reshape(b, s, -1))
    y = ssd_mixer(z.reshape(b, s, -1), xbc.reshape(b, s, -1), dts.reshape(b, s, -1),
                  conv_w, conv_b, dt_bias, a_log, d_skip, norm_g)
    return [o_nsa.reshape(b * s, -1), y.reshape(b * s, -1)]


def gdn_heads(x2d, b, s, g_pre, w_in, conv_w, dt_bias, a_log, norm_g):
    w_cat = jnp.concatenate([w_in[:, :GDN_CONV_DIM + GDN_V_W], _pad_cols(w_in[:, GDN_CONV_DIM + GDN_V_W:])],
                            axis=1).astype(BF16)
    qkv, z, sm = norm_matmul(x2d, g_pre, w_cat, (GDN_CONV_DIM, GDN_V_W, LANE))
    o = gdn_mixer(qkv.reshape(b, s, -1), z.reshape(b, s, -1), sm.reshape(b, s, -1), conv_w, dt_bias, a_log, norm_g)
    return [o.reshape(b * s, -1)]


def kernel(x, positions, norm_mix_pre, norm_mix_post, norm_ffn_pre, norm_ffn_post, hy_w_in, hy_w_out, nsa_pe_k, nsa_pe_v, nsa_ck_w1, nsa_ck_b1, nsa_ck_w2, nsa_cv_w1, nsa_cv_b1, nsa_cv_w2, ssd_conv_w, ssd_conv_b, ssd_dt_bias, ssd_a_log, ssd_d, ssd_norm, gdn_w_in, gdn_conv_w, gdn_dt_bias, gdn_a_log, gdn_norm, gdn_w_out, ffn_w_up, ffn_conv_w, ffn_conv_b, ffn_w_down):
    b, s, d = x.shape
    x2d = x.reshape(b * s, d)
    depth = norm_mix_pre.shape[0]
    for layer in range(depth):
        e = layer // 2
        if layer % 2 == 0:
            parts = hybrid_heads(x2d, b, s, positions, norm_mix_pre[layer], hy_w_in[e], nsa_pe_k[e], nsa_pe_v[e],
                                 nsa_ck_w1[e], nsa_ck_b1[e], nsa_ck_w2[e], nsa_cv_w1[e], nsa_cv_b1[e], nsa_cv_w2[e],
                                 ssd_conv_w[e], ssd_conv_b[e], ssd_dt_bias[e], ssd_a_log[e], ssd_d[e], ssd_norm[e])
            w_out = hy_w_out[e]
        else:
            parts = gdn_heads(x2d, b, s, norm_mix_pre[layer], gdn_w_in[e], gdn_conv_w[e], gdn_dt_bias[e],
                              gdn_a_log[e], gdn_norm[e])
            w_out = gdn_w_out[e]
        x2d = mixer_out_ffn(x2d, parts, w_out.astype(BF16), norm_mix_post[layer], s, norm_ffn_pre[layer],
                            ffn_w_up[layer].astype(BF16), ffn_conv_w[layer], ffn_conv_b[layer],
                            ffn_w_down[layer].astype(BF16), norm_ffn_post[layer])
    return x2d.reshape(b, s, d)
```

```python
import functools
import math

import numpy as np
import jax
import jax.numpy as jnp
from jax import lax
from jax.experimental import pallas as pl
from jax.experimental.pallas import tpu as pltpu

F32 = jnp.float32
BF16 = jnp.bfloat16

D_MODEL = 1024
NORM_EPS = 1e-6
MASKED = -1e30

NSA_HEADS = 8
NSA_GROUPS = 2
NSA_REP = NSA_HEADS // NSA_GROUPS
NSA_DH = 64
CMP_BLOCK = 32
CMP_STRIDE = 16
SLC_BLOCK = 64
SLC_TOPK = 16
WINDOW = 512
ROPE_THETA = 500000.0
ROT_DIM = NSA_DH // 4
FORCED_SCORE = 1e9

SSD_HEADS = 8
SSD_P = 64
SSD_INNER = SSD_HEADS * SSD_P
SSD_GROUPS = 2
SSD_N = 128
SSD_CHUNK = 256
SSD_CONV_DIM = SSD_INNER + 2 * SSD_GROUPS * SSD_N

GDN_HEADS = 8
GDN_DK = 128
GDN_DV = 128
GDN_CHUNK = 64
GDN_QK_W = GDN_HEADS * GDN_DK
GDN_V_W = GDN_HEADS * GDN_DV
GDN_CONV_DIM = 2 * GDN_QK_W + GDN_V_W

FFN_DIM = 2816

LANE = 128
CONV_HALO = 8
FFN_HALO = 16
VMEM_LIMIT = 56 * 1024 * 1024
ROW_TILE = 512
PREP_TILE = 256


def _cparams(sem):
    return pltpu.CompilerParams(dimension_semantics=sem, vmem_limit_bytes=VMEM_LIMIT)


def _rms(x, g):
    return x * lax.rsqrt(jnp.mean(x * x, axis=-1, keepdims=True) + NORM_EPS) * g


def _silu(x):
    return x * jax.nn.sigmoid(x)


def _softplus(x):
    return jnp.maximum(x, 0.0) + jnp.log1p(jnp.exp(-jnp.abs(x)))


def _dot(a, b):
    return jnp.dot(a, b, preferred_element_type=F32)


def _dot_nt(a, b):
    return lax.dot_general(a, b, (((1,), (1,)), ((), ())), preferred_element_type=F32)


def _dot_tn(a, b):
    return lax.dot_general(a, b, (((0,), (0,)), ((), ())), preferred_element_type=F32)


def _split3(x):
    x1 = x.astype(BF16)
    r1 = x - x1.astype(F32)
    x2 = r1.astype(BF16)
    x3 = (r1 - x2.astype(F32)).astype(BF16)
    return x1, x2, x3


def _split2(x):
    x1 = x.astype(BF16)
    x2 = (x - x1.astype(F32)).astype(BF16)
    return x1, x2


def _cumsum_cols(tril, a):
    a1, a2, a3 = _split3(a)
    return _dot(tril, a1) + _dot(tril, a2) + _dot(tril, a3)


def _cumsum_rows_t(a, triu):
    a1, a2, a3 = _split3(a)
    return _dot_tn(a1, triu) + _dot_tn(a2, triu) + _dot_tn(a3, triu)


def _causal_conv(ext, cw, halo):
    width = cw.shape[0]
    y = ext[halo:] * cw[width - 1:width]
    for k in range(1, width):
        y = y + pltpu.roll(ext, k, 0)[halo:] * cw[width - 1 - k:width - k]
    return y


def _inproj_kernel(x_ref, g_ref, w_ref, *out_refs, splits):
    hn = _rms(x_ref[...], g_ref[...]).astype(BF16)
    y = _dot(hn, w_ref[...])
    off = 0
    for o_ref, n in zip(out_refs, splits):
        o_ref[...] = y[:, off:off + n].astype(o_ref.dtype)
        off += n


def norm_matmul(x2d, g, w, splits, tm=ROW_TILE):
    t = x2d.shape[0]
    n = w.shape[1]
    assert sum(splits) == n and t % tm == 0
    return pl.pallas_call(
        functools.partial(_inproj_kernel, splits=splits),
        grid=(t // tm,),
        in_specs=[pl.BlockSpec((tm, D_MODEL), lambda i: (i, 0)),
                  pl.BlockSpec((1, D_MODEL), lambda i: (0, 0)),
                  pl.BlockSpec((D_MODEL, n), lambda i: (0, 0))],
        out_specs=[pl.BlockSpec((tm, s), lambda i: (i, 0)) for s in splits],
        out_shape=[jax.ShapeDtypeStruct((t, s), F32) for s in splits],
        compiler_params=_cparams(("parallel",)),
        name="norm_matmul",
    )(x2d, g.reshape(1, D_MODEL), w)


FFN_CHUNK = 256
FFN_DOWN_CHUNKS = 2


def _ffn_kernel(*refs, n_parts, tm, seq):
    x_ref, halo_ref = refs[:2]
    part_refs = refs[2:2 + n_parts]
    part_halo_refs = refs[2 + n_parts:2 + 2 * n_parts]
    (wo_ref, gmix_ref, gpre_ref, wup_ref, cw_ref, cb_ref, wd_ref, gpost_ref, o_ref, hn_ref) = refs[2 + 2 * n_parts:]

    def after_mixer(x, parts):
        acc = None
        off = 0
        for p_ref in parts:
            k = p_ref.shape[-1]
            part = _dot(p_ref[...], wo_ref[off:off + k, :])
            acc = part if acc is None else acc + part
            off += k
        return x + _rms(acc, gmix_ref[...])

    o_ref[...] = after_mixer(x_ref[...], part_refs)
    keep = ((pl.program_id(0) * tm) % seq != 0).astype(F32)
    hn_ref[:FFN_HALO, :] = (_rms(after_mixer(halo_ref[...], part_halo_refs), gpre_ref[...]) * keep).astype(BF16)
    hn_ref[FFN_HALO:, :] = _rms(o_ref[...], gpre_ref[...]).astype(BF16)
    hn = hn_ref[...]

    def up(lo):
        return _dot(hn, wup_ref[:, lo:lo + FFN_CHUNK]), _dot(hn, wup_ref[:, FFN_DIM + lo:FFN_DIM + lo + FFN_CHUNK])

    def conv(u, lo):
        return _causal_conv(u, cw_ref[:, lo:lo + FFN_CHUNK], FFN_HALO) + cb_ref[:, lo:lo + FFN_CHUNK]

    acc = None
    pending = []
    n_chunks = FFN_DIM // FFN_CHUNK
    ahead = up(0)
    for c in range(n_chunks):
        lo = c * FFN_CHUNK
        u_gate, u_val = ahead
        if c + 1 < n_chunks:
            ahead = up(lo + FFN_CHUNK)
        pending.append((_silu(conv(u_gate, lo)) * conv(u_val, FFN_DIM + lo)).astype(BF16))
        if len(pending) == FFN_DOWN_CHUNKS or c == n_chunks - 1:
            width = len(pending) * FFN_CHUNK
            h = pending[0] if len(pending) == 1 else jnp.concatenate(pending, axis=1)
            part = _dot(h, wd_ref[lo + FFN_CHUNK - width:lo + FFN_CHUNK, :])
            acc = part if acc is None else acc + part
            pending = []
    o_ref[...] = o_ref[...] + _rms(acc, gpost_ref[...])


def mixer_out_ffn(x2d, parts, w_out, gmix, seq, gpre, w_up, conv_w, conv_b, w_down, gpost, tm=ROW_TILE):
    t = x2d.shape[0]
    assert FFN_DIM % FFN_CHUNK == 0 and t % tm == 0 and seq % tm == 0 and tm % FFN_HALO == 0
    hb = tm // FFN_HALO

    def resident(a):
        return pl.BlockSpec(a.shape, lambda i: (0,) * a.ndim, pipeline_mode=pl.Buffered(1))

    def halo(width):
        return pl.BlockSpec((FFN_HALO, width), lambda i: (jnp.maximum(i * hb - 1, 0), 0))

    consts = (w_out, gmix.reshape(1, D_MODEL), gpre.reshape(1, D_MODEL), w_up, conv_w,
              conv_b.reshape(1, 2 * FFN_DIM), w_down, gpost.reshape(1, D_MODEL))
    return pl.pallas_call(
        functools.partial(_ffn_kernel, n_parts=len(parts), tm=tm, seq=seq),
        grid=(t // tm,),
        in_specs=[pl.BlockSpec((tm, D_MODEL), lambda i: (i, 0)), halo(D_MODEL)]
        + [pl.BlockSpec((tm, p.shape[1]), lambda i: (i, 0)) for p in parts]
        + [halo(p.shape[1]) for p in parts]
        + [resident(a) for a in consts],
        out_specs=pl.BlockSpec((tm, D_MODEL), lambda i: (i, 0)),
        out_shape=jax.ShapeDtypeStruct((t, D_MODEL), F32),
        scratch_shapes=[pltpu.VMEM((tm + FFN_HALO, D_MODEL), BF16)],
        compiler_params=_cparams(("parallel",)),
        name="mixer_out_ffn",
    )(x2d, x2d, *parts, *parts, *consts)


def _nsa_prep_kernel(q_ref, kv_ref, pos_ref, freq_ref, ecos_ref, esin_ref, rest_ref,
                     qh_ref, kcmp_ref, vcmp_ref, kslc_ref, vslc_ref, kwin_ref, vwin_ref):
    ang = freq_ref[...] * pos_ref[0].astype(F32)

    def spread(t, e_ref):
        t1, t2, t3 = _split3(t)
        e = e_ref[...]
        return _dot_tn(t1, e) + _dot_tn(t2, e) + _dot_tn(t3, e)

    cs = spread(jnp.cos(ang), ecos_ref) + rest_ref[...]
    sn = spread(jnp.sin(ang), esin_ref)
    lane = lax.broadcasted_iota(jnp.int32, (1, LANE), 1) % NSA_DH
    first_half = lane < ROT_DIM // 2

    def rope(x):
        partner = jnp.where(first_half, pltpu.roll(x, LANE - ROT_DIM // 2, 1), pltpu.roll(x, ROT_DIM // 2, 1))
        return x * cs + partner * sn

    ts = q_ref.shape[1]
    aug_w = kslc_ref.shape[-1]
    low_half = lax.broadcasted_iota(jnp.int32, (1, LANE), 1) < NSA_DH

    def widen(x, upper, fill):
        low = jnp.where(low_half, pltpu.roll(x, NSA_DH, 1) if upper else x, fill[:, :LANE])
        return low if aug_w == LANE else jnp.concatenate([low, fill[:, LANE:]], axis=1)

    scale = NSA_DH ** -0.5 * math.log2(math.e)
    for j in range(NSA_HEADS // 2):
        t = (rope(q_ref[0, :, j * LANE:(j + 1) * LANE]) * scale).T
        qh_ref[0, 2 * j] = t[:NSA_DH].astype(BF16)
        qh_ref[0, 2 * j + 1] = t[NSA_DH:].astype(BF16)

    tok = pl.program_id(1) * ts + lax.broadcasted_iota(jnp.int32, (ts, 1), 0)
    block_onehot = jnp.where(lax.broadcasted_iota(jnp.int32, (1, aug_w), 1) - NSA_DH == tok // SLC_BLOCK, 1.0, 0.0)
    t = rope(kv_ref[0, :, 2 * LANE:3 * LANE])
    for g in range(NSA_GROUPS):
        kslc_ref[0, g] = widen(t, g == 1, block_onehot).astype(BF16)

    for i, o_ref in ((0, kcmp_ref), (4, kwin_ref), (1, vcmp_ref)):
        t = kv_ref[0, :, i * LANE:(i + 1) * LANE]
        if i % 2 == 0:
            t = rope(t)
        for g in range(NSA_GROUPS):
            o_ref[0, g] = t[:, g * NSA_DH:(g + 1) * NSA_DH].astype(o_ref.dtype)
    ones_row = jnp.where(lax.broadcasted_iota(jnp.int32, (NSA_VROWS - NSA_DH, ts), 0) == 0, 1.0, 0.0)
    for i, o_ref in ((3, vslc_ref), (5, vwin_ref)):
        t = kv_ref[0, :, i * LANE:(i + 1) * LANE].T
        for g in range(NSA_GROUPS):
            o_ref[0, g] = jnp.concatenate([t[g * NSA_DH:(g + 1) * NSA_DH, :], ones_row], axis=0).astype(o_ref.dtype)


NSA_VROWS = NSA_DH + 16


def _aug_width(seq):
    return -(-(NSA_DH + seq // SLC_BLOCK) // LANE) * LANE


def nsa_prep(q, kv, positions, ts=PREP_TILE):
    b, s, _ = q.shape
    half = ROT_DIM // 2
    freq_rows = 16
    inv_freq = ROPE_THETA ** (-jnp.arange(0, ROT_DIM, 2, dtype=F32) / ROT_DIM)
    freq = jnp.zeros((freq_rows, 1), F32).at[:half, 0].set(inv_freq)
    d = np.arange(LANE) % NSA_DH
    j = np.arange(freq_rows)[:, None]
    e_cos = ((d[None, :] < ROT_DIM) & (d[None, :] % half == j)).astype(np.float32)
    e_sin = e_cos * np.where(d[None, :] < half, -1.0, 1.0)
    rest = (d >= ROT_DIM).astype(np.float32).reshape(1, LANE)
    tables = (jnp.asarray(e_cos, dtype=BF16), jnp.asarray(e_sin, dtype=BF16), jnp.asarray(rest))
    tok_shape = (b, NSA_GROUPS, s, NSA_DH)
    tok_spec = pl.BlockSpec((1, NSA_GROUPS, ts, NSA_DH), lambda bi, i: (bi, 0, i, 0))
    feat_shape = (b, NSA_GROUPS, NSA_VROWS, s)
    feat_spec = pl.BlockSpec((1, NSA_GROUPS, NSA_VROWS, ts), lambda bi, i: (bi, 0, 0, i))
    aug_w = _aug_width(s)
    return pl.pallas_call(
        _nsa_prep_kernel,
        grid=(b, s // ts),
        in_specs=[pl.BlockSpec((1, ts, q.shape[2]), lambda bi, i: (bi, i, 0)),
                  pl.BlockSpec((1, ts, kv.shape[2]), lambda bi, i: (bi, i, 0)),
                  pl.BlockSpec((1, 1, ts), lambda bi, i: (bi, 0, i)),
                  pl.BlockSpec((freq_rows, 1), lambda bi, i: (0, 0))]
        + [pl.BlockSpec(t.shape, lambda bi, i: (0, 0)) for t in tables],
        out_specs=[pl.BlockSpec((1, NSA_HEADS, NSA_DH, ts), lambda bi, i: (bi, 0, 0, i)),
                   tok_spec, tok_spec,
                   pl.BlockSpec((1, NSA_GROUPS, ts, aug_w), lambda bi, i: (bi, 0, i, 0)),
                   feat_spec, tok_spec, feat_spec],
        out_shape=[jax.ShapeDtypeStruct((b, NSA_HEADS, NSA_DH, s), BF16),
                   jax.ShapeDtypeStruct(tok_shape, F32), jax.ShapeDtypeStruct(tok_shape, F32),
                   jax.ShapeDtypeStruct((b, NSA_GROUPS, s, aug_w), BF16), jax.ShapeDtypeStruct(feat_shape, BF16),
                   jax.ShapeDtypeStruct(tok_shape, BF16), jax.ShapeDtypeStruct(feat_shape, BF16)],
        compiler_params=_cparams(("parallel", "parallel")),
        name="nsa_prep",
    )(q, kv, positions.reshape(b, 1, s), freq, *tables)


def _compress_kernel(k_ref, v_ref, pek_ref, pev_ref, kw1_ref, vw1_ref, kb1_ref, vb1_ref, kw2_ref, vw2_ref,
                     kc_ref, vc_ref):
    half = CMP_STRIDE * NSA_DH

    def mlp(t_ref, pe_ref, w1_ref, b1_ref, w2_ref):
        a = t_ref[0, 0]
        n = a.shape[0]
        h_lo = _dot((a + pe_ref[:, :half]).astype(BF16), w1_ref[:half, :])
        h_hi = _dot((a + pe_ref[:, half:]).astype(BF16), w1_ref[half:, :])
        hid = _silu(h_lo + pltpu.roll(h_hi, n - 1, 0) + b1_ref[...])
        return _dot(hid.astype(BF16), w2_ref[...])

    kc_ref[0, 0] = mlp(k_ref, pek_ref, kw1_ref, kb1_ref, kw2_ref).astype(kc_ref.dtype)
    vc_ref[0, 0] = mlp(v_ref, pev_ref, vw1_ref, vb1_ref, vw2_ref).astype(vc_ref.dtype).T


def nsa_compress(kcmp, vcmp, pe_k, pe_v, ck_w1, ck_b1, ck_w2, cv_w1, cv_b1, cv_w2):
    b, g, s, dh = kcmp.shape
    n = s // CMP_STRIDE
    wide = CMP_STRIDE * dh
    assert CMP_BLOCK == 2 * CMP_STRIDE
    kr = kcmp.reshape(b, g, n, wide)
    vr = vcmp.reshape(b, g, n, wide)
    blk = pl.BlockSpec((1, 1, n, wide), lambda bi, gi: (bi, gi, 0, 0))

    def full(a):
        return pl.BlockSpec(a.shape, lambda bi, gi: (0,) * a.ndim)

    args = (pe_k.reshape(1, 2 * wide), pe_v.reshape(1, 2 * wide), ck_w1.astype(BF16), cv_w1.astype(BF16),
            ck_b1.reshape(1, dh), cv_b1.reshape(1, dh), ck_w2.astype(BF16), cv_w2.astype(BF16))
    return pl.pallas_call(
        _compress_kernel,
        grid=(b, g),
        in_specs=[blk, blk] + [full(a) for a in args],
        out_specs=[pl.BlockSpec((1, 1, n, dh), lambda bi, gi: (bi, gi, 0, 0)),
                   pl.BlockSpec((1, 1, dh, n), lambda bi, gi: (bi, gi, 0, 0))],
        out_shape=[jax.ShapeDtypeStruct((b, g, n, dh), BF16), jax.ShapeDtypeStruct((b, g, dh, n), BF16)],
        compiler_params=_cparams(("parallel", "parallel")),
        name="nsa_compress",
    )(kr, vr, *args)


ATT_TQ = 256
ATT_TK = 256
ATT_RING = 4


def _att_kernel(q_ref, kc_ref, vc_ref, ks_ref, vs_ref, kw_ref, vw_ref, gate_ref, ov_ref, o_ref,
                *s_refs, seq):
    tq, tk, rep, dh = ATT_TQ, ATT_TK, NSA_REP, NSA_DH
    cols = rep * tq
    nblk = seq // SLC_BLOCK
    ncmp = kc_ref.shape[2]
    aug_w = ks_ref.shape[-1]
    t0 = pl.program_id(2) * tq
    q = jnp.concatenate([q_ref[0, r] for r in range(rep)], axis=1)
    tpos = t0 + lax.broadcasted_iota(jnp.int32, (1, tq), 1)

    def per_head(x):
        return jnp.concatenate([x] * rep, axis=1)


    band = WINDOW + tq
    start = pl.multiple_of(jnp.maximum(t0 - WINDOW, 0), tq)
    dlt = tpos - (start + lax.broadcasted_iota(jnp.int32, (band, 1), 0))
    bias_w = jnp.where((dlt >= 0) & (dlt < WINDOW), 0.0, MASKED)
    s_w = _dot(kw_ref[0, 0, pl.ds(start, band), :], q) + per_head(bias_w)

    cmp_end = lax.broadcasted_iota(jnp.int32, (ncmp, 1), 0) * CMP_STRIDE + (CMP_BLOCK - 1)
    s_c = _dot(kc_ref[0, 0], q) + per_head(jnp.where(cmp_end <= tpos, 0.0, MASKED))
    p_c = jnp.exp2(s_c - jnp.max(s_c, axis=0, keepdims=True))
    l_c = jnp.sum(p_c, axis=0, keepdims=True)
    p_c = p_c * (per_head(jnp.where(tpos >= CMP_BLOCK - 1, 1.0, 0.0)) / l_c)
    o_c = _dot(vc_ref[0, 0], p_c.astype(BF16))

    p_sum = p_c[:, :tq]
    for r in range(1, rep):
        p_sum = p_sum + p_c[:, r * tq:(r + 1) * tq]
    p_hi, p_lo = _split2(p_sum)
    imp = _dot(ov_ref[...], p_hi) + _dot(ov_ref[...], p_lo)
    blk = lax.broadcasted_iota(jnp.int32, (nblk, 1), 0)
    cur = tpos // SLC_BLOCK
    forced = (blk == 0) | (blk == cur) | (blk == cur - 1)
    imp = jnp.where(forced, FORCED_SCORE, jnp.where(blk <= cur, imp, -1.0))
    sel = jnp.full((nblk, tq), MASKED, F32)
    for _ in range(min(SLC_TOPK, nblk)):
        best = jnp.max(imp, axis=0, keepdims=True)
        first = jnp.min(jnp.where(imp == best, blk, nblk), axis=0, keepdims=True)
        hit = blk == first
        sel = jnp.where(hit, 0.0, sel)
        imp = jnp.where(hit, -jnp.inf, imp)

    mask_rows = jnp.concatenate([sel.astype(BF16), jnp.zeros((aug_w - dh - nblk, tq), BF16)], axis=0)
    q_aug = jnp.concatenate([q, per_head(mask_rows)], axis=0)
    kidx = lax.broadcasted_iota(jnp.int32, (tk, 1), 0)

    def produce(kt, s_ref):
        k0 = pl.multiple_of(kt * tk, tk)
        s = _dot(ks_ref[0, 0, pl.ds(k0, tk), :], q_aug)
        s_ref[...] = s
        return jnp.max(s, axis=0, keepdims=True)

    def consume(kt, s_ref, tile_max, carry, causal):
        m, acc = carry
        k0 = pl.multiple_of(kt * tk, tk)
        v = vs_ref[0, 0, :, pl.ds(k0, tk)]
        s = s_ref[...]
        if causal:
            s = s + per_head(jnp.where(k0 + kidx <= tpos, 0.0, MASKED))
            tile_max = jnp.max(s, axis=0, keepdims=True)
        m_new = jnp.maximum(m, tile_max)
        alpha = jnp.exp2(m - m_new)
        p = jnp.exp2(s - m_new)
        return m_new, alpha * acc + _dot(v, p.astype(BF16))

    nbuf = len(s_refs)

    def slc_trip(j, carry):
        maxes, state = list(carry[:nbuf]), carry[nbuf:]
        for i, s_ref in enumerate(s_refs):
            state = consume(nbuf * j + i, s_ref, maxes[i], state, False)
            maxes[i] = produce(nbuf * (j + 1) + i, s_ref)
        return tuple(maxes) + state

    n_full = t0 // (nbuf * tk)
    init = tuple(produce(i, s_ref) for i, s_ref in enumerate(s_refs)) + (
        jnp.full((1, cols), MASKED, F32), jnp.zeros((NSA_VROWS, cols), F32))

    p_w = jnp.exp2(s_w - jnp.max(s_w, axis=0, keepdims=True))
    o_w = _dot(vw_ref[0, 0, :, pl.ds(start, band)], p_w.astype(BF16))
    o_w = o_w[:dh] / o_w[dh:dh + 1]

    carry = lax.fori_loop(0, n_full, slc_trip, init)
    maxes, state = carry[:nbuf], carry[nbuf:]
    own = (t0 - n_full * nbuf * tk) // tk
    for i, s_ref in enumerate(s_refs):
        kt = nbuf * n_full + i
        state = lax.cond(
            i < own,
            lambda st, kt=kt, s_ref=s_ref, mx=maxes[i]: consume(kt, s_ref, mx, st, False),
            lambda st, kt=kt, s_ref=s_ref: lax.cond(
                i == own, lambda st2: consume(kt, s_ref, None, st2, True), lambda st2: st2, st),
            state)
    acc_s = state[1]
    o_s = acc_s[:dh] / acc_s[dh:dh + 1]

    gates = jax.nn.sigmoid(gate_ref[0]).T
    merged = []
    for r in range(rep):
        sl = slice(r * tq, (r + 1) * tq)
        merged.append(gates[3 * r:3 * r + 1] * o_c[:, sl] + gates[3 * r + 1:3 * r + 2] * o_s[:, sl]
                      + gates[3 * r + 2:3 * r + 3] * o_w[:, sl])
    o_ref[0] = jnp.concatenate(merged, axis=0).T.astype(o_ref.dtype)


def _overlap_matrix(seq):
    ncp = seq // CMP_STRIDE
    n_cmp = (seq - CMP_BLOCK) // CMP_STRIDE + 1
    nblk = seq // SLC_BLOCK
    cs = np.arange(ncp) * CMP_STRIDE
    ss = np.arange(nblk) * SLC_BLOCK
    ov = ((cs[None, :] <= ss[:, None] + SLC_BLOCK - 1) & (cs[None, :] + CMP_BLOCK - 1 >= ss[:, None])
          & (np.arange(ncp)[None, :] < n_cmp))
    return jnp.asarray(ov.astype(np.float32), dtype=BF16)


def nsa_attention(qh, kc, vc, kslc, vslc, kwin, vwin, gates):
    b, _, dh, s = qh.shape
    aug_w = kslc.shape[-1]
    g = NSA_GROUPS
    ncp = kc.shape[2]
    nblk = s // SLC_BLOCK
    assert s % (ATT_RING * ATT_TK) == 0 and s >= WINDOW + ATT_TQ and aug_w == _aug_width(s)
    ov = _overlap_matrix(s)
    tok_spec = pl.BlockSpec((1, 1, s, dh), lambda bi, gi, i: (bi, gi, 0, 0))
    feat_spec = pl.BlockSpec((1, 1, NSA_VROWS, s), lambda bi, gi, i: (bi, gi, 0, 0))
    return pl.pallas_call(
        functools.partial(_att_kernel, seq=s),
        grid=(b, g, s // ATT_TQ),
        in_specs=[pl.BlockSpec((1, NSA_REP, dh, ATT_TQ), lambda bi, gi, i: (bi, gi, 0, i)),
                  pl.BlockSpec((1, 1, ncp, dh), lambda bi, gi, i: (bi, gi, 0, 0)),
                  pl.BlockSpec((1, 1, dh, ncp), lambda bi, gi, i: (bi, gi, 0, 0)),
                  pl.BlockSpec((1, 1, s, aug_w), lambda bi, gi, i: (bi, gi, 0, 0)),
                  feat_spec, tok_spec, feat_spec,
                  pl.BlockSpec((1, ATT_TQ, LANE), lambda bi, gi, i: (bi, i, gi)),
                  pl.BlockSpec((nblk, ncp), lambda bi, gi, i: (0, 0))],
        out_specs=pl.BlockSpec((1, ATT_TQ, NSA_REP * dh), lambda bi, gi, i: (bi, i, gi)),
        out_shape=jax.ShapeDtypeStruct((b, s, NSA_HEADS * dh), BF16),
        scratch_shapes=[pltpu.VMEM((ATT_TK, NSA_REP * ATT_TQ), F32)] * ATT_RING,
        compiler_params=_cparams(("parallel", "parallel", "arbitrary")),
        name="nsa_attention",
    )(qh, kc, vc, kslc, vslc, kwin, vwin, gates, ov)


def _ssd_kernel(z_ref, xbc_ref, dt_ref, cw_ref, cb_ref, dtb_ref, alog_ref, drow_ref, ng_ref, tril_ref, triu_ref,
                o_ref, carry_ref, state_ref, y_ref):
    cl = SSD_CHUNK
    hpg = SSD_HEADS // SSD_GROUPS

    @pl.when(pl.program_id(1) == 0)
    def _():
        carry_ref[...] = jnp.zeros_like(carry_ref)
        state_ref[...] = jnp.zeros_like(state_ref)

    xbc = xbc_ref[0]
    ext = jnp.concatenate([carry_ref[...], xbc], axis=0)
    act = _silu(_causal_conv(ext, cw_ref[...], CONV_HALO) + cb_ref[...])
    carry_ref[...] = xbc[cl - CONV_HALO:, :]

    xs = act[:, :SSD_INNER]
    dt = _softplus(dt_ref[0] + dtb_ref[...])
    a = dt * (-jnp.exp(alog_ref[...]))
    acs_c = _cumsum_cols(tril_ref[...], a)
    acs_r = _cumsum_rows_t(a, triu_ref[...])
    causal = lax.broadcasted_iota(jnp.int32, (cl, cl), 0) >= lax.broadcasted_iota(jnp.int32, (cl, cl), 1)

    for g in range(SSD_GROUPS):
        bg = act[:, SSD_INNER + g * SSD_N:SSD_INNER + (g + 1) * SSD_N].astype(BF16)
        cg = act[:, SSD_INNER + (SSD_GROUPS + g) * SSD_N:SSD_INNER + (SSD_GROUPS + g + 1) * SSD_N].astype(BF16)
        cb = _dot_nt(cg, bg)
        state = state_ref[g]
        y_off = _dot(cg, state.astype(BF16))
        weighted = []
        decay = []
        for hl in range(hpg):
            h = g * hpg + hl
            col = acs_c[:, h:h + 1]
            row = acs_r[h:h + 1, :]
            lmat = jnp.exp(jnp.where(causal, col - row, MASKED))
            xh = xs[:, h * SSD_P:(h + 1) * SSD_P]
            xdt = xh * dt[:, h:h + 1]
            y_diag = _dot((cb * lmat).astype(BF16), xdt.astype(BF16))
            y_ref[:, h * SSD_P:(h + 1) * SSD_P] = (y_diag + y_off[:, hl * SSD_P:(hl + 1) * SSD_P] * jnp.exp(col)
                                                   + drow_ref[:, h * SSD_P:(h + 1) * SSD_P] * xh)
            a_last = acs_c[cl - 1:cl, h:h + 1]
            weighted.append((xdt * jnp.exp(a_last - col)).astype(BF16))
            decay.append(jnp.broadcast_to(jnp.exp(a_last), (1, SSD_P)))
        contrib = _dot_tn(bg, jnp.concatenate(weighted, axis=1))
        state_ref[g] = state * jnp.concatenate(decay, axis=1) + contrib

    y = y_ref[...] * _silu(z_ref[0])
    gw = SSD_INNER // SSD_GROUPS
    for g in range(SSD_GROUPS):
        o_ref[0, :, g * gw:(g + 1) * gw] = _rms(y[:, g * gw:(g + 1) * gw],
                                                 ng_ref[:, g * gw:(g + 1) * gw]).astype(o_ref.dtype)


def _tri(n, upper):
    m = np.triu(np.ones((n, n), np.float32)) if upper else np.tril(np.ones((n, n), np.float32))
    return jnp.asarray(m, dtype=BF16)


def _lane_row(v, offset=0):
    row = jnp.zeros((1, LANE), F32)
    return row.at[0, offset:offset + v.shape[0]].set(v)


def ssd_mixer(z, xbc, dts, conv_w, conv_b, dt_bias, a_log, d_skip, norm_g):
    b, s, _ = z.shape
    cl = SSD_CHUNK
    hpg = SSD_HEADS // SSD_GROUPS

    def full(a):
        return pl.BlockSpec(a.shape, lambda bi, c: (0,) * a.ndim)

    args = (conv_w, conv_b.reshape(1, -1), _lane_row(dt_bias), _lane_row(a_log),
            jnp.repeat(d_skip, SSD_P).reshape(1, SSD_INNER), norm_g.reshape(1, SSD_INNER),
            _tri(cl, False), _tri(cl, True))
    return pl.pallas_call(
        _ssd_kernel,
        grid=(b, s // cl),
        in_specs=[pl.BlockSpec((1, cl, SSD_INNER), lambda bi, c: (bi, c, 0)),
                  pl.BlockSpec((1, cl, SSD_CONV_DIM), lambda bi, c: (bi, c, 0)),
                  pl.BlockSpec((1, cl, LANE), lambda bi, c: (bi, c, 0))] + [full(a) for a in args],
        out_specs=pl.BlockSpec((1, cl, SSD_INNER), lambda bi, c: (bi, c, 0)),
        out_shape=jax.ShapeDtypeStruct((b, s, SSD_INNER), BF16),
        scratch_shapes=[pltpu.VMEM((CONV_HALO, SSD_CONV_DIM), F32),
                        pltpu.VMEM((SSD_GROUPS, SSD_N, hpg * SSD_P), F32),
                        pltpu.VMEM((cl, SSD_INNER), F32)],
        compiler_params=_cparams(("parallel", "arbitrary")),
        name="ssd_mixer",
    )(z, xbc, dts, *args)


GDN_STEP_CHUNKS = 4


def _pair_diag(x):
    cl = x.shape[0]
    left = jnp.where(lax.broadcasted_iota(jnp.int32, x.shape, 1) < cl, 1.0, 0.0).astype(BF16)
    return jnp.concatenate([x * left, x * (1.0 - left).astype(BF16)], axis=0)


def _mm_pairs(a, b_diag):
    return _dot(a[0], b_diag[0]) + _dot(a[0], b_diag[1]) + _dot(a[1], b_diag[0])


def _unit_lower_inverses(a_list, row, col, fillers=()):
    eye = jnp.where(row == col, 1.0, 0.0)
    same16 = (row // 16) == (col // 16)
    same32 = (row // 32) == (col // 32)
    pending = list(fillers)

    def fill(anchor):
        if pending:
            pending.pop(0)(anchor[:1, :1] * 0.0)

    def diag_parts(x):
        hi, lo = _split2(x)
        return (hi, lo), (_pair_diag(hi), _pair_diag(lo))

    n1 = [jnp.where(same16, -a, 0.0) for a in a_list]
    p = [eye + x for x in n1]
    n = [diag_parts(x) for x in n1]
    anchor = p[-1]
    for _ in range(3):
        squares = [_mm_pairs(x, xd) for x, xd in n]
        fill(anchor)
        anchor = squares[-1]
        n = [diag_parts(x) for x in squares]
        updates = [_mm_pairs(_split2(x), yd) for x, (_, yd) in zip(p, n)]
        fill(anchor)
        p = [x + y for x, y in zip(p, updates)]
        anchor = p[-1]
    for level_mask in (jnp.where(same16, 0.0, jnp.where(same32, 1.0, 0.0)), jnp.where(same32, 0.0, 1.0)):
        off = [diag_parts(a * level_mask)[1] for a in a_list]
        ps = [diag_parts(x) for x in p]
        t = [_mm_pairs(x, y) for (x, _), y in zip(ps, off)]
        fill(anchor)
        anchor = t[-1]
        updates = [_mm_pairs(_split2(x), zd) for x, (_, zd) in zip(t, ps)]
        fill(anchor)
        p = [x - y for x, y in zip(p, updates)]
        anchor = p[-1]
    while pending:
        fill(anchor)
    return p


def _gdn_kernel(qkv_ref, z_ref, sm_ref, cw_ref, dtb_ref, alog_ref, ng_ref, tril_ref, triu_ref,
                o_ref, carry_ref, state_ref):
    cl = GDN_CHUNK
    rows = qkv_ref.shape[1]
    heads = range(GDN_HEADS)
    half_heads = GDN_HEADS // 2
    chunks = range(rows // cl)
    pairs = [(c, j) for c in chunks for j in range(half_heads)]
    chains = [(c, h) for c in chunks for h in heads]

    @pl.when(pl.program_id(1) == 0)
    def _():
        carry_ref[...] = jnp.zeros_like(carry_ref)
        state_ref[...] = jnp.zeros_like(state_ref)

    qkv = qkv_ref[0]
    ext = jnp.concatenate([carry_ref[...], qkv], axis=0)
    carry_ref[...] = qkv[rows - CONV_HALO:, :]

    def conv_silu(lo, width, tie=0.0):
        return _silu(_causal_conv(ext[:, lo:lo + width] + tie, cw_ref[:, lo:lo + width], CONV_HALO))

    sm = sm_ref[0]
    beta_all = jax.nn.sigmoid(sm)
    g_all = -jnp.exp(alog_ref[...]) * _softplus(sm + dtb_ref[...])
    row = lax.broadcasted_iota(jnp.int32, (cl, 2 * cl), 0)
    lane = lax.broadcasted_iota(jnp.int32, (cl, 2 * cl), 1)
    first = lane < cl
    col = jnp.where(first, lane, lane - cl)
    incl = row >= col

    def l2n(x):
        return x * lax.rsqrt(jnp.sum(x * x, axis=-1, keepdims=True) + NORM_EPS)

    def chunk(x, c):
        return x[c * cl:(c + 1) * cl]

    act_k = conv_silu(GDN_QK_W, GDN_QK_W)
    kn = [l2n(act_k[:, h * GDN_DK:(h + 1) * GDN_DK]) for h in heads]
    beta = [beta_all[:, h:h + 1] for h in heads]
    kb = [kn[h] * beta[h] for h in heads]
    knb = [x.astype(BF16) for x in kn]
    gcs_c = [_cumsum_cols(tril_ref[...], chunk(g_all, c)) for c in chunks]
    gcs_r = [_cumsum_rows_t(chunk(g_all, c), triu_ref[...]) for c in chunks]
    gc = {(c, h): gcs_c[c][:, GDN_HEADS + h:GDN_HEADS + h + 1] for c, h in chains}
    eg = {k: jnp.exp(g) for k, g in gc.items()}

    def decay(c, j):
        g_col = jnp.where(first, gc[c, 2 * j], gc[c, 2 * j + 1])
        g_row = jnp.concatenate([gcs_r[c][GDN_HEADS + 2 * j + i:GDN_HEADS + 2 * j + i + 1, :] for i in range(2)],
                                axis=1)
        return jnp.exp(jnp.where(incl, g_col - g_row, MASKED))

    dec = {k: decay(*k) for k in pairs}
    zeros_k = jnp.zeros((cl, GDN_DK), BF16)

    def against_keys(x, c, j):
        k0, k1 = chunk(knb[2 * j], c), chunk(knb[2 * j + 1], c)
        return (_dot_nt(chunk(x[2 * j], c).astype(BF16), jnp.concatenate([k0, zeros_k], axis=0))
                + _dot_nt(chunk(x[2 * j + 1], c).astype(BF16), jnp.concatenate([zeros_k, k1], axis=0)))

    a_kk = [jnp.where(row > col, against_keys(kb, c, j) * dec[c, j], 0.0) for c, j in pairs]

    qn, v, a_qk, rhs, q_decayed, k_decayed, z_gate = {}, {}, {}, {}, {}, {}, {}

    def fill_queries(j):
        def run(tie):
            act_q = conv_silu(2 * j * GDN_DK, 2 * GDN_DK, tie)
            for i in range(2):
                qn[2 * j + i] = l2n(act_q[:, i * GDN_DK:(i + 1) * GDN_DK]) * GDN_DK ** -0.5
        return run

    def fill_values(j):
        def run(tie):
            act_v = conv_silu(2 * GDN_QK_W + 2 * j * GDN_DV, 2 * GDN_DV, tie)
            for i in range(2):
                v[2 * j + i] = act_v[:, i * GDN_DV:(i + 1) * GDN_DV]
        return run

    def fill_intra(tie):
        for c, j in pairs:
            a_qk[c, j] = (against_keys(qn, c, j) * (dec[c, j] + tie)).astype(BF16)

    def fill_rhs(tie):
        for c, h in chains:
            rhs[c, h] = jnp.concatenate([chunk(v[h], c) * (chunk(beta[h], c) + tie),
                                         chunk(kb[h], c) * (eg[c, h] + tie)], axis=1).astype(BF16)

    def fill_recurrence_operands(tie):
        for c, h in chains:
            q_decayed[c, h] = chunk(qn[h], c) * (eg[c, h] + tie)
            k_decayed[c, h] = (chunk(kn[h], c) * jnp.exp(gc[c, h][cl - 1:cl, :] - gc[c, h] + tie)).astype(BF16)

    def fill_gate(tie):
        z_gate[0] = _silu(z_ref[0] + tie)

    fillers = ([fill_queries(j) for j in range(half_heads)] + [fill_values(j) for j in range(half_heads)]
               + [fill_intra, fill_rhs, fill_recurrence_operands, fill_gate])
    t_inv = [t.astype(BF16) for t in _unit_lower_inverses(a_kk, row, col, fillers)]

    def stacked(x0, x1):
        z = jnp.zeros_like(x0)
        return jnp.concatenate([jnp.concatenate([x0, z], axis=1), jnp.concatenate([z, x1], axis=1)], axis=0)

    uw = {(c, j): _dot(t, stacked(rhs[c, 2 * j], rhs[c, 2 * j + 1])) for (c, j), t in zip(pairs, t_inv)}
    width = GDN_DV + GDN_DK

    state = [state_ref[h] for h in heads]
    for c in chunks:
        sb = [s.astype(BF16) for s in state]
        u = [uw[c, h // 2][:, (h % 2) * width:(h % 2) * width + GDN_DV] for h in heads]
        w = [uw[c, h // 2][:, (h % 2) * width + GDN_DV:(h % 2 + 1) * width] for h in heads]
        ws = [_dot(jnp.concatenate([w[h], q_decayed[c, h]], axis=0).astype(BF16), sb[h]) for h in heads]
        vb = [(u[h] - ws[h][:cl]).astype(BF16) for h in heads]
        intra = [_dot(a_qk[c, j], stacked(vb[2 * j], vb[2 * j + 1])) for j in range(half_heads)]
        o = [ws[h][cl:] + intra[h // 2][:, (h % 2) * GDN_DV:(h % 2 + 1) * GDN_DV] for h in heads]
        state = [state[h] * jnp.exp(gc[c, h][cl - 1:cl, :]) + _dot_tn(k_decayed[c, h], vb[h]) for h in heads]
        for h in heads:
            out = _rms(o[h], ng_ref[...]) * z_gate[0][c * cl:(c + 1) * cl, h * GDN_DV:(h + 1) * GDN_DV]
            o_ref[0, c * cl:(c + 1) * cl, h * GDN_DV:(h + 1) * GDN_DV] = out.astype(o_ref.dtype)
    for h in heads:
        state_ref[h] = state[h]


def gdn_mixer(qkv, z, sm, conv_w, dt_bias, a_log, norm_g):
    b, s, _ = qkv.shape
    cl = GDN_CHUNK

    def full(a):
        return pl.BlockSpec(a.shape, lambda bi, c: (0,) * a.ndim)

    args = (conv_w, _lane_row(dt_bias, GDN_HEADS), _lane_row(a_log, GDN_HEADS), norm_g.reshape(1, GDN_DV),
            _tri(cl, False), _tri(cl, True))
    rows = GDN_STEP_CHUNKS * cl
    assert s % rows == 0
    return pl.pallas_call(
        _gdn_kernel,
        grid=(b, s // rows),
        in_specs=[pl.BlockSpec((1, rows, GDN_CONV_DIM), lambda bi, c: (bi, c, 0)),
                  pl.BlockSpec((1, rows, GDN_V_W), lambda bi, c: (bi, c, 0)),
                  pl.BlockSpec((1, rows, LANE), lambda bi, c: (bi, c, 0))] + [full(a) for a in args],
        out_specs=pl.BlockSpec((1, rows, GDN_V_W), lambda bi, c: (bi, c, 0)),
        out_shape=jax.ShapeDtypeStruct((b, s, GDN_V_W), BF16),
        scratch_shapes=[pltpu.VMEM((CONV_HALO, GDN_CONV_DIM), F32),
                        pltpu.VMEM((GDN_HEADS, GDN_DK, GDN_DV), F32)],
        compiler_params=_cparams(("parallel", "arbitrary")),
        name="gdn_mixer",
    )(qkv, z, sm, *args)


def _pad_cols(w, width=LANE):
    return jnp.pad(w, ((0, 0), (0, width - w.shape[1])))


HY_Q_W = NSA_HEADS * NSA_DH
HY_KV_W = 6 * NSA_GROUPS * NSA_DH
HY_GATE_W = NSA_HEADS * 3


def hybrid_heads(x2d, b, s, positions, g_pre, w_in, pe_k, pe_v, ck_w1, ck_b1, ck_w2,
                 cv_w1, cv_b1, cv_w2, conv_w, conv_b, dt_bias, a_log, d_skip, norm_g):
    o = 0
    cols = {}
    for name, width in (("q", HY_Q_W), ("kv", HY_KV_W), ("gate", HY_GATE_W), ("z", SSD_INNER),
                        ("xbc", SSD_CONV_DIM), ("dt", SSD_HEADS)):
        cols[name] = w_in[:, o:o + width]
        o += width
    gpg = HY_GATE_W // NSA_GROUPS
    w_cat = jnp.concatenate(
        [cols["q"], cols["kv"], cols["z"], cols["xbc"]]
        + [_pad_cols(cols["gate"][:, g * gpg:(g + 1) * gpg]) for g in range(NSA_GROUPS)]
        + [_pad_cols(cols["dt"])], axis=1).astype(BF16)
    splits = (HY_Q_W, HY_KV_W, SSD_INNER, SSD_CONV_DIM, NSA_GROUPS * LANE, LANE)
    q, kv, z, xbc, gates, dts = norm_matmul(x2d, g_pre, w_cat, splits)

    qh, kcmp, vcmp, kslc, vslc, kwin, vwin = nsa_prep(q.reshape(b, s, -1), kv.reshape(b, s, -1), positions)
    kc, vc = nsa_compress(kcmp, vcmp, pe_k, pe_v, ck_w1, ck_b1, ck_w2, cv_w1, cv_b1, cv_w2)
    o_nsa = nsa_attention(qh, kc, vc, kslc, vslc, kwin, vwin, gates.reshape(b, s, -1))
    y = ssd_mixer(z.reshape(b, s, -1), xbc.reshape(b, s, -1), dts.reshape(b, s, -1),
                  conv_w, conv_b, dt_bias, a_log, d_skip, norm_g)
    return [o_nsa.reshape(b * s, -1), y.reshape(b * s, -1)]


def gdn_heads(x2d, b, s, g_pre, w_in, conv_w, dt_bias, a_log, norm_g):
    w_cat = jnp.concatenate([w_in[:, :GDN_CONV_DIM + GDN_V_W], _pad_cols(w_in[:, GDN_CONV_DIM + GDN_V_W:])],
                            axis=1).astype(BF16)
    qkv, z, sm = norm_matmul(x2d, g_pre, w_cat, (GDN_CONV_DIM, GDN_V_W, LANE))
    o = gdn_mixer(qkv.reshape(b, s, -1), z.reshape(b, s, -1), sm.reshape(b, s, -1), conv_w, dt_bias, a_log, norm_g)
    return [o.reshape(b * s, -1)]


def kernel(x, positions, norm_mix_pre, norm_mix_post, norm_ffn_pre, norm_ffn_post, hy_w_in, hy_w_out, nsa_pe_k, nsa_pe_v, nsa_ck_w1, nsa_ck_b1, nsa_ck_w2, nsa_cv_w1, nsa_cv_b1, nsa_cv_w2, ssd_conv_w, ssd_conv_b, ssd_dt_bias, ssd_a_log, ssd_d, ssd_norm, gdn_w_in, gdn_conv_w, gdn_dt_bias, gdn_a_log, gdn_norm, gdn_w_out, ffn_w_up, ffn_conv_w, ffn_conv_b, ffn_w_down):
    b, s, d = x.shape
    x2d = x.reshape(b * s, d)
    depth = norm_mix_pre.shape[0]
    for layer in range(depth):
        e = layer // 2
        if layer % 2 == 0:
            parts = hybrid_heads(x2d, b, s, positions, norm_mix_pre[layer], hy_w_in[e], nsa_pe_k[e], nsa_pe_v[e],
                                 nsa_ck_w1[e], nsa_ck_b1[e], nsa_ck_w2[e], nsa_cv_w1[e], nsa_cv_b1[e], nsa_cv_w2[e],
                                 ssd_conv_w[e], ssd_conv_b[e], ssd_dt_bias[e], ssd_a_log[e], ssd_d[e], ssd_norm[e])
            w_out = hy_w_out[e]
        else:
            parts = gdn_heads(x2d, b, s, norm_mix_pre[layer], gdn_w_in[e], gdn_conv_w[e], gdn_dt_bias[e],
                              gdn_a_log[e], gdn_norm[e])
            w_out = gdn_w_out[e]
        x2d = mixer_out_ffn(x2d, parts, w_out.astype(BF16), norm_mix_post[layer], s, norm_ffn_pre[layer],
                            ffn_w_up[layer].astype(BF16), ffn_conv_w[layer], ffn_conv_b[layer],
                            ffn_w_down[layer].astype(BF16), norm_ffn_post[layer])
    return x2d.reshape(b, s, d)
```

```python
import functools
import math

import numpy as np
import jax
import jax.numpy as jnp
from jax import lax
from jax.experimental import pallas as pl
from jax.experimental.pallas import tpu as pltpu

F32 = jnp.float32
BF16 = jnp.bfloat16

D_MODEL = 1024
NORM_EPS = 1e-6
MASKED = -1e30

NSA_HEADS = 8
NSA_GROUPS = 2
NSA_REP = NSA_HEADS // NSA_GROUPS
NSA_DH = 64
CMP_BLOCK = 32
CMP_STRIDE = 16
SLC_BLOCK = 64
SLC_TOPK = 16
WINDOW = 512
ROPE_THETA = 500000.0
ROT_DIM = NSA_DH // 4
FORCED_SCORE = 1e9

SSD_HEADS = 8
SSD_P = 64
SSD_INNER = SSD_HEADS * SSD_P
SSD_GROUPS = 2
SSD_N = 128
SSD_CHUNK = 256
SSD_CONV_DIM = SSD_INNER + 2 * SSD_GROUPS * SSD_N

GDN_HEADS = 8
GDN_DK = 128
GDN_DV = 128
GDN_CHUNK = 64
GDN_QK_W = GDN_HEADS * GDN_DK
GDN_V_W = GDN_HEADS * GDN_DV
GDN_CONV_DIM = 2 * GDN_QK_W + GDN_V_W

FFN_DIM = 2816

LANE = 128
CONV_HALO = 8
FFN_HALO = 16
VMEM_LIMIT = 56 * 1024 * 1024
ROW_TILE = 512
PREP_TILE = 256


def _cparams(sem):
    return pltpu.CompilerParams(dimension_semantics=sem, vmem_limit_bytes=VMEM_LIMIT)


def _rms(x, g):
    return x * lax.rsqrt(jnp.mean(x * x, axis=-1, keepdims=True) + NORM_EPS) * g


def _silu(x):
    return x * jax.nn.sigmoid(x)


def _softplus(x):
    return jnp.maximum(x, 0.0) + jnp.log1p(jnp.exp(-jnp.abs(x)))


def _dot(a, b):
    return jnp.dot(a, b, preferred_element_type=F32)


def _dot_nt(a, b):
    return lax.dot_general(a, b, (((1,), (1,)), ((), ())), preferred_element_type=F32)


def _dot_tn(a, b):
    return lax.dot_general(a, b, (((0,), (0,)), ((), ())), preferred_element_type=F32)


def _split3(x):
    x1 = x.astype(BF16)
    r1 = x - x1.astype(F32)
    x2 = r1.astype(BF16)
    x3 = (r1 - x2.astype(F32)).astype(BF16)
    return x1, x2, x3


def _split2(x):
    x1 = x.astype(BF16)
    x2 = (x - x1.astype(F32)).astype(BF16)
    return x1, x2


def _cumsum_cols(tril, a):
    a1, a2, a3 = _split3(a)
    return _dot(tril, a1) + _dot(tril, a2) + _dot(tril, a3)


def _cumsum_rows_t(a, triu):
    a1, a2, a3 = _split3(a)
    return _dot_tn(a1, triu) + _dot_tn(a2, triu) + _dot_tn(a3, triu)


def _causal_conv(ext, cw, halo):
    width = cw.shape[0]
    y = ext[halo:] * cw[width - 1:width]
    for k in range(1, width):
        y = y + pltpu.roll(ext, k, 0)[halo:] * cw[width - 1 - k:width - k]
    return y


def _inproj_kernel(x_ref, g_ref, w_ref, *out_refs, splits):
    hn = _rms(x_ref[...], g_ref[...]).astype(BF16)
    y = _dot(hn, w_ref[...])
    off = 0
    for o_ref, n in zip(out_refs, splits):
        o_ref[...] = y[:, off:off + n].astype(o_ref.dtype)
        off += n


def norm_matmul(x2d, g, w, splits, tm=ROW_TILE):
    t = x2d.shape[0]
    n = w.shape[1]
    assert sum(splits) == n and t % tm == 0
    return pl.pallas_call(
        functools.partial(_inproj_kernel, splits=splits),
        grid=(t // tm,),
        in_specs=[pl.BlockSpec((tm, D_MODEL), lambda i: (i, 0)),
                  pl.BlockSpec((1, D_MODEL), lambda i: (0, 0)),
                  pl.BlockSpec((D_MODEL, n), lambda i: (0, 0))],
        out_specs=[pl.BlockSpec((tm, s), lambda i: (i, 0)) for s in splits],
        out_shape=[jax.ShapeDtypeStruct((t, s), F32) for s in splits],
        compiler_params=_cparams(("parallel",)),
        name="norm_matmul",
    )(x2d, g.reshape(1, D_MODEL), w)


FFN_CHUNK = 256
FFN_DOWN_CHUNKS = 2


def _ffn_kernel(*refs, n_parts, tm, seq):
    x_ref, halo_ref = refs[:2]
    part_refs = refs[2:2 + n_parts]
    part_halo_refs = refs[2 + n_parts:2 + 2 * n_parts]
    (wo_ref, gmix_ref, gpre_ref, wup_ref, cw_ref, cb_ref, wd_ref, gpost_ref, o_ref, hn_ref) = refs[2 + 2 * n_parts:]

    def after_mixer(x, parts):
        acc = None
        off = 0
        for p_ref in parts:
            k = p_ref.shape[-1]
            part = _dot(p_ref[...], wo_ref[off:off + k, :])
            acc = part if acc is None else acc + part
            off += k
        return x + _rms(acc, gmix_ref[...])

    o_ref[...] = after_mixer(x_ref[...], part_refs)
    keep = ((pl.program_id(0) * tm) % seq != 0).astype(F32)
    hn_ref[:FFN_HALO, :] = (_rms(after_mixer(halo_ref[...], part_halo_refs), gpre_ref[...]) * keep).astype(BF16)
    hn_ref[FFN_HALO:, :] = _rms(o_ref[...], gpre_ref[...]).astype(BF16)
    hn = hn_ref[...]

    def up(lo):
        return _dot(hn, wup_ref[:, lo:lo + FFN_CHUNK]), _dot(hn, wup_ref[:, FFN_DIM + lo:FFN_DIM + lo + FFN_CHUNK])

    def conv(u, lo):
        return _causal_conv(u, cw_ref[:, lo:lo + FFN_CHUNK], FFN_HALO) + cb_ref[:, lo:lo + FFN_CHUNK]

    acc = None
    pending = []
    n_chunks = FFN_DIM // FFN_CHUNK
    ahead = up(0)
    for c in range(n_chunks):
        lo = c * FFN_CHUNK
        u_gate, u_val = ahead
        if c + 1 < n_chunks:
            ahead = up(lo + FFN_CHUNK)
        pending.append((_silu(conv(u_gate, lo)) * conv(u_val, FFN_DIM + lo)).astype(BF16))
        if len(pending) == FFN_DOWN_CHUNKS or c == n_chunks - 1:
            width = len(pending) * FFN_CHUNK
            h = pending[0] if len(pending) == 1 else jnp.concatenate(pending, axis=1)
            part = _dot(h, wd_ref[lo + FFN_CHUNK - width:lo + FFN_CHUNK, :])
            acc = part if acc is None else acc + part
            pending = []
    o_ref[...] = o_ref[...] + _rms(acc, gpost_ref[...])


def mixer_out_ffn(x2d, parts, w_out, gmix, seq, gpre, w_up, conv_w, conv_b, w_down, gpost, tm=ROW_TILE):
    t = x2d.shape[0]
    assert FFN_DIM % FFN_CHUNK == 0 and t % tm == 0 and seq % tm == 0 and tm % FFN_HALO == 0
    hb = tm // FFN_HALO

    def resident(a):
        return pl.BlockSpec(a.shape, lambda i: (0,) * a.ndim, pipeline_mode=pl.Buffered(1))

    def halo(width):
        return pl.BlockSpec((FFN_HALO, width), lambda i: (jnp.maximum(i * hb - 1, 0), 0))

    consts = (w_out, gmix.reshape(1, D_MODEL), gpre.reshape(1, D_MODEL), w_up, conv_w,
              conv_b.reshape(1, 2 * FFN_DIM), w_down, gpost.reshape(1, D_MODEL))
    return pl.pallas_call(
        functools.partial(_ffn_kernel, n_parts=len(parts), tm=tm, seq=seq),
        grid=(t // tm,),
        in_specs=[pl.BlockSpec((tm, D_MODEL), lambda i: (i, 0)), halo(D_MODEL)]
        + [pl.BlockSpec((tm, p.shape[1]), lambda i: (i, 0)) for p in parts]
        + [halo(p.shape[1]) for p in parts]
        + [resident(a) for a in consts],
        out_specs=pl.BlockSpec((tm, D_MODEL), lambda i: (i, 0)),
        out_shape=jax.ShapeDtypeStruct((t, D_MODEL), F32),
        scratch_shapes=[pltpu.VMEM((tm + FFN_HALO, D_MODEL), BF16)],
        compiler_params=_cparams(("parallel",)),
        name="mixer_out_ffn",
    )(x2d, x2d, *parts, *parts, *consts)


def _nsa_prep_kernel(q_ref, kv_ref, pos_ref, freq_ref, ecos_ref, esin_ref, rest_ref,
                     qh_ref, kcmp_ref, vcmp_ref, kslc_ref, vslc_ref, kwin_ref, vwin_ref):
    ang = freq_ref[...] * pos_ref[0].astype(F32)

    def spread(t, e_ref):
        t1, t2, t3 = _split3(t)
        e = e_ref[...]
        return _dot_tn(t1, e) + _dot_tn(t2, e) + _dot_tn(t3, e)

    cs = spread(jnp.cos(ang), ecos_ref) + rest_ref[...]
    sn = spread(jnp.sin(ang), esin_ref)
    lane = lax.broadcasted_iota(jnp.int32, (1, LANE), 1) % NSA_DH
    first_half = lane < ROT_DIM // 2

    def rope(x):
        partner = jnp.where(first_half, pltpu.roll(x, LANE - ROT_DIM // 2, 1), pltpu.roll(x, ROT_DIM // 2, 1))
        return x * cs + partner * sn

    ts = q_ref.shape[1]
    aug_w = kslc_ref.shape[-1]
    low_half = lax.broadcasted_iota(jnp.int32, (1, LANE), 1) < NSA_DH

    def widen(x, upper, fill):
        low = jnp.where(low_half, pltpu.roll(x, NSA_DH, 1) if upper else x, fill[:, :LANE])
        return low if aug_w == LANE else jnp.concatenate([low, fill[:, LANE:]], axis=1)

    scale = NSA_DH ** -0.5 * math.log2(math.e)
    for j in range(NSA_HEADS // 2):
        t = (rope(q_ref[0, :, j * LANE:(j + 1) * LANE]) * scale).T
        qh_ref[0, 2 * j] = t[:NSA_DH].astype(BF16)
        qh_ref[0, 2 * j + 1] = t[NSA_DH:].astype(BF16)

    tok = pl.program_id(1) * ts + lax.broadcasted_iota(jnp.int32, (ts, 1), 0)
    block_onehot = jnp.where(lax.broadcasted_iota(jnp.int32, (1, aug_w), 1) - NSA_DH == tok // SLC_BLOCK, 1.0, 0.0)
    t = rope(kv_ref[0, :, 2 * LANE:3 * LANE])
    for g in range(NSA_GROUPS):
        kslc_ref[0, g] = widen(t, g == 1, block_onehot).astype(BF16)

    for i, o_ref in ((0, kcmp_ref), (4, kwin_ref), (1, vcmp_ref)):
        t = kv_ref[0, :, i * LANE:(i + 1) * LANE]
        if i % 2 == 0:
            t = rope(t)
        for g in range(NSA_GROUPS):
            o_ref[0, g] = t[:, g * NSA_DH:(g + 1) * NSA_DH].astype(o_ref.dtype)
    ones_row = jnp.where(lax.broadcasted_iota(jnp.int32, (NSA_VROWS - NSA_DH, ts), 0) == 0, 1.0, 0.0)
    for i, o_ref in ((3, vslc_ref), (5, vwin_ref)):
        t = kv_ref[0, :, i * LANE:(i + 1) * LANE].T
        for g in range(NSA_GROUPS):
            o_ref[0, g] = jnp.concatenate([t[g * NSA_DH:(g + 1) * NSA_DH, :], ones_row], axis=0).astype(o_ref.dtype)


NSA_VROWS = NSA_DH + 16


def _aug_width(seq):
    return -(-(NSA_DH + seq // SLC_BLOCK) // LANE) * LANE


def nsa_prep(q, kv, positions, ts=PREP_TILE):
    b, s, _ = q.shape
    half = ROT_DIM // 2
    freq_rows = 16
    inv_freq = ROPE_THETA ** (-jnp.arange(0, ROT_DIM, 2, dtype=F32) / ROT_DIM)
    freq = jnp.zeros((freq_rows, 1), F32).at[:half, 0].set(inv_freq)
    d = np.arange(LANE) % NSA_DH
    j = np.arange(freq_rows)[:, None]
    e_cos = ((d[None, :] < ROT_DIM) & (d[None, :] % half == j)).astype(np.float32)
    e_sin = e_cos * np.where(d[None, :] < half, -1.0, 1.0)
    rest = (d >= ROT_DIM).astype(np.float32).reshape(1, LANE)
    tables = (jnp.asarray(e_cos, dtype=BF16), jnp.asarray(e_sin, dtype=BF16), jnp.asarray(rest))
    tok_shape = (b, NSA_GROUPS, s, NSA_DH)
    tok_spec = pl.BlockSpec((1, NSA_GROUPS, ts, NSA_DH), lambda bi, i: (bi, 0, i, 0))
    feat_shape = (b, NSA_GROUPS, NSA_VROWS, s)
    feat_spec = pl.BlockSpec((1, NSA_GROUPS, NSA_VROWS, ts), lambda bi, i: (bi, 0, 0, i))
    aug_w = _aug_width(s)
    return pl.pallas_call(
        _nsa_prep_kernel,
        grid=(b, s // ts),
        in_specs=[pl.BlockSpec((1, ts, q.shape[2]), lambda bi, i: (bi, i, 0)),
                  pl.BlockSpec((1, ts, kv.shape[2]), lambda bi, i: (bi, i, 0)),
                  pl.BlockSpec((1, 1, ts), lambda bi, i: (bi, 0, i)),
                  pl.BlockSpec((freq_rows, 1), lambda bi, i: (0, 0))]
        + [pl.BlockSpec(t.shape, lambda bi, i: (0, 0)) for t in tables],
        out_specs=[pl.BlockSpec((1, NSA_HEADS, NSA_DH, ts), lambda bi, i: (bi, 0, 0, i)),
                   tok_spec, tok_spec,
                   pl.BlockSpec((1, NSA_GROUPS, ts, aug_w), lambda bi, i: (bi, 0, i, 0)),
                   feat_spec, tok_spec, feat_spec],
        out_shape=[jax.ShapeDtypeStruct((b, NSA_HEADS, NSA_DH, s), BF16),
                   jax.ShapeDtypeStruct(tok_shape, F32), jax.ShapeDtypeStruct(tok_shape, F32),
                   jax.ShapeDtypeStruct((b, NSA_GROUPS, s, aug_w), BF16), jax.ShapeDtypeStruct(feat_shape, BF16),
                   jax.ShapeDtypeStruct(tok_shape, BF16), jax.ShapeDtypeStruct(feat_shape, BF16)],
        compiler_params=_cparams(("parallel", "parallel")),
        name="nsa_prep",
    )(q, kv, positions.reshape(b, 1, s), freq, *tables)


def _compress_kernel(k_ref, v_ref, pek_ref, pev_ref, kw1_ref, vw1_ref, kb1_ref, vb1_ref, kw2_ref, vw2_ref,
                     kc_ref, vc_ref):
    half = CMP_STRIDE * NSA_DH

    def mlp(t_ref, pe_ref, w1_ref, b1_ref, w2_ref):
        a = t_ref[0, 0]
        n = a.shape[0]
        h_lo = _dot((a + pe_ref[:, :half]).astype(BF16), w1_ref[:half, :])
        h_hi = _dot((a + pe_ref[:, half:]).astype(BF16), w1_ref[half:, :])
        hid = _silu(h_lo + pltpu.roll(h_hi, n - 1, 0) + b1_ref[...])
        return _dot(hid.astype(BF16), w2_ref[...])

    kc_ref[0, 0] = mlp(k_ref, pek_ref, kw1_ref, kb1_ref, kw2_ref).astype(kc_ref.dtype)
    vc_ref[0, 0] = mlp(v_ref, pev_ref, vw1_ref, vb1_ref, vw2_ref).astype(vc_ref.dtype).T


def nsa_compress(kcmp, vcmp, pe_k, pe_v, ck_w1, ck_b1, ck_w2, cv_w1, cv_b1, cv_w2):
    b, g, s, dh = kcmp.shape
    n = s // CMP_STRIDE
    wide = CMP_STRIDE * dh
    assert CMP_BLOCK == 2 * CMP_STRIDE
    kr = kcmp.reshape(b, g, n, wide)
    vr = vcmp.reshape(b, g, n, wide)
    blk = pl.BlockSpec((1, 1, n, wide), lambda bi, gi: (bi, gi, 0, 0))

    def full(a):
        return pl.BlockSpec(a.shape, lambda bi, gi: (0,) * a.ndim)

    args = (pe_k.reshape(1, 2 * wide), pe_v.reshape(1, 2 * wide), ck_w1.astype(BF16), cv_w1.astype(BF16),
            ck_b1.reshape(1, dh), cv_b1.reshape(1, dh), ck_w2.astype(BF16), cv_w2.astype(BF16))
    return pl.pallas_call(
        _compress_kernel,
        grid=(b, g),
        in_specs=[blk, blk] + [full(a) for a in args],
        out_specs=[pl.BlockSpec((1, 1, n, dh), lambda bi, gi: (bi, gi, 0, 0)),
                   pl.BlockSpec((1, 1, dh, n), lambda bi, gi: (bi, gi, 0, 0))],
        out_shape=[jax.ShapeDtypeStruct((b, g, n, dh), BF16), jax.ShapeDtypeStruct((b, g, dh, n), BF16)],
        compiler_params=_cparams(("parallel", "parallel")),
        name="nsa_compress",
    )(kr, vr, *args)


ATT_TQ = 256
ATT_TK = 256
ATT_RING = 4


def _att_kernel(q_ref, kc_ref, vc_ref, ks_ref, vs_ref, kw_ref, vw_ref, gate_ref, ov_ref, o_ref,
                *s_refs, seq):
    tq, tk, rep, dh = ATT_TQ, ATT_TK, NSA_REP, NSA_DH
    cols = rep * tq
    nblk = seq // SLC_BLOCK
    ncmp = kc_ref.shape[2]
    aug_w = ks_ref.shape[-1]
    t0 = pl.program_id(2) * tq
    q = jnp.concatenate([q_ref[0, r] for r in range(rep)], axis=1)
    tpos = t0 + lax.broadcasted_iota(jnp.int32, (1, tq), 1)

    def per_head(x):
        return jnp.concatenate([x] * rep, axis=1)


    band = WINDOW + tq
    start = pl.multiple_of(jnp.maximum(t0 - WINDOW, 0), tq)
    dlt = tpos - (start + lax.broadcasted_iota(jnp.int32, (band, 1), 0))
    bias_w = jnp.where((dlt >= 0) & (dlt < WINDOW), 0.0, MASKED)
    s_w = _dot(kw_ref[0, 0, pl.ds(start, band), :], q) + per_head(bias_w)

    cmp_end = lax.broadcasted_iota(jnp.int32, (ncmp, 1), 0) * CMP_STRIDE + (CMP_BLOCK - 1)
    s_c = _dot(kc_ref[0, 0], q) + per_head(jnp.where(cmp_end <= tpos, 0.0, MASKED))
    p_c = jnp.exp2(s_c - jnp.max(s_c, axis=0, keepdims=True))
    l_c = jnp.sum(p_c, axis=0, keepdims=True)
    p_c = p_c * (per_head(jnp.where(tpos >= CMP_BLOCK - 1, 1.0, 0.0)) / l_c)
    o_c = _dot(vc_ref[0, 0], p_c.astype(BF16))

    p_sum = p_c[:, :tq]
    for r in range(1, rep):
        p_sum = p_sum + p_c[:, r * tq:(r + 1) * tq]
    p_hi, p_lo = _split2(p_sum)
    imp = _dot(ov_ref[...], p_hi) + _dot(ov_ref[...], p_lo)
    blk = lax.broadcasted_iota(jnp.int32, (nblk, 1), 0)
    cur = tpos // SLC_BLOCK
    forced = (blk == 0) | (blk == cur) | (blk == cur - 1)
    imp = jnp.where(forced, FORCED_SCORE, jnp.where(blk <= cur, imp, -1.0))
    sel = jnp.full((nblk, tq), MASKED, F32)
    for _ in range(min(SLC_TOPK, nblk)):
        best = jnp.max(imp, axis=0, keepdims=True)
        first = jnp.min(jnp.where(imp == best, blk, nblk), axis=0, keepdims=True)
        hit = blk == first
        sel = jnp.where(hit, 0.0, sel)
        imp = jnp.where(hit, -jnp.inf, imp)

    mask_rows = jnp.concatenate([sel.astype(BF16), jnp.zeros((aug_w - dh - nblk, tq), BF16)], axis=0)
    q_aug = jnp.concatenate([q, per_head(mask_rows)], axis=0)
    kidx = lax.broadcasted_iota(jnp.int32, (tk, 1), 0)

    def produce(kt, s_ref):
        k0 = pl.multiple_of(kt * tk, tk)
        s = _dot(ks_ref[0, 0, pl.ds(k0, tk), :], q_aug)
        s_ref[...] = s
        return jnp.max(s, axis=0, keepdims=True)

    def consume(kt, s_ref, tile_max, carry, causal):
        m, acc = carry
        k0 = pl.multiple_of(kt * tk, tk)
        v = vs_ref[0, 0, :, pl.ds(k0, tk)]
        s = s_ref[...]
        if causal:
            s = s + per_head(jnp.where(k0 + kidx <= tpos, 0.0, MASKED))
            tile_max = jnp.max(s, axis=0, keepdims=True)
        m_new = jnp.maximum(m, tile_max)
        alpha = jnp.exp2(m - m_new)
        p = jnp.exp2(s - m_new)
        return m_new, alpha * acc + _dot(v, p.astype(BF16))

    nbuf = len(s_refs)

    def slc_trip(j, carry):
        maxes, state = list(carry[:nbuf]), carry[nbuf:]
        for i, s_ref in enumerate(s_refs):
            state = consume(nbuf * j + i, s_ref, maxes[i], state, False)
            maxes[i] = produce(nbuf * (j + 1) + i, s_ref)
        return tuple(maxes) + state

    n_full = t0 // (nbuf * tk)
    init = tuple(produce(i, s_ref) for i, s_ref in enumerate(s_refs)) + (
        jnp.full((1, cols), MASKED, F32), jnp.zeros((NSA_VROWS, cols), F32))

    p_w = jnp.exp2(s_w - jnp.max(s_w, axis=0, keepdims=True))
    o_w = _dot(vw_ref[0, 0, :, pl.ds(start, band)], p_w.astype(BF16))
    o_w = o_w[:dh] / o_w[dh:dh + 1]

    carry = lax.fori_loop(0, n_full, slc_trip, init)
    maxes, state = carry[:nbuf], carry[nbuf:]
    own = (t0 - n_full * nbuf * tk) // tk

    def last_trip(n_before):
        def run(st):
            for i in range(n_before):
                st = consume(nbuf * n_full + i, s_refs[i], maxes[i], st, False)
            return consume(nbuf * n_full + n_before, s_refs[n_before], None, st, True)
        return run

    finish = last_trip(nbuf - 1)
    for n_before in reversed(range(nbuf - 1)):
        finish = functools.partial(lax.cond, own == n_before, last_trip(n_before), finish)
    acc_s = finish(state)[1]
    o_s = acc_s[:dh] / acc_s[dh:dh + 1]

    gates = jax.nn.sigmoid(gate_ref[0]).T
    merged = []
    for r in range(rep):
        sl = slice(r * tq, (r + 1) * tq)
        merged.append(gates[3 * r:3 * r + 1] * o_c[:, sl] + gates[3 * r + 1:3 * r + 2] * o_s[:, sl]
                      + gates[3 * r + 2:3 * r + 3] * o_w[:, sl])
    o_ref[0] = jnp.concatenate(merged, axis=0).T.astype(o_ref.dtype)


def _overlap_matrix(seq):
    ncp = seq // CMP_STRIDE
    n_cmp = (seq - CMP_BLOCK) // CMP_STRIDE + 1
    nblk = seq // SLC_BLOCK
    cs = np.arange(ncp) * CMP_STRIDE
    ss = np.arange(nblk) * SLC_BLOCK
    ov = ((cs[None, :] <= ss[:, None] + SLC_BLOCK - 1) & (cs[None, :] + CMP_BLOCK - 1 >= ss[:, None])
          & (np.arange(ncp)[None, :] < n_cmp))
    return jnp.asarray(ov.astype(np.float32), dtype=BF16)


def nsa_attention(qh, kc, vc, kslc, vslc, kwin, vwin, gates):
    b, _, dh, s = qh.shape
    aug_w = kslc.shape[-1]
    g = NSA_GROUPS
    ncp = kc.shape[2]
    nblk = s // SLC_BLOCK
    assert s % (ATT_RING * ATT_TK) == 0 and s >= WINDOW + ATT_TQ and aug_w == _aug_width(s)
    ov = _overlap_matrix(s)
    tok_spec = pl.BlockSpec((1, 1, s, dh), lambda bi, gi, i: (bi, gi, 0, 0))
    feat_spec = pl.BlockSpec((1, 1, NSA_VROWS, s), lambda bi, gi, i: (bi, gi, 0, 0))
    return pl.pallas_call(
        functools.partial(_att_kernel, seq=s),
        grid=(b, g, s // ATT_TQ),
        in_specs=[pl.BlockSpec((1, NSA_REP, dh, ATT_TQ), lambda bi, gi, i: (bi, gi, 0, i)),
                  pl.BlockSpec((1, 1, ncp, dh), lambda bi, gi, i: (bi, gi, 0, 0)),
                  pl.BlockSpec((1, 1, dh, ncp), lambda bi, gi, i: (bi, gi, 0, 0)),
                  pl.BlockSpec((1, 1, s, aug_w), lambda bi, gi, i: (bi, gi, 0, 0)),
                  feat_spec, tok_spec, feat_spec,
                  pl.BlockSpec((1, ATT_TQ, LANE), lambda bi, gi, i: (bi, i, gi)),
                  pl.BlockSpec((nblk, ncp), lambda bi, gi, i: (0, 0))],
        out_specs=pl.BlockSpec((1, ATT_TQ, NSA_REP * dh), lambda bi, gi, i: (bi, i, gi)),
        out_shape=jax.ShapeDtypeStruct((b, s, NSA_HEADS * dh), BF16),
        scratch_shapes=[pltpu.VMEM((ATT_TK, NSA_REP * ATT_TQ), F32)] * ATT_RING,
        compiler_params=_cparams(("parallel", "parallel", "arbitrary")),
        name="nsa_attention",
    )(qh, kc, vc, kslc, vslc, kwin, vwin, gates, ov)


def _ssd_kernel(z_ref, xbc_ref, dt_ref, cw_ref, cb_ref, dtb_ref, alog_ref, drow_ref, ng_ref, tril_ref, triu_ref,
                o_ref, carry_ref, state_ref, y_ref):
    cl = SSD_CHUNK
    hpg = SSD_HEADS // SSD_GROUPS

    @pl.when(pl.program_id(1) == 0)
    def _():
        carry_ref[...] = jnp.zeros_like(carry_ref)
        state_ref[...] = jnp.zeros_like(state_ref)

    xbc = xbc_ref[0]
    ext = jnp.concatenate([carry_ref[...], xbc], axis=0)
    act = _silu(_causal_conv(ext, cw_ref[...], CONV_HALO) + cb_ref[...])
    carry_ref[...] = xbc[cl - CONV_HALO:, :]

    xs = act[:, :SSD_INNER]
    dt = _softplus(dt_ref[0] + dtb_ref[...])
    a = dt * (-jnp.exp(alog_ref[...]))
    acs_c = _cumsum_cols(tril_ref[...], a)
    acs_r = _cumsum_rows_t(a, triu_ref[...])
    causal = lax.broadcasted_iota(jnp.int32, (cl, cl), 0) >= lax.broadcasted_iota(jnp.int32, (cl, cl), 1)

    for g in range(SSD_GROUPS):
        bg = act[:, SSD_INNER + g * SSD_N:SSD_INNER + (g + 1) * SSD_N].astype(BF16)
        cg = act[:, SSD_INNER + (SSD_GROUPS + g) * SSD_N:SSD_INNER + (SSD_GROUPS + g + 1) * SSD_N].astype(BF16)
        cb = _dot_nt(cg, bg)
        state = state_ref[g]
        y_off = _dot(cg, state.astype(BF16))
        weighted = []
        decay = []
        for hl in range(hpg):
            h = g * hpg + hl
            col = acs_c[:, h:h + 1]
            row = acs_r[h:h + 1, :]
            lmat = jnp.exp(jnp.where(causal, col - row, MASKED))
            xh = xs[:, h * SSD_P:(h + 1) * SSD_P]
            xdt = xh * dt[:, h:h + 1]
            y_diag = _dot((cb * lmat).astype(BF16), xdt.astype(BF16))
            y_ref[:, h * SSD_P:(h + 1) * SSD_P] = (y_diag + y_off[:, hl * SSD_P:(hl + 1) * SSD_P] * jnp.exp(col)
                                                   + drow_ref[:, h * SSD_P:(h + 1) * SSD_P] * xh)
            a_last = acs_c[cl - 1:cl, h:h + 1]
            weighted.append((xdt * jnp.exp(a_last - col)).astype(BF16))
            decay.append(jnp.broadcast_to(jnp.exp(a_last), (1, SSD_P)))
        contrib = _dot_tn(bg, jnp.concatenate(weighted, axis=1))
        state_ref[g] = state * jnp.concatenate(decay, axis=1) + contrib

    y = y_ref[...] * _silu(z_ref[0])
    gw = SSD_INNER // SSD_GROUPS
    for g in range(SSD_GROUPS):
        o_ref[0, :, g * gw:(g + 1) * gw] = _rms(y[:, g * gw:(g + 1) * gw],
                                                 ng_ref[:, g * gw:(g + 1) * gw]).astype(o_ref.dtype)


def _tri(n, upper):
    m = np.triu(np.ones((n, n), np.float32)) if upper else np.tril(np.ones((n, n), np.float32))
    return jnp.asarray(m, dtype=BF16)


def _lane_row(v, offset=0):
    row = jnp.zeros((1, LANE), F32)
    return row.at[0, offset:offset + v.shape[0]].set(v)


def ssd_mixer(z, xbc, dts, conv_w, conv_b, dt_bias, a_log, d_skip, norm_g):
    b, s, _ = z.shape
    cl = SSD_CHUNK
    hpg = SSD_HEADS // SSD_GROUPS

    def full(a):
        return pl.BlockSpec(a.shape, lambda bi, c: (0,) * a.ndim)

    args = (conv_w, conv_b.reshape(1, -1), _lane_row(dt_bias), _lane_row(a_log),
            jnp.repeat(d_skip, SSD_P).reshape(1, SSD_INNER), norm_g.reshape(1, SSD_INNER),
            _tri(cl, False), _tri(cl, True))
    return pl.pallas_call(
        _ssd_kernel,
        grid=(b, s // cl),
        in_specs=[pl.BlockSpec((1, cl, SSD_INNER), lambda bi, c: (bi, c, 0)),
                  pl.BlockSpec((1, cl, SSD_CONV_DIM), lambda bi, c: (bi, c, 0)),
                  pl.BlockSpec((1, cl, LANE), lambda bi, c: (bi, c, 0))] + [full(a) for a in args],
        out_specs=pl.BlockSpec((1, cl, SSD_INNER), lambda bi, c: (bi, c, 0)),
        out_shape=jax.ShapeDtypeStruct((b, s, SSD_INNER), BF16),
        scratch_shapes=[pltpu.VMEM((CONV_HALO, SSD_CONV_DIM), F32),
                        pltpu.VMEM((SSD_GROUPS, SSD_N, hpg * SSD_P), F32),
                        pltpu.VMEM((cl, SSD_INNER), F32)],
        compiler_params=_cparams(("parallel", "arbitrary")),
        name="ssd_mixer",
    )(z, xbc, dts, *args)


GDN_STEP_CHUNKS = 4


def _pair_diag(x):
    cl = x.shape[0]
    left = jnp.where(lax.broadcasted_iota(jnp.int32, x.shape, 1) < cl, 1.0, 0.0).astype(BF16)
    return jnp.concatenate([x * left, x * (1.0 - left).astype(BF16)], axis=0)


def _mm_pairs(a, b_diag):
    return _dot(a[0], b_diag[0]) + _dot(a[0], b_diag[1]) + _dot(a[1], b_diag[0])


def _unit_lower_inverses(a_list, row, col, fillers=()):
    eye = jnp.where(row == col, 1.0, 0.0)
    same16 = (row // 16) == (col // 16)
    same32 = (row // 32) == (col // 32)
    pending = list(fillers)

    def fill(anchor):
        if pending:
            pending.pop(0)(anchor[:1, :1] * 0.0)

    def diag_parts(x):
        hi, lo = _split2(x)
        return (hi, lo), (_pair_diag(hi), _pair_diag(lo))

    n1 = [jnp.where(same16, -a, 0.0) for a in a_list]
    p = [eye + x for x in n1]
    n = [diag_parts(x) for x in n1]
    anchor = p[-1]
    for _ in range(3):
        squares = [_mm_pairs(x, xd) for x, xd in n]
        fill(anchor)
        anchor = squares[-1]
        n = [diag_parts(x) for x in squares]
        updates = [_mm_pairs(_split2(x), yd) for x, (_, yd) in zip(p, n)]
        fill(anchor)
        p = [x + y for x, y in zip(p, updates)]
        anchor = p[-1]
    for level_mask in (jnp.where(same16, 0.0, jnp.where(same32, 1.0, 0.0)), jnp.where(same32, 0.0, 1.0)):
        off = [diag_parts(a * level_mask)[1] for a in a_list]
        ps = [diag_parts(x) for x in p]
        t = [_mm_pairs(x, y) for (x, _), y in zip(ps, off)]
        fill(anchor)
        anchor = t[-1]
        updates = [_mm_pairs(_split2(x), zd) for x, (_, zd) in zip(t, ps)]
        fill(anchor)
        p = [x - y for x, y in zip(p, updates)]
        anchor = p[-1]
    while pending:
        fill(anchor)
    return p


def _gdn_kernel(qkv_ref, z_ref, sm_ref, cw_ref, dtb_ref, alog_ref, ng_ref, tril_ref, triu_ref,
                o_ref, carry_ref, state_ref):
    cl = GDN_CHUNK
    rows = qkv_ref.shape[1]
    heads = range(GDN_HEADS)
    half_heads = GDN_HEADS // 2
    chunks = range(rows // cl)
    pairs = [(c, j) for c in chunks for j in range(half_heads)]
    chains = [(c, h) for c in chunks for h in heads]

    @pl.when(pl.program_id(1) == 0)
    def _():
        carry_ref[...] = jnp.zeros_like(carry_ref)
        state_ref[...] = jnp.zeros_like(state_ref)

    qkv = qkv_ref[0]
    ext = jnp.concatenate([carry_ref[...], qkv], axis=0)
    carry_ref[...] = qkv[rows - CONV_HALO:, :]

    def conv_silu(lo, width, tie=0.0):
        return _silu(_causal_conv(ext[:, lo:lo + width] + tie, cw_ref[:, lo:lo + width], CONV_HALO))

    sm = sm_ref[0]
    beta_all = jax.nn.sigmoid(sm)
    g_all = -jnp.exp(alog_ref[...]) * _softplus(sm + dtb_ref[...])
    row = lax.broadcasted_iota(jnp.int32, (cl, 2 * cl), 0)
    lane = lax.broadcasted_iota(jnp.int32, (cl, 2 * cl), 1)
    first = lane < cl
    col = jnp.where(first, lane, lane - cl)
    incl = row >= col

    def l2n(x):
        return x * lax.rsqrt(jnp.sum(x * x, axis=-1, keepdims=True) + NORM_EPS)

    def chunk(x, c):
        return x[c * cl:(c + 1) * cl]

    act_k = conv_silu(GDN_QK_W, GDN_QK_W)
    kn = [l2n(act_k[:, h * GDN_DK:(h + 1) * GDN_DK]) for h in heads]
    beta = [beta_all[:, h:h + 1] for h in heads]
    kb = [kn[h] * beta[h] for h in heads]
    knb = [x.astype(BF16) for x in kn]
    gcs_c = [_cumsum_cols(tril_ref[...], chunk(g_all, c)) for c in chunks]
    gcs_r = [_cumsum_rows_t(chunk(g_all, c), triu_ref[...]) for c in chunks]
    gc = {(c, h): gcs_c[c][:, GDN_HEADS + h:GDN_HEADS + h + 1] for c, h in chains}
    eg = {k: jnp.exp(g) for k, g in gc.items()}

    def decay(c, j):
        g_col = jnp.where(first, gc[c, 2 * j], gc[c, 2 * j + 1])
        g_row = jnp.concatenate([gcs_r[c][GDN_HEADS + 2 * j + i:GDN_HEADS + 2 * j + i + 1, :] for i in range(2)],
                                axis=1)
        return jnp.exp(jnp.where(incl, g_col - g_row, MASKED))

    dec = {k: decay(*k) for k in pairs}
    zeros_k = jnp.zeros((cl, GDN_DK), BF16)

    def against_keys(x, c, j):
        k0, k1 = chunk(knb[2 * j], c), chunk(knb[2 * j + 1], c)
        return (_dot_nt(chunk(x[2 * j], c).astype(BF16), jnp.concatenate([k0, zeros_k], axis=0))
                + _dot_nt(chunk(x[2 * j + 1], c).astype(BF16), jnp.concatenate([zeros_k, k1], axis=0)))

    a_kk = [jnp.where(row > col, against_keys(kb, c, j) * dec[c, j], 0.0) for c, j in pairs]

    qn, v, a_qk, rhs, q_decayed, k_decayed, z_gate = {}, {}, {}, {}, {}, {}, {}

    def fill_queries(j):
        def run(tie):
            act_q = conv_silu(2 * j * GDN_DK, 2 * GDN_DK, tie)
            for i in range(2):
                qn[2 * j + i] = l2n(act_q[:, i * GDN_DK:(i + 1) * GDN_DK]) * GDN_DK ** -0.5
        return run

    def fill_values(j):
        def run(tie):
            act_v = conv_silu(2 * GDN_QK_W + 2 * j * GDN_DV, 2 * GDN_DV, tie)
            for i in range(2):
                v[2 * j + i] = act_v[:, i * GDN_DV:(i + 1) * GDN_DV]
        return run

    def fill_intra(tie):
        for c, j in pairs:
            a_qk[c, j] = (against_keys(qn, c, j) * (dec[c, j] + tie)).astype(BF16)

    def fill_rhs(tie):
        for c, h in chains:
            rhs[c, h] = jnp.concatenate([chunk(v[h], c) * (chunk(beta[h], c) + tie),
                                         chunk(kb[h], c) * (eg[c, h] + tie)], axis=1).astype(BF16)

    def fill_recurrence_operands(tie):
        for c, h in chains:
            q_decayed[c, h] = chunk(qn[h], c) * (eg[c, h] + tie)
            k_decayed[c, h] = (chunk(kn[h], c) * jnp.exp(gc[c, h][cl - 1:cl, :] - gc[c, h] + tie)).astype(BF16)

    def fill_gate(tie):
        z_gate[0] = _silu(z_ref[0] + tie)

    fillers = ([fill_queries(j) for j in range(half_heads)] + [fill_values(j) for j in range(half_heads)]
               + [fill_intra, fill_rhs, fill_recurrence_operands, fill_gate])
    t_inv = [t.astype(BF16) for t in _unit_lower_inverses(a_kk, row, col, fillers)]

    def stacked(x0, x1):
        z = jnp.zeros_like(x0)
        return jnp.concatenate([jnp.concatenate([x0, z], axis=1), jnp.concatenate([z, x1], axis=1)], axis=0)

    uw = {(c, j): _dot(t, stacked(rhs[c, 2 * j], rhs[c, 2 * j + 1])) for (c, j), t in zip(pairs, t_inv)}
    width = GDN_DV + GDN_DK

    state = [state_ref[h] for h in heads]
    for c in chunks:
        sb = [s.astype(BF16) for s in state]
        u = [uw[c, h // 2][:, (h % 2) * width:(h % 2) * width + GDN_DV] for h in heads]
        w = [uw[c, h // 2][:, (h % 2) * width + GDN_DV:(h % 2 + 1) * width] for h in heads]
        ws = [_dot(jnp.concatenate([w[h], q_decayed[c, h]], axis=0).astype(BF16), sb[h]) for h in heads]
        vb = [(u[h] - ws[h][:cl]).astype(BF16) for h in heads]
        intra = [_dot(a_qk[c, j], stacked(vb[2 * j], vb[2 * j + 1])) for j in range(half_heads)]
        o = [ws[h][cl:] + intra[h // 2][:, (h % 2) * GDN_DV:(h % 2 + 1) * GDN_DV] for h in heads]
        state = [state[h] * jnp.exp(gc[c, h][cl - 1:cl, :]) + _dot_tn(k_decayed[c, h], vb[h]) for h in heads]
        for h in heads:
            out = _rms(o[h], ng_ref[...]) * z_gate[0][c * cl:(c + 1) * cl, h * GDN_DV:(h + 1) * GDN_DV]
            o_ref[0, c * cl:(c + 1) * cl, h * GDN_DV:(h + 1) * GDN_DV] = out.astype(o_ref.dtype)
    for h in heads:
        state_ref[h] = state[h]


def gdn_mixer(qkv, z, sm, conv_w, dt_bias, a_log, norm_g):
    b, s, _ = qkv.shape
    cl = GDN_CHUNK

    def full(a):
        return pl.BlockSpec(a.shape, lambda bi, c: (0,) * a.ndim)

    args = (conv_w, _lane_row(dt_bias, GDN_HEADS), _lane_row(a_log, GDN_HEADS), norm_g.reshape(1, GDN_DV),
            _tri(cl, False), _tri(cl, True))
    rows = GDN_STEP_CHUNKS * cl
    assert s % rows == 0
    return pl.pallas_call(
        _gdn_kernel,
        grid=(b, s // rows),
        in_specs=[pl.BlockSpec((1, rows, GDN_CONV_DIM), lambda bi, c: (bi, c, 0)),
                  pl.BlockSpec((1, rows, GDN_V_W), lambda bi, c: (bi, c, 0)),
                  pl.BlockSpec((1, rows, LANE), lambda bi, c: (bi, c, 0))] + [full(a) for a in args],
        out_specs=pl.BlockSpec((1, rows, GDN_V_W), lambda bi, c: (bi, c, 0)),
        out_shape=jax.ShapeDtypeStruct((b, s, GDN_V_W), BF16),
        scratch_shapes=[pltpu.VMEM((CONV_HALO, GDN_CONV_DIM), F32),
                        pltpu.VMEM((GDN_HEADS, GDN_DK, GDN_DV), F32)],
        compiler_params=_cparams(("parallel", "arbitrary")),
        name="gdn_mixer",
    )(qkv, z, sm, *args)


def _pad_cols(w, width=LANE):
    return jnp.pad(w, ((0, 0), (0, width - w.shape[1])))


HY_Q_W = NSA_HEADS * NSA_DH
HY_KV_W = 6 * NSA_GROUPS * NSA_DH
HY_GATE_W = NSA_HEADS * 3


def hybrid_heads(x2d, b, s, positions, g_pre, w_in, pe_k, pe_v, ck_w1, ck_b1, ck_w2,
                 cv_w1, cv_b1, cv_w2, conv_w, conv_b, dt_bias, a_log, d_skip, norm_g):
    o = 0
    cols = {}
    for name, width in (("q", HY_Q_W), ("kv", HY_KV_W), ("gate", HY_GATE_W), ("z", SSD_INNER),
                        ("xbc", SSD_CONV_DIM), ("dt", SSD_HEADS)):
        cols[name] = w_in[:, o:o + width]
        o += width
    gpg = HY_GATE_W // NSA_GROUPS
    w_cat = jnp.concatenate(
        [cols["q"], cols["kv"], cols["z"], cols["xbc"]]
        + [_pad_cols(cols["gate"][:, g * gpg:(g + 1) * gpg]) for g in range(NSA_GROUPS)]
        + [_pad_cols(cols["dt"])], axis=1).astype(BF16)
    splits = (HY_Q_W, HY_KV_W, SSD_INNER, SSD_CONV_DIM, NSA_GROUPS * LANE, LANE)
    q, kv, z, xbc, gates, dts = norm_matmul(x2d, g_pre, w_cat, splits)

    qh, kcmp, vcmp, kslc, vslc, kwin, vwin = nsa_prep(q.reshape(b, s, -1), kv.reshape(b, s, -1), positions)
    kc, vc = nsa_compress(kcmp, vcmp, pe_k, pe_v, ck_w1, ck_b1, ck_w2, cv_w1, cv_b1, cv_w2)
    o_nsa = nsa_attention(qh, kc, vc, kslc, vslc, kwin, vwin, gates.reshape(b, s, -1))
    y = ssd_mixer(z.reshape(b, s, -1), xbc.reshape(b, s, -1), dts.reshape(b, s, -1),
                  conv_w, conv_b, dt_bias, a_log, d_skip, norm_g)
    return [o_nsa.reshape(b * s, -1), y.reshape(b * s, -1)]


def gdn_heads(x2d, b, s, g_pre, w_in, conv_w, dt_bias, a_log, norm_g):
    w_cat = jnp.concatenate([w_in[:, :GDN_CONV_DIM + GDN_V_W], _pad_cols(w_in[:, GDN_CONV_DIM + GDN_V_W:])],
                            axis=1).astype(BF16)
    qkv, z, sm = norm_matmul(x2d, g_pre, w_cat, (GDN_CONV_DIM, GDN_V_W, LANE))
    o = gdn_mixer(qkv.reshape(b, s, -1), z.reshape(b, s, -1), sm.reshape(b, s, -1), conv_w, dt_bias, a_log, norm_g)
    return [o.reshape(b * s, -1)]


def kernel(x, positions, norm_mix_pre, norm_mix_post, norm_ffn_pre, norm_ffn_post, hy_w_in, hy_w_out, nsa_pe_k, nsa_pe_v, nsa_ck_w1, nsa_ck_b1, nsa_ck_w2, nsa_cv_w1, nsa_cv_b1, nsa_cv_w2, ssd_conv_w, ssd_conv_b, ssd_dt_bias, ssd_a_log, ssd_d, ssd_norm, gdn_w_in, gdn_conv_w, gdn_dt_bias, gdn_a_log, gdn_norm, gdn_w_out, ffn_w_up, ffn_conv_w, ffn_conv_b, ffn_w_down):
    b, s, d = x.shape
    x2d = x.reshape(b * s, d)
    depth = norm_mix_pre.shape[0]
    for layer in range(depth):
        e = layer // 2
        if layer % 2 == 0:
            parts = hybrid_heads(x2d, b, s, positions, norm_mix_pre[layer], hy_w_in[e], nsa_pe_k[e], nsa_pe_v[e],
                                 nsa_ck_w1[e], nsa_ck_b1[e], nsa_ck_w2[e], nsa_cv_w1[e], nsa_cv_b1[e], nsa_cv_w2[e],
                                 ssd_conv_w[e], ssd_conv_b[e], ssd_dt_bias[e], ssd_a_log[e], ssd_d[e], ssd_norm[e])
            w_out = hy_w_out[e]
        else:
            parts = gdn_heads(x2d, b, s, norm_mix_pre[layer], gdn_w_in[e], gdn_conv_w[e], gdn_dt_bias[e],
                              gdn_a_log[e], gdn_norm[e])
            w_out = gdn_w_out[e]
        x2d = mixer_out_ffn(x2d, parts, w_out.astype(BF16), norm_mix_post[layer], s, norm_ffn_pre[layer],
                            ffn_w_up[layer].astype(BF16), ffn_conv_w[layer], ffn_conv_b[layer],
                            ffn_w_down[layer].astype(BF16), norm_ffn_post[layer])
    return x2d.reshape(b, s, d)
```

```python
import functools
import math

import numpy as np
import jax
import jax.numpy as jnp
from jax import lax
from jax.experimental import pallas as pl
from jax.experimental.pallas import tpu as pltpu

F32 = jnp.float32
BF16 = jnp.bfloat16

D_MODEL = 1024
NORM_EPS = 1e-6
MASKED = -1e30

NSA_HEADS = 8
NSA_GROUPS = 2
NSA_REP = NSA_HEADS // NSA_GROUPS
NSA_DH = 64
CMP_BLOCK = 32
CMP_STRIDE = 16
SLC_BLOCK = 64
SLC_TOPK = 16
WINDOW = 512
ROPE_THETA = 500000.0
ROT_DIM = NSA_DH // 4
FORCED_SCORE = 1e9

SSD_HEADS = 8
SSD_P = 64
SSD_INNER = SSD_HEADS * SSD_P
SSD_GROUPS = 2
SSD_N = 128
SSD_CHUNK = 256
SSD_CONV_DIM = SSD_INNER + 2 * SSD_GROUPS * SSD_N

GDN_HEADS = 8
GDN_DK = 128
GDN_DV = 128
GDN_CHUNK = 64
GDN_QK_W = GDN_HEADS * GDN_DK
GDN_V_W = GDN_HEADS * GDN_DV
GDN_CONV_DIM = 2 * GDN_QK_W + GDN_V_W

FFN_DIM = 2816

LANE = 128
CONV_HALO = 8
FFN_HALO = 16
VMEM_LIMIT = 56 * 1024 * 1024
ROW_TILE = 512
PREP_TILE = 256


def _cparams(sem):
    return pltpu.CompilerParams(dimension_semantics=sem, vmem_limit_bytes=VMEM_LIMIT)


def _rms(x, g):
    return x * lax.rsqrt(jnp.mean(x * x, axis=-1, keepdims=True) + NORM_EPS) * g


def _silu(x):
    return x * jax.nn.sigmoid(x)


def _softplus(x):
    return jnp.maximum(x, 0.0) + jnp.log1p(jnp.exp(-jnp.abs(x)))


def _dot(a, b):
    return jnp.dot(a, b, preferred_element_type=F32)


def _dot_nt(a, b):
    return lax.dot_general(a, b, (((1,), (1,)), ((), ())), preferred_element_type=F32)


def _dot_tn(a, b):
    return lax.dot_general(a, b, (((0,), (0,)), ((), ())), preferred_element_type=F32)


def _split3(x):
    x1 = x.astype(BF16)
    r1 = x - x1.astype(F32)
    x2 = r1.astype(BF16)
    x3 = (r1 - x2.astype(F32)).astype(BF16)
    return x1, x2, x3


def _split2(x):
    x1 = x.astype(BF16)
    x2 = (x - x1.astype(F32)).astype(BF16)
    return x1, x2


def _cumsum_cols(tril, a):
    a1, a2, a3 = _split3(a)
    return _dot(tril, a1) + _dot(tril, a2) + _dot(tril, a3)


def _cumsum_rows_t(a, triu):
    a1, a2, a3 = _split3(a)
    return _dot_tn(a1, triu) + _dot_tn(a2, triu) + _dot_tn(a3, triu)


def _causal_conv(ext, cw, halo):
    width = cw.shape[0]
    y = ext[halo:] * cw[width - 1:width]
    for k in range(1, width):
        y = y + pltpu.roll(ext, k, 0)[halo:] * cw[width - 1 - k:width - k]
    return y


def _inproj_kernel(x_ref, g_ref, w_ref, *out_refs, splits):
    hn = _rms(x_ref[...], g_ref[...]).astype(BF16)
    y = _dot(hn, w_ref[...])
    off = 0
    for o_ref, n in zip(out_refs, splits):
        o_ref[...] = y[:, off:off + n].astype(o_ref.dtype)
        off += n


def norm_matmul(x2d, g, w, splits, tm=ROW_TILE):
    t = x2d.shape[0]
    n = w.shape[1]
    assert sum(splits) == n and t % tm == 0
    return pl.pallas_call(
        functools.partial(_inproj_kernel, splits=splits),
        grid=(t // tm,),
        in_specs=[pl.BlockSpec((tm, D_MODEL), lambda i: (i, 0)),
                  pl.BlockSpec((1, D_MODEL), lambda i: (0, 0)),
                  pl.BlockSpec((D_MODEL, n), lambda i: (0, 0))],
        out_specs=[pl.BlockSpec((tm, s), lambda i: (i, 0)) for s in splits],
        out_shape=[jax.ShapeDtypeStruct((t, s), F32) for s in splits],
        compiler_params=_cparams(("parallel",)),
        name="norm_matmul",
    )(x2d, g.reshape(1, D_MODEL), w)


FFN_CHUNK = 256


def _ffn_kernel(*refs, n_parts, tm, seq):
    x_ref, halo_ref = refs[:2]
    part_refs = refs[2:2 + n_parts]
    part_halo_refs = refs[2 + n_parts:2 + 2 * n_parts]
    (wo_ref, gmix_ref, gpre_ref, wup_ref, cw_ref, cb_ref, wd_ref, gpost_ref, o_ref, hn_ref) = refs[2 + 2 * n_parts:]

    def after_mixer(x, parts):
        acc = None
        off = 0
        for p_ref in parts:
            k = p_ref.shape[-1]
            part = _dot(p_ref[...], wo_ref[off:off + k, :])
            acc = part if acc is None else acc + part
            off += k
        return x + _rms(acc, gmix_ref[...])

    o_ref[...] = after_mixer(x_ref[...], part_refs)
    keep = ((pl.program_id(0) * tm) % seq != 0).astype(F32)
    hn_ref[:FFN_HALO, :] = (_rms(after_mixer(halo_ref[...], part_halo_refs), gpre_ref[...]) * keep).astype(BF16)
    hn_ref[FFN_HALO:, :] = _rms(o_ref[...], gpre_ref[...]).astype(BF16)
    hn = hn_ref[...]

    def up(lo):
        return _dot(hn, wup_ref[:, lo:lo + FFN_CHUNK]), _dot(hn, wup_ref[:, FFN_DIM + lo:FFN_DIM + lo + FFN_CHUNK])

    def conv(u, lo):
        return _causal_conv(u, cw_ref[:, lo:lo + FFN_CHUNK], FFN_HALO) + cb_ref[:, lo:lo + FFN_CHUNK]

    hidden = []
    n_chunks = FFN_DIM // FFN_CHUNK
    ahead = up(0)
    for c in range(n_chunks):
        lo = c * FFN_CHUNK
        u_gate, u_val = ahead
        if c + 1 < n_chunks:
            ahead = up(lo + FFN_CHUNK)
        hidden.append((_silu(conv(u_gate, lo)) * conv(u_val, FFN_DIM + lo)).astype(BF16))
    down = _dot(jnp.concatenate(hidden, axis=1), wd_ref[...])
    o_ref[...] = o_ref[...] + _rms(down, gpost_ref[...])


def mixer_out_ffn(x2d, parts, w_out, gmix, seq, gpre, w_up, conv_w, conv_b, w_down, gpost, tm=ROW_TILE):
    t = x2d.shape[0]
    assert FFN_DIM % FFN_CHUNK == 0 and t % tm == 0 and seq % tm == 0 and tm % FFN_HALO == 0
    hb = tm // FFN_HALO

    def resident(a):
        return pl.BlockSpec(a.shape, lambda i: (0,) * a.ndim, pipeline_mode=pl.Buffered(1))

    def halo(width):
        return pl.BlockSpec((FFN_HALO, width), lambda i: (jnp.maximum(i * hb - 1, 0), 0))

    consts = (w_out, gmix.reshape(1, D_MODEL), gpre.reshape(1, D_MODEL), w_up, conv_w,
              conv_b.reshape(1, 2 * FFN_DIM), w_down, gpost.reshape(1, D_MODEL))
    return pl.pallas_call(
        functools.partial(_ffn_kernel, n_parts=len(parts), tm=tm, seq=seq),
        grid=(t // tm,),
        in_specs=[pl.BlockSpec((tm, D_MODEL), lambda i: (i, 0)), halo(D_MODEL)]
        + [pl.BlockSpec((tm, p.shape[1]), lambda i: (i, 0)) for p in parts]
        + [halo(p.shape[1]) for p in parts]
        + [resident(a) for a in consts],
        out_specs=pl.BlockSpec((tm, D_MODEL), lambda i: (i, 0)),
        out_shape=jax.ShapeDtypeStruct((t, D_MODEL), F32),
        scratch_shapes=[pltpu.VMEM((tm + FFN_HALO, D_MODEL), BF16)],
        compiler_params=_cparams(("parallel",)),
        name="mixer_out_ffn",
    )(x2d, x2d, *parts, *parts, *consts)


def _nsa_prep_kernel(q_ref, kv_ref, pos_ref, freq_ref, ecos_ref, esin_ref, rest_ref,
                     qh_ref, kcmp_ref, vcmp_ref, kslc_ref, vslc_ref, kwin_ref, vwin_ref):
    ang = freq_ref[...] * pos_ref[0].astype(F32)

    def spread(t, e_ref):
        t1, t2, t3 = _split3(t)
        e = e_ref[...]
        return _dot_tn(t1, e) + _dot_tn(t2, e) + _dot_tn(t3, e)

    cs = spread(jnp.cos(ang), ecos_ref) + rest_ref[...]
    sn = spread(jnp.sin(ang), esin_ref)
    lane = lax.broadcasted_iota(jnp.int32, (1, LANE), 1) % NSA_DH
    first_half = lane < ROT_DIM // 2

    def rope(x):
        partner = jnp.where(first_half, pltpu.roll(x, LANE - ROT_DIM // 2, 1), pltpu.roll(x, ROT_DIM // 2, 1))
        return x * cs + partner * sn

    ts = q_ref.shape[1]
    aug_w = kslc_ref.shape[-1]
    low_half = lax.broadcasted_iota(jnp.int32, (1, LANE), 1) < NSA_DH

    def widen(x, upper, fill):
        low = jnp.where(low_half, pltpu.roll(x, NSA_DH, 1) if upper else x, fill[:, :LANE])
        return low if aug_w == LANE else jnp.concatenate([low, fill[:, LANE:]], axis=1)

    scale = NSA_DH ** -0.5 * math.log2(math.e)
    for j in range(NSA_HEADS // 2):
        t = (rope(q_ref[0, :, j * LANE:(j + 1) * LANE]) * scale).T
        qh_ref[0, 2 * j] = t[:NSA_DH].astype(BF16)
        qh_ref[0, 2 * j + 1] = t[NSA_DH:].astype(BF16)

    tok = pl.program_id(1) * ts + lax.broadcasted_iota(jnp.int32, (ts, 1), 0)
    block_onehot = jnp.where(lax.broadcasted_iota(jnp.int32, (1, aug_w), 1) - NSA_DH == tok // SLC_BLOCK, 1.0, 0.0)
    t = rope(kv_ref[0, :, 2 * LANE:3 * LANE])
    for g in range(NSA_GROUPS):
        kslc_ref[0, g] = widen(t, g == 1, block_onehot).astype(BF16)

    for i, o_ref in ((0, kcmp_ref), (4, kwin_ref), (1, vcmp_ref)):
        t = kv_ref[0, :, i * LANE:(i + 1) * LANE]
        if i % 2 == 0:
            t = rope(t)
        for g in range(NSA_GROUPS):
            o_ref[0, g] = t[:, g * NSA_DH:(g + 1) * NSA_DH].astype(o_ref.dtype)
    ones_row = jnp.where(lax.broadcasted_iota(jnp.int32, (NSA_VROWS - NSA_DH, ts), 0) == 0, 1.0, 0.0)
    for i, o_ref in ((3, vslc_ref), (5, vwin_ref)):
        t = kv_ref[0, :, i * LANE:(i + 1) * LANE].T
        for g in range(NSA_GROUPS):
            o_ref[0, g] = jnp.concatenate([t[g * NSA_DH:(g + 1) * NSA_DH, :], ones_row], axis=0).astype(o_ref.dtype)


NSA_VROWS = NSA_DH + 16


def _aug_width(seq):
    return -(-(NSA_DH + seq // SLC_BLOCK) // LANE) * LANE


def nsa_prep(q, kv, positions, ts=PREP_TILE):
    b, s, _ = q.shape
    half = ROT_DIM // 2
    freq_rows = 16
    inv_freq = ROPE_THETA ** (-jnp.arange(0, ROT_DIM, 2, dtype=F32) / ROT_DIM)
    freq = jnp.zeros((freq_rows, 1), F32).at[:half, 0].set(inv_freq)
    d = np.arange(LANE) % NSA_DH
    j = np.arange(freq_rows)[:, None]
    e_cos = ((d[None, :] < ROT_DIM) & (d[None, :] % half == j)).astype(np.float32)
    e_sin = e_cos * np.where(d[None, :] < half, -1.0, 1.0)
    rest = (d >= ROT_DIM).astype(np.float32).reshape(1, LANE)
    tables = (jnp.asarray(e_cos, dtype=BF16), jnp.asarray(e_sin, dtype=BF16), jnp.asarray(rest))
    tok_shape = (b, NSA_GROUPS, s, NSA_DH)
    tok_spec = pl.BlockSpec((1, NSA_GROUPS, ts, NSA_DH), lambda bi, i: (bi, 0, i, 0))
    feat_shape = (b, NSA_GROUPS, NSA_VROWS, s)
    feat_spec = pl.BlockSpec((1, NSA_GROUPS, NSA_VROWS, ts), lambda bi, i: (bi, 0, 0, i))
    aug_w = _aug_width(s)
    return pl.pallas_call(
        _nsa_prep_kernel,
        grid=(b, s // ts),
        in_specs=[pl.BlockSpec((1, ts, q.shape[2]), lambda bi, i: (bi, i, 0)),
                  pl.BlockSpec((1, ts, kv.shape[2]), lambda bi, i: (bi, i, 0)),
                  pl.BlockSpec((1, 1, ts), lambda bi, i: (bi, 0, i)),
                  pl.BlockSpec((freq_rows, 1), lambda bi, i: (0, 0))]
        + [pl.BlockSpec(t.shape, lambda bi, i: (0, 0)) for t in tables],
        out_specs=[pl.BlockSpec((1, NSA_HEADS, NSA_DH, ts), lambda bi, i: (bi, 0, 0, i)),
                   tok_spec, tok_spec,
                   pl.BlockSpec((1, NSA_GROUPS, ts, aug_w), lambda bi, i: (bi, 0, i, 0)),
                   feat_spec, tok_spec, feat_spec],
        out_shape=[jax.ShapeDtypeStruct((b, NSA_HEADS, NSA_DH, s), BF16),
                   jax.ShapeDtypeStruct(tok_shape, F32), jax.ShapeDtypeStruct(tok_shape, F32),
                   jax.ShapeDtypeStruct((b, NSA_GROUPS, s, aug_w), BF16), jax.ShapeDtypeStruct(feat_shape, BF16),
                   jax.ShapeDtypeStruct(tok_shape, BF16), jax.ShapeDtypeStruct(feat_shape, BF16)],
        compiler_params=_cparams(("parallel", "parallel")),
        name="nsa_prep",
    )(q, kv, positions.reshape(b, 1, s), freq, *tables)


def _compress_kernel(k_ref, v_ref, pek_ref, pev_ref, kw1_ref, vw1_ref, kb1_ref, vb1_ref, kw2_ref, vw2_ref,
                     kc_ref, vc_ref):
    half = CMP_STRIDE * NSA_DH

    def mlp(t_ref, pe_ref, w1_ref, b1_ref, w2_ref):
        a = t_ref[0, 0]
        n = a.shape[0]
        h_lo = _dot((a + pe_ref[:, :half]).astype(BF16), w1_ref[:half, :])
        h_hi = _dot((a + pe_ref[:, half:]).astype(BF16), w1_ref[half:, :])
        hid = _silu(h_lo + pltpu.roll(h_hi, n - 1, 0) + b1_ref[...])
        return _dot(hid.astype(BF16), w2_ref[...])

    kc_ref[0, 0] = mlp(k_ref, pek_ref, kw1_ref, kb1_ref, kw2_ref).astype(kc_ref.dtype)
    vc_ref[0, 0] = mlp(v_ref, pev_ref, vw1_ref, vb1_ref, vw2_ref).astype(vc_ref.dtype).T


def nsa_compress(kcmp, vcmp, pe_k, pe_v, ck_w1, ck_b1, ck_w2, cv_w1, cv_b1, cv_w2):
    b, g, s, dh = kcmp.shape
    n = s // CMP_STRIDE
    wide = CMP_STRIDE * dh
    assert CMP_BLOCK == 2 * CMP_STRIDE
    kr = kcmp.reshape(b, g, n, wide)
    vr = vcmp.reshape(b, g, n, wide)
    blk = pl.BlockSpec((1, 1, n, wide), lambda bi, gi: (bi, gi, 0, 0))

    def full(a):
        return pl.BlockSpec(a.shape, lambda bi, gi: (0,) * a.ndim)

    args = (pe_k.reshape(1, 2 * wide), pe_v.reshape(1, 2 * wide), ck_w1.astype(BF16), cv_w1.astype(BF16),
            ck_b1.reshape(1, dh), cv_b1.reshape(1, dh), ck_w2.astype(BF16), cv_w2.astype(BF16))
    return pl.pallas_call(
        _compress_kernel,
        grid=(b, g),
        in_specs=[blk, blk] + [full(a) for a in args],
        out_specs=[pl.BlockSpec((1, 1, n, dh), lambda bi, gi: (bi, gi, 0, 0)),
                   pl.BlockSpec((1, 1, dh, n), lambda bi, gi: (bi, gi, 0, 0))],
        out_shape=[jax.ShapeDtypeStruct((b, g, n, dh), BF16), jax.ShapeDtypeStruct((b, g, dh, n), BF16)],
        compiler_params=_cparams(("parallel", "parallel")),
        name="nsa_compress",
    )(kr, vr, *args)


ATT_TQ = 256
ATT_TK = 256
ATT_RING = 4
SEL_SIZES = 4


def _att_kernel(q_ref, kc_ref, vc_ref, ks_ref, vs_ref, kw_ref, vw_ref, gate_ref, ov_ref, o_ref,
                *s_refs, seq):
    tq, tk, rep, dh = ATT_TQ, ATT_TK, NSA_REP, NSA_DH
    cols = rep * tq
    nblk = seq // SLC_BLOCK
    ncmp = kc_ref.shape[2]
    aug_w = ks_ref.shape[-1]
    t0 = pl.program_id(2) * tq
    q = jnp.concatenate([q_ref[0, r] for r in range(rep)], axis=1)
    tpos = t0 + lax.broadcasted_iota(jnp.int32, (1, tq), 1)

    def per_head(x):
        return jnp.concatenate([x] * rep, axis=1)


    band = WINDOW + tq
    start = pl.multiple_of(jnp.maximum(t0 - WINDOW, 0), tq)
    dlt = tpos - (start + lax.broadcasted_iota(jnp.int32, (band, 1), 0))
    bias_w = jnp.where((dlt >= 0) & (dlt < WINDOW), 0.0, MASKED)
    s_w = _dot(kw_ref[0, 0, pl.ds(start, band), :], q) + per_head(bias_w)

    cmp_end = lax.broadcasted_iota(jnp.int32, (ncmp, 1), 0) * CMP_STRIDE + (CMP_BLOCK - 1)
    s_c = _dot(kc_ref[0, 0], q) + per_head(jnp.where(cmp_end <= tpos, 0.0, MASKED))
    p_c = jnp.exp2(s_c - jnp.max(s_c, axis=0, keepdims=True))
    l_c = jnp.sum(p_c, axis=0, keepdims=True)
    p_c = p_c * (per_head(jnp.where(tpos >= CMP_BLOCK - 1, 1.0, 0.0)) / l_c)
    o_c = _dot(vc_ref[0, 0], p_c.astype(BF16))

    p_sum = p_c[:, :tq]
    for r in range(1, rep):
        p_sum = p_sum + p_c[:, r * tq:(r + 1) * tq]
    p_hi, p_lo = _split2(p_sum)
    importance = _dot(ov_ref[...], p_hi) + _dot(ov_ref[...], p_lo)
    cur = tpos // SLC_BLOCK

    def select(n):
        def run():
            blk = lax.broadcasted_iota(jnp.int32, (n, 1), 0)
            forced = (blk == 0) | (blk == cur) | (blk == cur - 1)
            imp = jnp.where(forced, FORCED_SCORE, jnp.where(blk <= cur, importance[:n], -1.0))
            sel = jnp.full((n, tq), MASKED, F32)
            for _ in range(min(SLC_TOPK, nblk)):
                best = jnp.max(imp, axis=0, keepdims=True)
                first = jnp.min(jnp.where(imp == best, blk, n), axis=0, keepdims=True)
                hit = blk == first
                sel = jnp.where(hit, 0.0, sel)
                imp = jnp.where(hit, -jnp.inf, imp)
            return sel if n == nblk else jnp.concatenate([sel, jnp.full((nblk - n, tq), MASKED, F32)], axis=0)
        return run

    blk_step = nblk // SEL_SIZES
    size_class = ((t0 + tq - 1) // SLC_BLOCK) // blk_step
    pick = select(nblk)
    for k in reversed(range(SEL_SIZES - 1)):
        pick = functools.partial(lax.cond, size_class == k, select((k + 1) * blk_step), pick)
    sel = pick()

    mask_rows = jnp.concatenate([sel.astype(BF16), jnp.zeros((aug_w - dh - nblk, tq), BF16)], axis=0)
    q_aug = jnp.concatenate([q, per_head(mask_rows)], axis=0)
    kidx = lax.broadcasted_iota(jnp.int32, (tk, 1), 0)

    def produce(kt, s_ref):
        k0 = pl.multiple_of(kt * tk, tk)
        s = _dot(ks_ref[0, 0, pl.ds(k0, tk), :], q_aug)
        s_ref[...] = s
        return jnp.max(s, axis=0, keepdims=True)

    def consume(kt, s_ref, tile_max, carry, causal):
        m, acc = carry
        k0 = pl.multiple_of(kt * tk, tk)
        v = vs_ref[0, 0, :, pl.ds(k0, tk)]
        s = s_ref[...]
        if causal:
            s = s + per_head(jnp.where(k0 + kidx <= tpos, 0.0, MASKED))
            tile_max = jnp.max(s, axis=0, keepdims=True)
        m_new = jnp.maximum(m, tile_max)
        alpha = jnp.exp2(m - m_new)
        p = jnp.exp2(s - m_new)
        return m_new, alpha * acc + _dot(v, p.astype(BF16))

    nbuf = len(s_refs)

    def slc_trip(j, carry):
        maxes, state = list(carry[:nbuf]), carry[nbuf:]
        for i, s_ref in enumerate(s_refs):
            state = consume(nbuf * j + i, s_ref, maxes[i], state, False)
            maxes[i] = produce(nbuf * (j + 1) + i, s_ref)
        return tuple(maxes) + state

    n_full = t0 // (nbuf * tk)
    init = tuple(produce(i, s_ref) for i, s_ref in enumerate(s_refs)) + (
        jnp.full((1, cols), MASKED, F32), jnp.zeros((NSA_VROWS, cols), F32))

    p_w = jnp.exp2(s_w - jnp.max(s_w, axis=0, keepdims=True))
    o_w = _dot(vw_ref[0, 0, :, pl.ds(start, band)], p_w.astype(BF16))
    o_w = o_w[:dh] / o_w[dh:dh + 1]

    carry = lax.fori_loop(0, n_full, slc_trip, init)
    maxes, state = carry[:nbuf], carry[nbuf:]
    own = (t0 - n_full * nbuf * tk) // tk

    def last_trip(n_before):
        def run(st):
            for i in range(n_before):
                st = consume(nbuf * n_full + i, s_refs[i], maxes[i], st, False)
            return consume(nbuf * n_full + n_before, s_refs[n_before], None, st, True)
        return run

    finish = last_trip(nbuf - 1)
    for n_before in reversed(range(nbuf - 1)):
        finish = functools.partial(lax.cond, own == n_before, last_trip(n_before), finish)
    acc_s = finish(state)[1]
    o_s = acc_s[:dh] / acc_s[dh:dh + 1]

    gates = jax.nn.sigmoid(gate_ref[0]).T
    merged = []
    for r in range(rep):
        sl = slice(r * tq, (r + 1) * tq)
        merged.append(gates[3 * r:3 * r + 1] * o_c[:, sl] + gates[3 * r + 1:3 * r + 2] * o_s[:, sl]
                      + gates[3 * r + 2:3 * r + 3] * o_w[:, sl])
    o_ref[0] = jnp.concatenate(merged, axis=0).T.astype(o_ref.dtype)


def _overlap_matrix(seq):
    ncp = seq // CMP_STRIDE
    n_cmp = (seq - CMP_BLOCK) // CMP_STRIDE + 1
    nblk = seq // SLC_BLOCK
    cs = np.arange(ncp) * CMP_STRIDE
    ss = np.arange(nblk) * SLC_BLOCK
    ov = ((cs[None, :] <= ss[:, None] + SLC_BLOCK - 1) & (cs[None, :] + CMP_BLOCK - 1 >= ss[:, None])
          & (np.arange(ncp)[None, :] < n_cmp))
    return jnp.asarray(ov.astype(np.float32), dtype=BF16)


def nsa_attention(qh, kc, vc, kslc, vslc, kwin, vwin, gates):
    b, _, dh, s = qh.shape
    aug_w = kslc.shape[-1]
    g = NSA_GROUPS
    ncp = kc.shape[2]
    nblk = s // SLC_BLOCK
    assert s % (ATT_RING * ATT_TK) == 0 and s >= WINDOW + ATT_TQ and aug_w == _aug_width(s)
    ov = _overlap_matrix(s)
    tok_spec = pl.BlockSpec((1, 1, s, dh), lambda bi, gi, i: (bi, gi, 0, 0))
    feat_spec = pl.BlockSpec((1, 1, NSA_VROWS, s), lambda bi, gi, i: (bi, gi, 0, 0))
    return pl.pallas_call(
        functools.partial(_att_kernel, seq=s),
        grid=(b, g, s // ATT_TQ),
        in_specs=[pl.BlockSpec((1, NSA_REP, dh, ATT_TQ), lambda bi, gi, i: (bi, gi, 0, i)),
                  pl.BlockSpec((1, 1, ncp, dh), lambda bi, gi, i: (bi, gi, 0, 0)),
                  pl.BlockSpec((1, 1, dh, ncp), lambda bi, gi, i: (bi, gi, 0, 0)),
                  pl.BlockSpec((1, 1, s, aug_w), lambda bi, gi, i: (bi, gi, 0, 0)),
                  feat_spec, tok_spec, feat_spec,
                  pl.BlockSpec((1, ATT_TQ, LANE), lambda bi, gi, i: (bi, i, gi)),
                  pl.BlockSpec((nblk, ncp), lambda bi, gi, i: (0, 0))],
        out_specs=pl.BlockSpec((1, ATT_TQ, NSA_REP * dh), lambda bi, gi, i: (bi, i, gi)),
        out_shape=jax.ShapeDtypeStruct((b, s, NSA_HEADS * dh), BF16),
        scratch_shapes=[pltpu.VMEM((ATT_TK, NSA_REP * ATT_TQ), F32)] * ATT_RING,
        compiler_params=_cparams(("parallel", "parallel", "arbitrary")),
        name="nsa_attention",
    )(qh, kc, vc, kslc, vslc, kwin, vwin, gates, ov)


def _ssd_kernel(z_ref, xbc_ref, dt_ref, cw_ref, cb_ref, dtb_ref, alog_ref, drow_ref, ng_ref, tril_ref, triu_ref,
                o_ref, carry_ref, state_ref, y_ref):
    cl = SSD_CHUNK
    hpg = SSD_HEADS // SSD_GROUPS

    @pl.when(pl.program_id(1) == 0)
    def _():
        carry_ref[...] = jnp.zeros_like(carry_ref)
        state_ref[...] = jnp.zeros_like(state_ref)

    xbc = xbc_ref[0]
    ext = jnp.concatenate([carry_ref[...], xbc], axis=0)
    act = _silu(_causal_conv(ext, cw_ref[...], CONV_HALO) + cb_ref[...])
    carry_ref[...] = xbc[cl - CONV_HALO:, :]

    xs = act[:, :SSD_INNER]
    dt = _softplus(dt_ref[0] + dtb_ref[...])
    a = dt * (-jnp.exp(alog_ref[...]))
    acs_c = _cumsum_cols(tril_ref[...], a)
    acs_r = _cumsum_rows_t(a, triu_ref[...])
    causal = lax.broadcasted_iota(jnp.int32, (cl, cl), 0) >= lax.broadcasted_iota(jnp.int32, (cl, cl), 1)

    for g in range(SSD_GROUPS):
        bg = act[:, SSD_INNER + g * SSD_N:SSD_INNER + (g + 1) * SSD_N].astype(BF16)
        cg = act[:, SSD_INNER + (SSD_GROUPS + g) * SSD_N:SSD_INNER + (SSD_GROUPS + g + 1) * SSD_N].astype(BF16)
        cb = _dot_nt(cg, bg)
        state = state_ref[g]
        y_off = _dot(cg, state.astype(BF16))
        weighted = []
        decay = []
        for hl in range(hpg):
            h = g * hpg + hl
            col = acs_c[:, h:h + 1]
            row = acs_r[h:h + 1, :]
            lmat = jnp.exp(jnp.where(causal, col - row, MASKED))
            xh = xs[:, h * SSD_P:(h + 1) * SSD_P]
            xdt = xh * dt[:, h:h + 1]
            y_diag = _dot((cb * lmat).astype(BF16), xdt.astype(BF16))
            y_ref[:, h * SSD_P:(h + 1) * SSD_P] = (y_diag + y_off[:, hl * SSD_P:(hl + 1) * SSD_P] * jnp.exp(col)
                                                   + drow_ref[:, h * SSD_P:(h + 1) * SSD_P] * xh)
            a_last = acs_c[cl - 1:cl, h:h + 1]
            weighted.append((xdt * jnp.exp(a_last - col)).astype(BF16))
            decay.append(jnp.broadcast_to(jnp.exp(a_last), (1, SSD_P)))
        contrib = _dot_tn(bg, jnp.concatenate(weighted, axis=1))
        state_ref[g] = state * jnp.concatenate(decay, axis=1) + contrib

    y = y_ref[...] * _silu(z_ref[0])
    gw = SSD_INNER // SSD_GROUPS
    for g in range(SSD_GROUPS):
        o_ref[0, :, g * gw:(g + 1) * gw] = _rms(y[:, g * gw:(g + 1) * gw],
                                                 ng_ref[:, g * gw:(g + 1) * gw]).astype(o_ref.dtype)


def _tri(n, upper):
    m = np.triu(np.ones((n, n), np.float32)) if upper else np.tril(np.ones((n, n), np.float32))
    return jnp.asarray(m, dtype=BF16)


def _lane_row(v, offset=0):
    row = jnp.zeros((1, LANE), F32)
    return row.at[0, offset:offset + v.shape[0]].set(v)


def ssd_mixer(z, xbc, dts, conv_w, conv_b, dt_bias, a_log, d_skip, norm_g):
    b, s, _ = z.shape
    cl = SSD_CHUNK
    hpg = SSD_HEADS // SSD_GROUPS

    def full(a):
        return pl.BlockSpec(a.shape, lambda bi, c: (0,) * a.ndim)

    args = (conv_w, conv_b.reshape(1, -1), _lane_row(dt_bias), _lane_row(a_log),
            jnp.repeat(d_skip, SSD_P).reshape(1, SSD_INNER), norm_g.reshape(1, SSD_INNER),
            _tri(cl, False), _tri(cl, True))
    return pl.pallas_call(
        _ssd_kernel,
        grid=(b, s // cl),
        in_specs=[pl.BlockSpec((1, cl, SSD_INNER), lambda bi, c: (bi, c, 0)),
                  pl.BlockSpec((1, cl, SSD_CONV_DIM), lambda bi, c: (bi, c, 0)),
                  pl.BlockSpec((1, cl, LANE), lambda bi, c: (bi, c, 0))] + [full(a) for a in args],
        out_specs=pl.BlockSpec((1, cl, SSD_INNER), lambda bi, c: (bi, c, 0)),
        out_shape=jax.ShapeDtypeStruct((b, s, SSD_INNER), BF16),
        scratch_shapes=[pltpu.VMEM((CONV_HALO, SSD_CONV_DIM), F32),
                        pltpu.VMEM((SSD_GROUPS, SSD_N, hpg * SSD_P), F32),
                        pltpu.VMEM((cl, SSD_INNER), F32)],
        compiler_params=_cparams(("parallel", "arbitrary")),
        name="ssd_mixer",
    )(z, xbc, dts, *args)


GDN_STEP_CHUNKS = 4


def _pair_diag(x):
    cl = x.shape[0]
    left = jnp.where(lax.broadcasted_iota(jnp.int32, x.shape, 1) < cl, 1.0, 0.0).astype(BF16)
    return jnp.concatenate([x * left, x * (1.0 - left).astype(BF16)], axis=0)


def _mm_pairs(a, b_diag):
    return _dot(a[0], b_diag[0]) + _dot(a[0], b_diag[1]) + _dot(a[1], b_diag[0])


def _unit_lower_inverses(a_list, row, col, fillers=()):
    eye = jnp.where(row == col, 1.0, 0.0)
    same16 = (row // 16) == (col // 16)
    same32 = (row // 32) == (col // 32)
    pending = list(fillers)

    def fill(anchor):
        if pending:
            pending.pop(0)(anchor[:1, :1] * 0.0)

    def diag_parts(x):
        hi, lo = _split2(x)
        return (hi, lo), (_pair_diag(hi), _pair_diag(lo))

    n1 = [jnp.where(same16, -a, 0.0) for a in a_list]
    p = [eye + x for x in n1]
    n = [diag_parts(x) for x in n1]
    anchor = p[-1]
    for _ in range(3):
        squares = [_mm_pairs(x, xd) for x, xd in n]
        fill(anchor)
        anchor = squares[-1]
        n = [diag_parts(x) for x in squares]
        updates = [_mm_pairs(_split2(x), yd) for x, (_, yd) in zip(p, n)]
        fill(anchor)
        p = [x + y for x, y in zip(p, updates)]
        anchor = p[-1]
    for level_mask in (jnp.where(same16, 0.0, jnp.where(same32, 1.0, 0.0)), jnp.where(same32, 0.0, 1.0)):
        off = [diag_parts(a * level_mask)[1] for a in a_list]
        ps = [diag_parts(x) for x in p]
        t = [_mm_pairs(x, y) for (x, _), y in zip(ps, off)]
        fill(anchor)
        anchor = t[-1]
        updates = [_mm_pairs(_split2(x), zd) for x, (_, zd) in zip(t, ps)]
        fill(anchor)
        p = [x - y for x, y in zip(p, updates)]
        anchor = p[-1]
    while pending:
        fill(anchor)
    return p


def _gdn_kernel(qkv_ref, z_ref, sm_ref, cw_ref, dtb_ref, alog_ref, ng_ref, tril_ref, triu_ref,
                o_ref, carry_ref, state_ref):
    cl = GDN_CHUNK
    rows = qkv_ref.shape[1]
    heads = range(GDN_HEADS)
    half_heads = GDN_HEADS // 2
    chunks = range(rows // cl)
    pairs = [(c, j) for c in chunks for j in range(half_heads)]
    chains = [(c, h) for c in chunks for h in heads]

    @pl.when(pl.program_id(1) == 0)
    def _():
        carry_ref[...] = jnp.zeros_like(carry_ref)
        state_ref[...] = jnp.zeros_like(state_ref)

    qkv = qkv_ref[0]
    ext = jnp.concatenate([carry_ref[...], qkv], axis=0)
    carry_ref[...] = qkv[rows - CONV_HALO:, :]

    def conv_silu(lo, width, tie=0.0):
        return _silu(_causal_conv(ext[:, lo:lo + width] + tie, cw_ref[:, lo:lo + width], CONV_HALO))

    sm = sm_ref[0]
    beta_all = jax.nn.sigmoid(sm)
    g_all = -jnp.exp(alog_ref[...]) * _softplus(sm + dtb_ref[...])
    row = lax.broadcasted_iota(jnp.int32, (cl, 2 * cl), 0)
    lane = lax.broadcasted_iota(jnp.int32, (cl, 2 * cl), 1)
    first = lane < cl
    col = jnp.where(first, lane, lane - cl)
    incl = row >= col

    def l2n(x):
        return x * lax.rsqrt(jnp.sum(x * x, axis=-1, keepdims=True) + NORM_EPS)

    def chunk(x, c):
        return x[c * cl:(c + 1) * cl]

    act_k = conv_silu(GDN_QK_W, GDN_QK_W)
    kn = [l2n(act_k[:, h * GDN_DK:(h + 1) * GDN_DK]) for h in heads]
    beta = [beta_all[:, h:h + 1] for h in heads]
    kb = [kn[h] * beta[h] for h in heads]
    knb = [x.astype(BF16) for x in kn]
    gcs_c = [_cumsum_cols(tril_ref[...], chunk(g_all, c)) for c in chunks]
    gcs_r = [_cumsum_rows_t(chunk(g_all, c), triu_ref[...]) for c in chunks]
    gc = {(c, h): gcs_c[c][:, GDN_HEADS + h:GDN_HEADS + h + 1] for c, h in chains}
    eg = {k: jnp.exp(g) for k, g in gc.items()}

    def decay(c, j):
        g_col = jnp.where(first, gc[c, 2 * j], gc[c, 2 * j + 1])
        g_row = jnp.concatenate([gcs_r[c][GDN_HEADS + 2 * j + i:GDN_HEADS + 2 * j + i + 1, :] for i in range(2)],
                                axis=1)
        return jnp.exp(jnp.where(incl, g_col - g_row, MASKED))

    dec = {k: decay(*k) for k in pairs}
    zeros_k = jnp.zeros((cl, GDN_DK), BF16)

    def against_keys(x, c, j):
        k0, k1 = chunk(knb[2 * j], c), chunk(knb[2 * j + 1], c)
        return (_dot_nt(chunk(x[2 * j], c).astype(BF16), jnp.concatenate([k0, zeros_k], axis=0))
                + _dot_nt(chunk(x[2 * j + 1], c).astype(BF16), jnp.concatenate([zeros_k, k1], axis=0)))

    a_kk = [jnp.where(row > col, against_keys(kb, c, j) * dec[c, j], 0.0) for c, j in pairs]

    qn, v, a_qk, rhs, q_decayed, k_decayed, z_gate = {}, {}, {}, {}, {}, {}, {}

    def fill_queries(j):
        def run(tie):
            act_q = conv_silu(2 * j * GDN_DK, 2 * GDN_DK, tie)
            for i in range(2):
                qn[2 * j + i] = l2n(act_q[:, i * GDN_DK:(i + 1) * GDN_DK]) * GDN_DK ** -0.5
        return run

    def fill_values(j):
        def run(tie):
            act_v = conv_silu(2 * GDN_QK_W + 2 * j * GDN_DV, 2 * GDN_DV, tie)
            for i in range(2):
                v[2 * j + i] = act_v[:, i * GDN_DV:(i + 1) * GDN_DV]
        return run

    def fill_intra(tie):
        for c, j in pairs:
            a_qk[c, j] = (against_keys(qn, c, j) * (dec[c, j] + tie)).astype(BF16)

    def fill_rhs(tie):
        for c, h in chains:
            rhs[c, h] = jnp.concatenate([chunk(v[h], c) * (chunk(beta[h], c) + tie),
                                         chunk(kb[h], c) * (eg[c, h] + tie)], axis=1).astype(BF16)

    def fill_recurrence_operands(tie):
        for c, h in chains:
            q_decayed[c, h] = chunk(qn[h], c) * (eg[c, h] + tie)
            k_decayed[c, h] = (chunk(kn[h], c) * jnp.exp(gc[c, h][cl - 1:cl, :] - gc[c, h] + tie)).astype(BF16)

    def fill_gate(tie):
        z_gate[0] = _silu(z_ref[0] + tie)

    fillers = ([fill_queries(j) for j in range(half_heads)] + [fill_values(j) for j in range(half_heads)]
               + [fill_intra, fill_rhs, fill_recurrence_operands, fill_gate])
    t_inv = [t.astype(BF16) for t in _unit_lower_inverses(a_kk, row, col, fillers)]

    def stacked(x0, x1):
        z = jnp.zeros_like(x0)
        return jnp.concatenate([jnp.concatenate([x0, z], axis=1), jnp.concatenate([z, x1], axis=1)], axis=0)

    uw = {(c, j): _dot(t, stacked(rhs[c, 2 * j], rhs[c, 2 * j + 1])) for (c, j), t in zip(pairs, t_inv)}
    width = GDN_DV + GDN_DK

    state = [state_ref[h] for h in heads]
    for c in chunks:
        sb = [s.astype(BF16) for s in state]
        u = [uw[c, h // 2][:, (h % 2) * width:(h % 2) * width + GDN_DV] for h in heads]
        w = [uw[c, h // 2][:, (h % 2) * width + GDN_DV:(h % 2 + 1) * width] for h in heads]
        ws = [_dot(jnp.concatenate([w[h], q_decayed[c, h]], axis=0).astype(BF16), sb[h]) for h in heads]
        vb = [(u[h] - ws[h][:cl]).astype(BF16) for h in heads]
        intra = [_dot(a_qk[c, j], stacked(vb[2 * j], vb[2 * j + 1])) for j in range(half_heads)]
        o = [ws[h][cl:] + intra[h // 2][:, (h % 2) * GDN_DV:(h % 2 + 1) * GDN_DV] for h in heads]
        state = [state[h] * jnp.exp(gc[c, h][cl - 1:cl, :]) + _dot_tn(k_decayed[c, h], vb[h]) for h in heads]
        for h in heads:
            out = _rms(o[h], ng_ref[...]) * z_gate[0][c * cl:(c + 1) * cl, h * GDN_DV:(h + 1) * GDN_DV]
            o_ref[0, c * cl:(c + 1) * cl, h * GDN_DV:(h + 1) * GDN_DV] = out.astype(o_ref.dtype)
    for h in heads:
        state_ref[h] = state[h]


def gdn_mixer(qkv, z, sm, conv_w, dt_bias, a_log, norm_g):
    b, s, _ = qkv.shape
    cl = GDN_CHUNK

    def full(a):
        return pl.BlockSpec(a.shape, lambda bi, c: (0,) * a.ndim)

    args = (conv_w, _lane_row(dt_bias, GDN_HEADS), _lane_row(a_log, GDN_HEADS), norm_g.reshape(1, GDN_DV),
            _tri(cl, False), _tri(cl, True))
    rows = GDN_STEP_CHUNKS * cl
    assert s % rows == 0
    return pl.pallas_call(
        _gdn_kernel,
        grid=(b, s // rows),
        in_specs=[pl.BlockSpec((1, rows, GDN_CONV_DIM), lambda bi, c: (bi, c, 0)),
                  pl.BlockSpec((1, rows, GDN_V_W), lambda bi, c: (bi, c, 0)),
                  pl.BlockSpec((1, rows, LANE), lambda bi, c: (bi, c, 0))] + [full(a) for a in args],
        out_specs=pl.BlockSpec((1, rows, GDN_V_W), lambda bi, c: (bi, c, 0)),
        out_shape=jax.ShapeDtypeStruct((b, s, GDN_V_W), BF16),
        scratch_shapes=[pltpu.VMEM((CONV_HALO, GDN_CONV_DIM), F32),
                        pltpu.VMEM((GDN_HEADS, GDN_DK, GDN_DV), F32)],
        compiler_params=_cparams(("parallel", "arbitrary")),
        name="gdn_mixer",
    )(qkv, z, sm, *args)


def _pad_cols(w, width=LANE):
    return jnp.pad(w, ((0, 0), (0, width - w.shape[1])))


HY_Q_W = NSA_HEADS * NSA_DH
HY_KV_W = 6 * NSA_GROUPS * NSA_DH
HY_GATE_W = NSA_HEADS * 3


def hybrid_heads(x2d, b, s, positions, g_pre, w_in, pe_k, pe_v, ck_w1, ck_b1, ck_w2,
                 cv_w1, cv_b1, cv_w2, conv_w, conv_b, dt_bias, a_log, d_skip, norm_g):
    o = 0
    cols = {}
    for name, width in (("q", HY_Q_W), ("kv", HY_KV_W), ("gate", HY_GATE_W), ("z", SSD_INNER),
                        ("xbc", SSD_CONV_DIM), ("dt", SSD_HEADS)):
        cols[name] = w_in[:, o:o + width]
        o += width
    gpg = HY_GATE_W // NSA_GROUPS
    w_cat = jnp.concatenate(
        [cols["q"], cols["kv"], cols["z"], cols["xbc"]]
        + [_pad_cols(cols["gate"][:, g * gpg:(g + 1) * gpg]) for g in range(NSA_GROUPS)]
        + [_pad_cols(cols["dt"])], axis=1).astype(BF16)
    splits = (HY_Q_W, HY_KV_W, SSD_INNER, SSD_CONV_DIM, NSA_GROUPS * LANE, LANE)
    q, kv, z, xbc, gates, dts = norm_matmul(x2d, g_pre, w_cat, splits)

    qh, kcmp, vcmp, kslc, vslc, kwin, vwin = nsa_prep(q.reshape(b, s, -1), kv.reshape(b, s, -1), positions)
    kc, vc = nsa_compress(kcmp, vcmp, pe_k, pe_v, ck_w1, ck_b1, ck_w2, cv_w1, cv_b1, cv_w2)
    o_nsa = nsa_attention(qh, kc, vc, kslc, vslc, kwin, vwin, gates.reshape(b, s, -1))
    y = ssd_mixer(z.reshape(b, s, -1), xbc.reshape(b, s, -1), dts.reshape(b, s, -1),
                  conv_w, conv_b, dt_bias, a_log, d_skip, norm_g)
    return [o_nsa.reshape(b * s, -1), y.reshape(b * s, -1)]


def gdn_heads(x2d, b, s, g_pre, w_in, conv_w, dt_bias, a_log, norm_g):
    w_cat = jnp.concatenate([w_in[:, :GDN_CONV_DIM + GDN_V_W], _pad_cols(w_in[:, GDN_CONV_DIM + GDN_V_W:])],
                            axis=1).astype(BF16)
    qkv, z, sm = norm_matmul(x2d, g_pre, w_cat, (GDN_CONV_DIM, GDN_V_W, LANE))
    o = gdn_mixer(qkv.reshape(b, s, -1), z.reshape(b, s, -1), sm.reshape(b, s, -1), conv_w, dt_bias, a_log, norm_g)
    return [o.reshape(b * s, -1)]


def kernel(x, positions, norm_mix_pre, norm_mix_post, norm_ffn_pre, norm_ffn_post, hy_w_in, hy_w_out, nsa_pe_k, nsa_pe_v, nsa_ck_w1, nsa_ck_b1, nsa_ck_w2, nsa_cv_w1, nsa_cv_b1, nsa_cv_w2, ssd_conv_w, ssd_conv_b, ssd_dt_bias, ssd_a_log, ssd_d, ssd_norm, gdn_w_in, gdn_conv_w, gdn_dt_bias, gdn_a_log, gdn_norm, gdn_w_out, ffn_w_up, ffn_conv_w, ffn_conv_b, ffn_w_down):
    b, s, d = x.shape
    x2d = x.reshape(b * s, d)
    depth = norm_mix_pre.shape[0]
    for layer in range(depth):
        e = layer // 2
        if layer % 2 == 0:
            parts = hybrid_heads(x2d, b, s, positions, norm_mix_pre[layer], hy_w_in[e], nsa_pe_k[e], nsa_pe_v[e],
                                 nsa_ck_w1[e], nsa_ck_b1[e], nsa_ck_w2[e], nsa_cv_w1[e], nsa_cv_b1[e], nsa_cv_w2[e],
                                 ssd_conv_w[e], ssd_conv_b[e], ssd_dt_bias[e], ssd_a_log[e], ssd_d[e], ssd_norm[e])
            w_out = hy_w_out[e]
        else:
            parts = gdn_heads(x2d, b, s, norm_mix_pre[layer], gdn_w_in[e], gdn_conv_w[e], gdn_dt_bias[e],
                              gdn_a_log[e], gdn_norm[e])
            w_out = gdn_w_out[e]
        x2d = mixer_out_ffn(x2d, parts, w_out.astype(BF16), norm_mix_post[layer], s, norm_ffn_pre[layer],
                            ffn_w_up[layer].astype(BF16), ffn_conv_w[layer], ffn_conv_b[layer],
                            ffn_w_down[layer].astype(BF16), norm_ffn_post[layer])
    return x2d.reshape(b, s, d)
```

```python
import functools
import math

import numpy as np
import jax
import jax.numpy as jnp
from jax import lax
from jax.experimental import pallas as pl
from jax.experimental.pallas import tpu as pltpu

F32 = jnp.float32
BF16 = jnp.bfloat16

D_MODEL = 1024
NORM_EPS = 1e-6
MASKED = -1e30

NSA_HEADS = 8
NSA_GROUPS = 2
NSA_REP = NSA_HEADS // NSA_GROUPS
NSA_DH = 64
CMP_BLOCK = 32
CMP_STRIDE = 16
SLC_BLOCK = 64
SLC_TOPK = 16
WINDOW = 512
ROPE_THETA = 500000.0
ROT_DIM = NSA_DH // 4
FORCED_SCORE = 1e9

SSD_HEADS = 8
SSD_P = 64
SSD_INNER = SSD_HEADS * SSD_P
SSD_GROUPS = 2
SSD_N = 128
SSD_CHUNK = 256
SSD_CONV_DIM = SSD_INNER + 2 * SSD_GROUPS * SSD_N

GDN_HEADS = 8
GDN_DK = 128
GDN_DV = 128
GDN_CHUNK = 64
GDN_QK_W = GDN_HEADS * GDN_DK
GDN_V_W = GDN_HEADS * GDN_DV
GDN_CONV_DIM = 2 * GDN_QK_W + GDN_V_W

FFN_DIM = 2816

LANE = 128
CONV_HALO = 8
FFN_HALO = 16
VMEM_LIMIT = 56 * 1024 * 1024
ROW_TILE = 512
FFN_ROW_TILE = 1024
PREP_TILE = 256


def _cparams(sem):
    return pltpu.CompilerParams(dimension_semantics=sem, vmem_limit_bytes=VMEM_LIMIT)


def _rms(x, g):
    return x * lax.rsqrt(jnp.mean(x * x, axis=-1, keepdims=True) + NORM_EPS) * g


def _silu(x):
    return x * jax.nn.sigmoid(x)


def _softplus(x):
    return jnp.maximum(x, 0.0) + jnp.log1p(jnp.exp(-jnp.abs(x)))


def _dot(a, b):
    return jnp.dot(a, b, preferred_element_type=F32)


def _dot_nt(a, b):
    return lax.dot_general(a, b, (((1,), (1,)), ((), ())), preferred_element_type=F32)


def _dot_tn(a, b):
    return lax.dot_general(a, b, (((0,), (0,)), ((), ())), preferred_element_type=F32)


def _split3(x):
    x1 = x.astype(BF16)
    r1 = x - x1.astype(F32)
    x2 = r1.astype(BF16)
    x3 = (r1 - x2.astype(F32)).astype(BF16)
    return x1, x2, x3


def _split2(x):
    x1 = x.astype(BF16)
    x2 = (x - x1.astype(F32)).astype(BF16)
    return x1, x2


def _cumsum_cols(tril, a):
    a1, a2, a3 = _split3(a)
    return _dot(tril, a1) + _dot(tril, a2) + _dot(tril, a3)


def _cumsum_rows_t(a, triu):
    a1, a2, a3 = _split3(a)
    return _dot_tn(a1, triu) + _dot_tn(a2, triu) + _dot_tn(a3, triu)


def _causal_conv(ext, cw, halo):
    width = cw.shape[0]
    y = ext[halo:] * cw[width - 1:width]
    for k in range(1, width):
        y = y + pltpu.roll(ext, k, 0)[halo:] * cw[width - 1 - k:width - k]
    return y


def _inproj_kernel(x_ref, g_ref, w_ref, *out_refs, splits):
    hn = _rms(x_ref[...], g_ref[...]).astype(BF16)
    y = _dot(hn, w_ref[...])
    off = 0
    for o_ref, n in zip(out_refs, splits):
        o_ref[...] = y[:, off:off + n].astype(o_ref.dtype)
        off += n


def norm_matmul(x2d, g, w, splits, tm=ROW_TILE):
    t = x2d.shape[0]
    n = w.shape[1]
    assert sum(splits) == n and t % tm == 0
    return pl.pallas_call(
        functools.partial(_inproj_kernel, splits=splits),
        grid=(t // tm,),
        in_specs=[pl.BlockSpec((tm, D_MODEL), lambda i: (i, 0)),
                  pl.BlockSpec((1, D_MODEL), lambda i: (0, 0)),
                  pl.BlockSpec((D_MODEL, n), lambda i: (0, 0))],
        out_specs=[pl.BlockSpec((tm, s), lambda i: (i, 0)) for s in splits],
        out_shape=[jax.ShapeDtypeStruct((t, s), F32) for s in splits],
        compiler_params=_cparams(("parallel",)),
        name="norm_matmul",
    )(x2d, g.reshape(1, D_MODEL), w)


FFN_CHUNK = 256


def _ffn_kernel(*refs, n_parts, tm, seq):
    x_ref, halo_ref = refs[:2]
    part_refs = refs[2:2 + n_parts]
    part_halo_refs = refs[2 + n_parts:2 + 2 * n_parts]
    (wo_ref, gmix_ref, gpre_ref, wup_ref, cw_ref, cb_ref, wd_ref, gpost_ref, o_ref, hn_ref) = refs[2 + 2 * n_parts:]

    def after_mixer(x, parts):
        acc = None
        off = 0
        for p_ref in parts:
            k = p_ref.shape[-1]
            part = _dot(p_ref[...], wo_ref[off:off + k, :])
            acc = part if acc is None else acc + part
            off += k
        return x + _rms(acc, gmix_ref[...])

    o_ref[...] = after_mixer(x_ref[...], part_refs)
    keep = ((pl.program_id(0) * tm) % seq != 0).astype(F32)
    hn_ref[:FFN_HALO, :] = (_rms(after_mixer(halo_ref[...], part_halo_refs), gpre_ref[...]) * keep).astype(BF16)
    hn_ref[FFN_HALO:, :] = _rms(o_ref[...], gpre_ref[...]).astype(BF16)
    hn = hn_ref[...]

    def up(lo):
        return _dot(hn, wup_ref[:, lo:lo + FFN_CHUNK]), _dot(hn, wup_ref[:, FFN_DIM + lo:FFN_DIM + lo + FFN_CHUNK])

    def conv(u, lo):
        return _causal_conv(u, cw_ref[:, lo:lo + FFN_CHUNK], FFN_HALO) + cb_ref[:, lo:lo + FFN_CHUNK]

    hidden = []
    n_chunks = FFN_DIM // FFN_CHUNK
    ahead = up(0)
    for c in range(n_chunks):
        lo = c * FFN_CHUNK
        u_gate, u_val = ahead
        if c + 1 < n_chunks:
            ahead = up(lo + FFN_CHUNK)
        hidden.append((_silu(conv(u_gate, lo)) * conv(u_val, FFN_DIM + lo)).astype(BF16))
    down = _dot(jnp.concatenate(hidden, axis=1), wd_ref[...])
    o_ref[...] = o_ref[...] + _rms(down, gpost_ref[...])


def mixer_out_ffn(x2d, parts, w_out, gmix, seq, gpre, w_up, conv_w, conv_b, w_down, gpost, tm=FFN_ROW_TILE):
    t = x2d.shape[0]
    assert FFN_DIM % FFN_CHUNK == 0 and t % tm == 0 and seq % tm == 0 and tm % FFN_HALO == 0
    hb = tm // FFN_HALO

    def resident(a):
        return pl.BlockSpec(a.shape, lambda i: (0,) * a.ndim, pipeline_mode=pl.Buffered(1))

    def halo(width):
        return pl.BlockSpec((FFN_HALO, width), lambda i: (jnp.maximum(i * hb - 1, 0), 0))

    consts = (w_out, gmix.reshape(1, D_MODEL), gpre.reshape(1, D_MODEL), w_up, conv_w,
              conv_b.reshape(1, 2 * FFN_DIM), w_down, gpost.reshape(1, D_MODEL))
    return pl.pallas_call(
        functools.partial(_ffn_kernel, n_parts=len(parts), tm=tm, seq=seq),
        grid=(t // tm,),
        in_specs=[pl.BlockSpec((tm, D_MODEL), lambda i: (i, 0)), halo(D_MODEL)]
        + [pl.BlockSpec((tm, p.shape[1]), lambda i: (i, 0)) for p in parts]
        + [halo(p.shape[1]) for p in parts]
        + [resident(a) for a in consts],
        out_specs=pl.BlockSpec((tm, D_MODEL), lambda i: (i, 0)),
        out_shape=jax.ShapeDtypeStruct((t, D_MODEL), F32),
        scratch_shapes=[pltpu.VMEM((tm + FFN_HALO, D_MODEL), BF16)],
        compiler_params=_cparams(("parallel",)),
        name="mixer_out_ffn",
    )(x2d, x2d, *parts, *parts, *consts)


def _nsa_prep_kernel(q_ref, kv_ref, pos_ref, freq_ref, ecos_ref, esin_ref, rest_ref,
                     qh_ref, kcmp_ref, vcmp_ref, kslc_ref, vslc_ref, kwin_ref, vwin_ref):
    ang = freq_ref[...] * pos_ref[0].astype(F32)

    def spread(t, e_ref):
        t1, t2, t3 = _split3(t)
        e = e_ref[...]
        return _dot_tn(t1, e) + _dot_tn(t2, e) + _dot_tn(t3, e)

    cs = spread(jnp.cos(ang), ecos_ref) + rest_ref[...]
    sn = spread(jnp.sin(ang), esin_ref)
    lane = lax.broadcasted_iota(jnp.int32, (1, LANE), 1) % NSA_DH
    first_half = lane < ROT_DIM // 2

    def rope(x):
        partner = jnp.where(first_half, pltpu.roll(x, LANE - ROT_DIM // 2, 1), pltpu.roll(x, ROT_DIM // 2, 1))
        return x * cs + partner * sn

    ts = q_ref.shape[1]
    aug_w = kslc_ref.shape[-1]
    low_half = lax.broadcasted_iota(jnp.int32, (1, LANE), 1) < NSA_DH

    def widen(x, upper, fill):
        low = jnp.where(low_half, pltpu.roll(x, NSA_DH, 1) if upper else x, fill[:, :LANE])
        return low if aug_w == LANE else jnp.concatenate([low, fill[:, LANE:]], axis=1)

    scale = NSA_DH ** -0.5 * math.log2(math.e)
    for j in range(NSA_HEADS // 2):
        t = (rope(q_ref[0, :, j * LANE:(j + 1) * LANE]) * scale).T
        qh_ref[0, 2 * j] = t[:NSA_DH].astype(BF16)
        qh_ref[0, 2 * j + 1] = t[NSA_DH:].astype(BF16)

    tok = pl.program_id(1) * ts + lax.broadcasted_iota(jnp.int32, (ts, 1), 0)
    block_onehot = jnp.where(lax.broadcasted_iota(jnp.int32, (1, aug_w), 1) - NSA_DH == tok // SLC_BLOCK, 1.0, 0.0)
    t = rope(kv_ref[0, :, 2 * LANE:3 * LANE])
    for g in range(NSA_GROUPS):
        kslc_ref[0, g] = widen(t, g == 1, block_onehot).astype(BF16)

    for i, o_ref in ((0, kcmp_ref), (4, kwin_ref), (1, vcmp_ref)):
        t = kv_ref[0, :, i * LANE:(i + 1) * LANE]
        if i % 2 == 0:
            t = rope(t)
        for g in range(NSA_GROUPS):
            o_ref[0, g] = t[:, g * NSA_DH:(g + 1) * NSA_DH].astype(o_ref.dtype)
    ones_row = jnp.where(lax.broadcasted_iota(jnp.int32, (NSA_VROWS - NSA_DH, ts), 0) == 0, 1.0, 0.0)
    for i, o_ref in ((3, vslc_ref), (5, vwin_ref)):
        t = kv_ref[0, :, i * LANE:(i + 1) * LANE].T
        for g in range(NSA_GROUPS):
            o_ref[0, g] = jnp.concatenate([t[g * NSA_DH:(g + 1) * NSA_DH, :], ones_row], axis=0).astype(o_ref.dtype)


NSA_VROWS = NSA_DH + 16


def _aug_width(seq):
    return -(-(NSA_DH + seq // SLC_BLOCK) // LANE) * LANE


def nsa_prep(q, kv, positions, ts=PREP_TILE):
    b, s, _ = q.shape
    half = ROT_DIM // 2
    freq_rows = 16
    inv_freq = ROPE_THETA ** (-jnp.arange(0, ROT_DIM, 2, dtype=F32) / ROT_DIM)
    freq = jnp.zeros((freq_rows, 1), F32).at[:half, 0].set(inv_freq)
    d = np.arange(LANE) % NSA_DH
    j = np.arange(freq_rows)[:, None]
    e_cos = ((d[None, :] < ROT_DIM) & (d[None, :] % half == j)).astype(np.float32)
    e_sin = e_cos * np.where(d[None, :] < half, -1.0, 1.0)
    rest = (d >= ROT_DIM).astype(np.float32).reshape(1, LANE)
    tables = (jnp.asarray(e_cos, dtype=BF16), jnp.asarray(e_sin, dtype=BF16), jnp.asarray(rest))
    tok_shape = (b, NSA_GROUPS, s, NSA_DH)
    tok_spec = pl.BlockSpec((1, NSA_GROUPS, ts, NSA_DH), lambda bi, i: (bi, 0, i, 0))
    feat_shape = (b, NSA_GROUPS, NSA_VROWS, s)
    feat_spec = pl.BlockSpec((1, NSA_GROUPS, NSA_VROWS, ts), lambda bi, i: (bi, 0, 0, i))
    aug_w = _aug_width(s)
    return pl.pallas_call(
        _nsa_prep_kernel,
        grid=(b, s // ts),
        in_specs=[pl.BlockSpec((1, ts, q.shape[2]), lambda bi, i: (bi, i, 0)),
                  pl.BlockSpec((1, ts, kv.shape[2]), lambda bi, i: (bi, i, 0)),
                  pl.BlockSpec((1, 1, ts), lambda bi, i: (bi, 0, i)),
                  pl.BlockSpec((freq_rows, 1), lambda bi, i: (0, 0))]
        + [pl.BlockSpec(t.shape, lambda bi, i: (0, 0)) for t in tables],
        out_specs=[pl.BlockSpec((1, NSA_HEADS, NSA_DH, ts), lambda bi, i: (bi, 0, 0, i)),
                   tok_spec, tok_spec,
                   pl.BlockSpec((1, NSA_GROUPS, ts, aug_w), lambda bi, i: (bi, 0, i, 0)),
                   feat_spec, tok_spec, feat_spec],
        out_shape=[jax.ShapeDtypeStruct((b, NSA_HEADS, NSA_DH, s), BF16),
                   jax.ShapeDtypeStruct(tok_shape, F32), jax.ShapeDtypeStruct(tok_shape, F32),
                   jax.ShapeDtypeStruct((b, NSA_GROUPS, s, aug_w), BF16), jax.ShapeDtypeStruct(feat_shape, BF16),
                   jax.ShapeDtypeStruct(tok_shape, BF16), jax.ShapeDtypeStruct(feat_shape, BF16)],
        compiler_params=_cparams(("parallel", "parallel")),
        name="nsa_prep",
    )(q, kv, positions.reshape(b, 1, s), freq, *tables)


def _compress_kernel(k_ref, v_ref, pek_ref, pev_ref, kw1_ref, vw1_ref, kb1_ref, vb1_ref, kw2_ref, vw2_ref,
                     kc_ref, vc_ref):
    half = CMP_STRIDE * NSA_DH

    def mlp(t_ref, pe_ref, w1_ref, b1_ref, w2_ref):
        a = t_ref[0, 0]
        n = a.shape[0]
        h_lo = _dot((a + pe_ref[:, :half]).astype(BF16), w1_ref[:half, :])
        h_hi = _dot((a + pe_ref[:, half:]).astype(BF16), w1_ref[half:, :])
        hid = _silu(h_lo + pltpu.roll(h_hi, n - 1, 0) + b1_ref[...])
        return _dot(hid.astype(BF16), w2_ref[...])

    kc_ref[0, 0] = mlp(k_ref, pek_ref, kw1_ref, kb1_ref, kw2_ref).astype(kc_ref.dtype)
    vc_ref[0, 0] = mlp(v_ref, pev_ref, vw1_ref, vb1_ref, vw2_ref).astype(vc_ref.dtype).T


def nsa_compress(kcmp, vcmp, pe_k, pe_v, ck_w1, ck_b1, ck_w2, cv_w1, cv_b1, cv_w2):
    b, g, s, dh = kcmp.shape
    n = s // CMP_STRIDE
    wide = CMP_STRIDE * dh
    assert CMP_BLOCK == 2 * CMP_STRIDE
    kr = kcmp.reshape(b, g, n, wide)
    vr = vcmp.reshape(b, g, n, wide)
    blk = pl.BlockSpec((1, 1, n, wide), lambda bi, gi: (bi, gi, 0, 0))

    def full(a):
        return pl.BlockSpec(a.shape, lambda bi, gi: (0,) * a.ndim)

    args = (pe_k.reshape(1, 2 * wide), pe_v.reshape(1, 2 * wide), ck_w1.astype(BF16), cv_w1.astype(BF16),
            ck_b1.reshape(1, dh), cv_b1.reshape(1, dh), ck_w2.astype(BF16), cv_w2.astype(BF16))
    return pl.pallas_call(
        _compress_kernel,
        grid=(b, g),
        in_specs=[blk, blk] + [full(a) for a in args],
        out_specs=[pl.BlockSpec((1, 1, n, dh), lambda bi, gi: (bi, gi, 0, 0)),
                   pl.BlockSpec((1, 1, dh, n), lambda bi, gi: (bi, gi, 0, 0))],
        out_shape=[jax.ShapeDtypeStruct((b, g, n, dh), BF16), jax.ShapeDtypeStruct((b, g, dh, n), BF16)],
        compiler_params=_cparams(("parallel", "parallel")),
        name="nsa_compress",
    )(kr, vr, *args)


ATT_TQ = 256
ATT_TK = 256
ATT_RING = 4
SEL_SIZES = 4


def _att_kernel(q_ref, kc_ref, vc_ref, ks_ref, vs_ref, kw_ref, vw_ref, gate_ref, ov_ref, o_ref,
                *s_refs, seq):
    tq, tk, rep, dh = ATT_TQ, ATT_TK, NSA_REP, NSA_DH
    cols = rep * tq
    nblk = seq // SLC_BLOCK
    ncmp = kc_ref.shape[2]
    aug_w = ks_ref.shape[-1]
    t0 = pl.program_id(2) * tq
    q = jnp.concatenate([q_ref[0, r] for r in range(rep)], axis=1)
    tpos = t0 + lax.broadcasted_iota(jnp.int32, (1, tq), 1)

    def per_head(x):
        return jnp.concatenate([x] * rep, axis=1)


    band = WINDOW + tq
    start = pl.multiple_of(jnp.maximum(t0 - WINDOW, 0), tq)
    dlt = tpos - (start + lax.broadcasted_iota(jnp.int32, (band, 1), 0))
    bias_w = jnp.where((dlt >= 0) & (dlt < WINDOW), 0.0, MASKED)
    s_w = _dot(kw_ref[0, 0, pl.ds(start, band), :], q) + per_head(bias_w)

    cmp_end = lax.broadcasted_iota(jnp.int32, (ncmp, 1), 0) * CMP_STRIDE + (CMP_BLOCK - 1)
    s_c = _dot(kc_ref[0, 0], q) + per_head(jnp.where(cmp_end <= tpos, 0.0, MASKED))
    p_c = jnp.exp2(s_c - jnp.max(s_c, axis=0, keepdims=True))
    l_c = jnp.sum(p_c, axis=0, keepdims=True)
    p_c = p_c * (per_head(jnp.where(tpos >= CMP_BLOCK - 1, 1.0, 0.0)) / l_c)
    o_c = _dot(vc_ref[0, 0], p_c.astype(BF16))

    p_sum = p_c[:, :tq]
    for r in range(1, rep):
        p_sum = p_sum + p_c[:, r * tq:(r + 1) * tq]
    p_hi, p_lo = _split2(p_sum)
    importance = _dot(ov_ref[...], p_hi) + _dot(ov_ref[...], p_lo)
    cur = tpos // SLC_BLOCK

    def select(n):
        def run():
            blk = lax.broadcasted_iota(jnp.int32, (n, 1), 0)
            forced = (blk == 0) | (blk == cur) | (blk == cur - 1)
            imp = jnp.where(forced, FORCED_SCORE, jnp.where(blk <= cur, importance[:n], -1.0))
            sel = jnp.full((n, tq), MASKED, F32)
            for _ in range(min(SLC_TOPK, nblk)):
                best = jnp.max(imp, axis=0, keepdims=True)
                first = jnp.min(jnp.where(imp == best, blk, n), axis=0, keepdims=True)
                hit = blk == first
                sel = jnp.where(hit, 0.0, sel)
                imp = jnp.where(hit, -jnp.inf, imp)
            return sel if n == nblk else jnp.concatenate([sel, jnp.full((nblk - n, tq), MASKED, F32)], axis=0)
        return run

    blk_step = nblk // SEL_SIZES
    size_class = ((t0 + tq - 1) // SLC_BLOCK) // blk_step
    pick = select(nblk)
    for k in reversed(range(SEL_SIZES - 1)):
        pick = functools.partial(lax.cond, size_class == k, select((k + 1) * blk_step), pick)
    sel = pick()

    mask_rows = jnp.concatenate([sel.astype(BF16), jnp.zeros((aug_w - dh - nblk, tq), BF16)], axis=0)
    q_aug = jnp.concatenate([q, per_head(mask_rows)], axis=0)
    kidx = lax.broadcasted_iota(jnp.int32, (tk, 1), 0)

    def produce(kt, s_ref):
        k0 = pl.multiple_of(kt * tk, tk)
        s = _dot(ks_ref[0, 0, pl.ds(k0, tk), :], q_aug)
        s_ref[...] = s
        return jnp.max(s, axis=0, keepdims=True)

    def consume(kt, s_ref, tile_max, carry, causal):
        m, acc = carry
        k0 = pl.multiple_of(kt * tk, tk)
        v = vs_ref[0, 0, :, pl.ds(k0, tk)]
        s = s_ref[...]
        if causal:
            s = s + per_head(jnp.where(k0 + kidx <= tpos, 0.0, MASKED))
            tile_max = jnp.max(s, axis=0, keepdims=True)
        m_new = jnp.maximum(m, tile_max)
        alpha = jnp.exp2(m - m_new)
        p = jnp.exp2(s - m_new)
        return m_new, alpha * acc + _dot(v, p.astype(BF16))

    nbuf = len(s_refs)

    def slc_trip(j, carry):
        maxes, state = list(carry[:nbuf]), carry[nbuf:]
        for i, s_ref in enumerate(s_refs):
            state = consume(nbuf * j + i, s_ref, maxes[i], state, False)
            maxes[i] = produce(nbuf * (j + 1) + i, s_ref)
        return tuple(maxes) + state

    n_full = t0 // (nbuf * tk)
    init = tuple(produce(i, s_ref) for i, s_ref in enumerate(s_refs)) + (
        jnp.full((1, cols), MASKED, F32), jnp.zeros((NSA_VROWS, cols), F32))

    p_w = jnp.exp2(s_w - jnp.max(s_w, axis=0, keepdims=True))
    o_w = _dot(vw_ref[0, 0, :, pl.ds(start, band)], p_w.astype(BF16))
    o_w = o_w[:dh] / o_w[dh:dh + 1]

    carry = lax.fori_loop(0, n_full, slc_trip, init)
    maxes, state = carry[:nbuf], carry[nbuf:]
    own = (t0 - n_full * nbuf * tk) // tk

    def last_trip(n_before):
        def run(st):
            for i in range(n_before):
                st = consume(nbuf * n_full + i, s_refs[i], maxes[i], st, False)
            return consume(nbuf * n_full + n_before, s_refs[n_before], None, st, True)
        return run

    finish = last_trip(nbuf - 1)
    for n_before in reversed(range(nbuf - 1)):
        finish = functools.partial(lax.cond, own == n_before, last_trip(n_before), finish)
    acc_s = finish(state)[1]
    o_s = acc_s[:dh] / acc_s[dh:dh + 1]

    gates = jax.nn.sigmoid(gate_ref[0]).T
    merged = []
    for r in range(rep):
        sl = slice(r * tq, (r + 1) * tq)
        merged.append(gates[3 * r:3 * r + 1] * o_c[:, sl] + gates[3 * r + 1:3 * r + 2] * o_s[:, sl]
                      + gates[3 * r + 2:3 * r + 3] * o_w[:, sl])
    o_ref[0] = jnp.concatenate(merged, axis=0).T.astype(o_ref.dtype)


def _overlap_matrix(seq):
    ncp = seq // CMP_STRIDE
    n_cmp = (seq - CMP_BLOCK) // CMP_STRIDE + 1
    nblk = seq // SLC_BLOCK
    cs = np.arange(ncp) * CMP_STRIDE
    ss = np.arange(nblk) * SLC_BLOCK
    ov = ((cs[None, :] <= ss[:, None] + SLC_BLOCK - 1) & (cs[None, :] + CMP_BLOCK - 1 >= ss[:, None])
          & (np.arange(ncp)[None, :] < n_cmp))
    return jnp.asarray(ov.astype(np.float32), dtype=BF16)


def nsa_attention(qh, kc, vc, kslc, vslc, kwin, vwin, gates):
    b, _, dh, s = qh.shape
    aug_w = kslc.shape[-1]
    g = NSA_GROUPS
    ncp = kc.shape[2]
    nblk = s // SLC_BLOCK
    assert s % (ATT_RING * ATT_TK) == 0 and s >= WINDOW + ATT_TQ and aug_w == _aug_width(s)
    ov = _overlap_matrix(s)
    tok_spec = pl.BlockSpec((1, 1, s, dh), lambda bi, gi, i: (bi, gi, 0, 0))
    feat_spec = pl.BlockSpec((1, 1, NSA_VROWS, s), lambda bi, gi, i: (bi, gi, 0, 0))
    return pl.pallas_call(
        functools.partial(_att_kernel, seq=s),
        grid=(b, g, s // ATT_TQ),
        in_specs=[pl.BlockSpec((1, NSA_REP, dh, ATT_TQ), lambda bi, gi, i: (bi, gi, 0, i)),
                  pl.BlockSpec((1, 1, ncp, dh), lambda bi, gi, i: (bi, gi, 0, 0)),
                  pl.BlockSpec((1, 1, dh, ncp), lambda bi, gi, i: (bi, gi, 0, 0)),
                  pl.BlockSpec((1, 1, s, aug_w), lambda bi, gi, i: (bi, gi, 0, 0)),
                  feat_spec, tok_spec, feat_spec,
                  pl.BlockSpec((1, ATT_TQ, LANE), lambda bi, gi, i: (bi, i, gi)),
                  pl.BlockSpec((nblk, ncp), lambda bi, gi, i: (0, 0))],
        out_specs=pl.BlockSpec((1, ATT_TQ, NSA_REP * dh), lambda bi, gi, i: (bi, i, gi)),
        out_shape=jax.ShapeDtypeStruct((b, s, NSA_HEADS * dh), BF16),
        scratch_shapes=[pltpu.VMEM((ATT_TK, NSA_REP * ATT_TQ), F32)] * ATT_RING,
        compiler_params=_cparams(("parallel", "parallel", "arbitrary")),
        name="nsa_attention",
    )(qh, kc, vc, kslc, vslc, kwin, vwin, gates, ov)


def _ssd_kernel(z_ref, xbc_ref, dt_ref, cw_ref, cb_ref, dtb_ref, alog_ref, drow_ref, ng_ref, tril_ref, triu_ref,
                o_ref, carry_ref, state_ref, y_ref):
    cl = SSD_CHUNK
    hpg = SSD_HEADS // SSD_GROUPS

    @pl.when(pl.program_id(1) == 0)
    def _():
        carry_ref[...] = jnp.zeros_like(carry_ref)
        state_ref[...] = jnp.zeros_like(state_ref)

    xbc = xbc_ref[0]
    ext = jnp.concatenate([carry_ref[...], xbc], axis=0)
    act = _silu(_causal_conv(ext, cw_ref[...], CONV_HALO) + cb_ref[...])
    carry_ref[...] = xbc[cl - CONV_HALO:, :]

    xs = act[:, :SSD_INNER]
    dt = _softplus(dt_ref[0] + dtb_ref[...])
    a = dt * (-jnp.exp(alog_ref[...]))
    acs_c = _cumsum_cols(tril_ref[...], a)
    acs_r = _cumsum_rows_t(a, triu_ref[...])
    causal = lax.broadcasted_iota(jnp.int32, (cl, cl), 0) >= lax.broadcasted_iota(jnp.int32, (cl, cl), 1)

    for g in range(SSD_GROUPS):
        bg = act[:, SSD_INNER + g * SSD_N:SSD_INNER + (g + 1) * SSD_N].astype(BF16)
        cg = act[:, SSD_INNER + (SSD_GROUPS + g) * SSD_N:SSD_INNER + (SSD_GROUPS + g + 1) * SSD_N].astype(BF16)
        cb = _dot_nt(cg, bg)
        state = state_ref[g]
        y_off = _dot(cg, state.astype(BF16))
        weighted = []
        decay = []
        for hl in range(hpg):
            h = g * hpg + hl
            col = acs_c[:, h:h + 1]
            row = acs_r[h:h + 1, :]
            lmat = jnp.exp(jnp.where(causal, col - row, MASKED))
            xh = xs[:, h * SSD_P:(h + 1) * SSD_P]
            xdt = xh * dt[:, h:h + 1]
            y_diag = _dot((cb * lmat).astype(BF16), xdt.astype(BF16))
            y_ref[:, h * SSD_P:(h + 1) * SSD_P] = (y_diag + y_off[:, hl * SSD_P:(hl + 1) * SSD_P] * jnp.exp(col)
                                                   + drow_ref[:, h * SSD_P:(h + 1) * SSD_P] * xh)
            a_last = acs_c[cl - 1:cl, h:h + 1]
            weighted.append((xdt * jnp.exp(a_last - col)).astype(BF16))
            decay.append(jnp.broadcast_to(jnp.exp(a_last), (1, SSD_P)))
        contrib = _dot_tn(bg, jnp.concatenate(weighted, axis=1))
        state_ref[g] = state * jnp.concatenate(decay, axis=1) + contrib

    y = y_ref[...] * _silu(z_ref[0])
    gw = SSD_INNER // SSD_GROUPS
    for g in range(SSD_GROUPS):
        o_ref[0, :, g * gw:(g + 1) * gw] = _rms(y[:, g * gw:(g + 1) * gw],
                                                 ng_ref[:, g * gw:(g + 1) * gw]).astype(o_ref.dtype)


def _tri(n, upper):
    m = np.triu(np.ones((n, n), np.float32)) if upper else np.tril(np.ones((n, n), np.float32))
    return jnp.asarray(m, dtype=BF16)


def _lane_row(v, offset=0):
    row = jnp.zeros((1, LANE), F32)
    return row.at[0, offset:offset + v.shape[0]].set(v)


def ssd_mixer(z, xbc, dts, conv_w, conv_b, dt_bias, a_log, d_skip, norm_g):
    b, s, _ = z.shape
    cl = SSD_CHUNK
    hpg = SSD_HEADS // SSD_GROUPS

    def full(a):
        return pl.BlockSpec(a.shape, lambda bi, c: (0,) * a.ndim)

    args = (conv_w, conv_b.reshape(1, -1), _lane_row(dt_bias), _lane_row(a_log),
            jnp.repeat(d_skip, SSD_P).reshape(1, SSD_INNER), norm_g.reshape(1, SSD_INNER),
            _tri(cl, False), _tri(cl, True))
    return pl.pallas_call(
        _ssd_kernel,
        grid=(b, s // cl),
        in_specs=[pl.BlockSpec((1, cl, SSD_INNER), lambda bi, c: (bi, c, 0)),
                  pl.BlockSpec((1, cl, SSD_CONV_DIM), lambda bi, c: (bi, c, 0)),
                  pl.BlockSpec((1, cl, LANE), lambda bi, c: (bi, c, 0))] + [full(a) for a in args],
        out_specs=pl.BlockSpec((1, cl, SSD_INNER), lambda bi, c: (bi, c, 0)),
        out_shape=jax.ShapeDtypeStruct((b, s, SSD_INNER), BF16),
        scratch_shapes=[pltpu.VMEM((CONV_HALO, SSD_CONV_DIM), F32),
                        pltpu.VMEM((SSD_GROUPS, SSD_N, hpg * SSD_P), F32),
                        pltpu.VMEM((cl, SSD_INNER), F32)],
        compiler_params=_cparams(("parallel", "arbitrary")),
        name="ssd_mixer",
    )(z, xbc, dts, *args)


GDN_STEP_CHUNKS = 4


def _pair_diag(x):
    cl = x.shape[0]
    left = jnp.where(lax.broadcasted_iota(jnp.int32, x.shape, 1) < cl, 1.0, 0.0).astype(BF16)
    return jnp.concatenate([x * left, x * (1.0 - left).astype(BF16)], axis=0)


def _mm_pairs(a, b_diag):
    return _dot(a[0], b_diag[0]) + _dot(a[0], b_diag[1]) + _dot(a[1], b_diag[0])


def _unit_lower_inverses(a_list, row, col, fillers=()):
    eye = jnp.where(row == col, 1.0, 0.0)
    same16 = (row // 16) == (col // 16)
    same32 = (row // 32) == (col // 32)
    pending = list(fillers)

    def fill(anchor):
        if pending:
            pending.pop(0)(anchor[:1, :1] * 0.0)

    def diag_parts(x):
        hi, lo = _split2(x)
        return (hi, lo), (_pair_diag(hi), _pair_diag(lo))

    n1 = [jnp.where(same16, -a, 0.0) for a in a_list]
    p = [eye + x for x in n1]
    n = [diag_parts(x) for x in n1]
    anchor = p[-1]
    for _ in range(3):
        squares = [_mm_pairs(x, xd) for x, xd in n]
        fill(anchor)
        anchor = squares[-1]
        n = [diag_parts(x) for x in squares]
        updates = [_mm_pairs(_split2(x), yd) for x, (_, yd) in zip(p, n)]
        fill(anchor)
        p = [x + y for x, y in zip(p, updates)]
        anchor = p[-1]
    for level_mask in (jnp.where(same16, 0.0, jnp.where(same32, 1.0, 0.0)), jnp.where(same32, 0.0, 1.0)):
        off = [diag_parts(a * level_mask)[1] for a in a_list]
        ps = [diag_parts(x) for x in p]
        t = [_mm_pairs(x, y) for (x, _), y in zip(ps, off)]
        fill(anchor)
        anchor = t[-1]
        updates = [_mm_pairs(_split2(x), zd) for x, (_, zd) in zip(t, ps)]
        fill(anchor)
        p = [x - y for x, y in zip(p, updates)]
        anchor = p[-1]
    while pending:
        fill(anchor)
    return p


def _gdn_kernel(qkv_ref, z_ref, sm_ref, cw_ref, dtb_ref, alog_ref, ng_ref, tril_ref, triu_ref,
                o_ref, carry_ref, state_ref):
    cl = GDN_CHUNK
    rows = qkv_ref.shape[1]
    heads = range(GDN_HEADS)
    half_heads = GDN_HEADS // 2
    chunks = range(rows // cl)
    pairs = [(c, j) for c in chunks for j in range(half_heads)]
    chains = [(c, h) for c in chunks for h in heads]

    @pl.when(pl.program_id(1) == 0)
    def _():
        carry_ref[...] = jnp.zeros_like(carry_ref)
        state_ref[...] = jnp.zeros_like(state_ref)

    qkv = qkv_ref[0]
    ext = jnp.concatenate([carry_ref[...], qkv], axis=0)
    carry_ref[...] = qkv[rows - CONV_HALO:, :]

    def conv_silu(lo, width, tie=0.0):
        return _silu(_causal_conv(ext[:, lo:lo + width] + tie, cw_ref[:, lo:lo + width], CONV_HALO))

    sm = sm_ref[0]
    beta_all = jax.nn.sigmoid(sm)
    g_all = -jnp.exp(alog_ref[...]) * _softplus(sm + dtb_ref[...])
    row = lax.broadcasted_iota(jnp.int32, (cl, 2 * cl), 0)
    lane = lax.broadcasted_iota(jnp.int32, (cl, 2 * cl), 1)
    first = lane < cl
    col = jnp.where(first, lane, lane - cl)
    incl = row >= col

    def l2n(x):
        return x * lax.rsqrt(jnp.sum(x * x, axis=-1, keepdims=True) + NORM_EPS)

    def chunk(x, c):
        return x[c * cl:(c + 1) * cl]

    act_k = conv_silu(GDN_QK_W, GDN_QK_W)
    kn = [l2n(act_k[:, h * GDN_DK:(h + 1) * GDN_DK]) for h in heads]
    beta = [beta_all[:, h:h + 1] for h in heads]
    kb = [kn[h] * beta[h] for h in heads]
    knb = [x.astype(BF16) for x in kn]
    gcs_c = [_cumsum_cols(tril_ref[...], chunk(g_all, c)) for c in chunks]
    gcs_r = [_cumsum_rows_t(chunk(g_all, c), triu_ref[...]) for c in chunks]
    gc = {(c, h): gcs_c[c][:, GDN_HEADS + h:GDN_HEADS + h + 1] for c, h in chains}
    eg = {k: jnp.exp(g) for k, g in gc.items()}

    def decay(c, j):
        g_col = jnp.where(first, gc[c, 2 * j], gc[c, 2 * j + 1])
        g_row = jnp.concatenate([gcs_r[c][GDN_HEADS + 2 * j + i:GDN_HEADS + 2 * j + i + 1, :] for i in range(2)],
                                axis=1)
        return jnp.exp(jnp.where(incl, g_col - g_row, MASKED))

    dec = {k: decay(*k) for k in pairs}
    zeros_k = jnp.zeros((cl, GDN_DK), BF16)

    def against_keys(x, c, j):
        k0, k1 = chunk(knb[2 * j], c), chunk(knb[2 * j + 1], c)
        return (_dot_nt(chunk(x[2 * j], c).astype(BF16), jnp.concatenate([k0, zeros_k], axis=0))
                + _dot_nt(chunk(x[2 * j + 1], c).astype(BF16), jnp.concatenate([zeros_k, k1], axis=0)))

    a_kk = [jnp.where(row > col, against_keys(kb, c, j) * dec[c, j], 0.0) for c, j in pairs]

    qn, v, a_qk, rhs, q_decayed, k_decayed, z_gate = {}, {}, {}, {}, {}, {}, {}

    def fill_queries(j):
        def run(tie):
            act_q = conv_silu(2 * j * GDN_DK, 2 * GDN_DK, tie)
            for i in range(2):
                qn[2 * j + i] = l2n(act_q[:, i * GDN_DK:(i + 1) * GDN_DK]) * GDN_DK ** -0.5
        return run

    def fill_values(j):
        def run(tie):
            act_v = conv_silu(2 * GDN_QK_W + 2 * j * GDN_DV, 2 * GDN_DV, tie)
            for i in range(2):
                v[2 * j + i] = act_v[:, i * GDN_DV:(i + 1) * GDN_DV]
        return run

    def fill_intra(tie):
        for c, j in pairs:
            a_qk[c, j] = (against_keys(qn, c, j) * (dec[c, j] + tie)).astype(BF16)

    def fill_rhs(tie):
        for c, h in chains:
            rhs[c, h] = jnp.concatenate([chunk(v[h], c) * (chunk(beta[h], c) + tie),
                                         chunk(kb[h], c) * (eg[c, h] + tie)], axis=1).astype(BF16)

    def fill_recurrence_operands(tie):
        for c, h in chains:
            q_decayed[c, h] = chunk(qn[h], c) * (eg[c, h] + tie)
            k_decayed[c, h] = (chunk(kn[h], c) * jnp.exp(gc[c, h][cl - 1:cl, :] - gc[c, h] + tie)).astype(BF16)

    def fill_gate(tie):
        z_gate[0] = _silu(z_ref[0] + tie)

    fillers = ([fill_queries(j) for j in range(half_heads)] + [fill_values(j) for j in range(half_heads)]
               + [fill_intra, fill_rhs, fill_recurrence_operands, fill_gate])
    t_inv = [t.astype(BF16) for t in _unit_lower_inverses(a_kk, row, col, fillers)]

    def stacked(x0, x1):
        z = jnp.zeros_like(x0)
        return jnp.concatenate([jnp.concatenate([x0, z], axis=1), jnp.concatenate([z, x1], axis=1)], axis=0)

    uw = {(c, j): _dot(t, stacked(rhs[c, 2 * j], rhs[c, 2 * j + 1])) for (c, j), t in zip(pairs, t_inv)}
    width = GDN_DV + GDN_DK

    state = [state_ref[h] for h in heads]
    for c in chunks:
        sb = [s.astype(BF16) for s in state]
        u = [uw[c, h // 2][:, (h % 2) * width:(h % 2) * width + GDN_DV] for h in heads]
        w = [uw[c, h // 2][:, (h % 2) * width + GDN_DV:(h % 2 + 1) * width] for h in heads]
        ws = [_dot(jnp.concatenate([w[h], q_decayed[c, h]], axis=0).astype(BF16), sb[h]) for h in heads]
        vb = [(u[h] - ws[h][:cl]).astype(BF16) for h in heads]
        intra = [_dot(a_qk[c, j], stacked(vb[2 * j], vb[2 * j + 1])) for j in range(half_heads)]
        o = [ws[h][cl:] + intra[h // 2][:, (h % 2) * GDN_DV:(h % 2 + 1) * GDN_DV] for h in heads]
        state = [state[h] * jnp.exp(gc[c, h][cl - 1:cl, :]) + _dot_tn(k_decayed[c, h], vb[h]) for h in heads]
        for h in heads:
            out = _rms(o[h], ng_ref[...]) * z_gate[0][c * cl:(c + 1) * cl, h * GDN_DV:(h + 1) * GDN_DV]
            o_ref[0, c * cl:(c + 1) * cl, h * GDN_DV:(h + 1) * GDN_DV] = out.astype(o_ref.dtype)
    for h in heads:
        state_ref[h] = state[h]


def gdn_mixer(qkv, z, sm, conv_w, dt_bias, a_log, norm_g):
    b, s, _ = qkv.shape
    cl = GDN_CHUNK

    def full(a):
        return pl.BlockSpec(a.shape, lambda bi, c: (0,) * a.ndim)

    args = (conv_w, _lane_row(dt_bias, GDN_HEADS), _lane_row(a_log, GDN_HEADS), norm_g.reshape(1, GDN_DV),
            _tri(cl, False), _tri(cl, True))
    rows = GDN_STEP_CHUNKS * cl
    assert s % rows == 0
    return pl.pallas_call(
        _gdn_kernel,
        grid=(b, s // rows),
        in_specs=[pl.BlockSpec((1, rows, GDN_CONV_DIM), lambda bi, c: (bi, c, 0)),
                  pl.BlockSpec((1, rows, GDN_V_W), lambda bi, c: (bi, c, 0)),
                  pl.BlockSpec((1, rows, LANE), lambda bi, c: (bi, c, 0))] + [full(a) for a in args],
        out_specs=pl.BlockSpec((1, rows, GDN_V_W), lambda bi, c: (bi, c, 0)),
        out_shape=jax.ShapeDtypeStruct((b, s, GDN_V_W), BF16),
        scratch_shapes=[pltpu.VMEM((CONV_HALO, GDN_CONV_DIM), F32),
                        pltpu.VMEM((GDN_HEADS, GDN_DK, GDN_DV), F32)],
        compiler_params=_cparams(("parallel", "arbitrary")),
        name="gdn_mixer",
    )(qkv, z, sm, *args)


def _pad_cols(w, width=LANE):
    return jnp.pad(w, ((0, 0), (0, width - w.shape[1])))


HY_Q_W = NSA_HEADS * NSA_DH
HY_KV_W = 6 * NSA_GROUPS * NSA_DH
HY_GATE_W = NSA_HEADS * 3


def hybrid_heads(x2d, b, s, positions, g_pre, w_in, pe_k, pe_v, ck_w1, ck_b1, ck_w2,
                 cv_w1, cv_b1, cv_w2, conv_w, conv_b, dt_bias, a_log, d_skip, norm_g):
    o = 0
    cols = {}
    for name, width in (("q", HY_Q_W), ("kv", HY_KV_W), ("gate", HY_GATE_W), ("z", SSD_INNER),
                        ("xbc", SSD_CONV_DIM), ("dt", SSD_HEADS)):
        cols[name] = w_in[:, o:o + width]
        o += width
    gpg = HY_GATE_W // NSA_GROUPS
    w_cat = jnp.concatenate(
        [cols["q"], cols["kv"], cols["z"], cols["xbc"]]
        + [_pad_cols(cols["gate"][:, g * gpg:(g + 1) * gpg]) for g in range(NSA_GROUPS)]
        + [_pad_cols(cols["dt"])], axis=1).astype(BF16)
    splits = (HY_Q_W, HY_KV_W, SSD_INNER, SSD_CONV_DIM, NSA_GROUPS * LANE, LANE)
    q, kv, z, xbc, gates, dts = norm_matmul(x2d, g_pre, w_cat, splits)

    qh, kcmp, vcmp, kslc, vslc, kwin, vwin = nsa_prep(q.reshape(b, s, -1), kv.reshape(b, s, -1), positions)
    kc, vc = nsa_compress(kcmp, vcmp, pe_k, pe_v, ck_w1, ck_b1, ck_w2, cv_w1, cv_b1, cv_w2)
    o_nsa = nsa_attention(qh, kc, vc, kslc, vslc, kwin, vwin, gates.reshape(b, s, -1))
    y = ssd_mixer(z.reshape(b, s, -1), xbc.reshape(b, s, -1), dts.reshape(b, s, -1),
                  conv_w, conv_b, dt_bias, a_log, d_skip, norm_g)
    return [o_nsa.reshape(b * s, -1), y.reshape(b * s, -1)]


def gdn_heads(x2d, b, s, g_pre, w_in, conv_w, dt_bias, a_log, norm_g):
    w_cat = jnp.concatenate([w_in[:, :GDN_CONV_DIM + GDN_V_W], _pad_cols(w_in[:, GDN_CONV_DIM + GDN_V_W:])],
                            axis=1).astype(BF16)
    qkv, z, sm = norm_matmul(x2d, g_pre, w_cat, (GDN_CONV_DIM, GDN_V_W, LANE))
    o = gdn_mixer(qkv.reshape(b, s, -1), z.reshape(b, s, -1), sm.reshape(b, s, -1), conv_w, dt_bias, a_log, norm_g)
    return [o.reshape(b * s, -1)]


def kernel(x, positions, norm_mix_pre, norm_mix_post, norm_ffn_pre, norm_ffn_post, hy_w_in, hy_w_out, nsa_pe_k, nsa_pe_v, nsa_ck_w1, nsa_ck_b1, nsa_ck_w2, nsa_cv_w1, nsa_cv_b1, nsa_cv_w2, ssd_conv_w, ssd_conv_b, ssd_dt_bias, ssd_a_log, ssd_d, ssd_norm, gdn_w_in, gdn_conv_w, gdn_dt_bias, gdn_a_log, gdn_norm, gdn_w_out, ffn_w_up, ffn_conv_w, ffn_conv_b, ffn_w_down):
    b, s, d = x.shape
    x2d = x.reshape(b * s, d)
    depth = norm_mix_pre.shape[0]
    for layer in range(depth):
        e = layer // 2
        if layer % 2 == 0:
            parts = hybrid_heads(x2d, b, s, positions, norm_mix_pre[layer], hy_w_in[e], nsa_pe_k[e], nsa_pe_v[e],
                                 nsa_ck_w1[e], nsa_ck_b1[e], nsa_ck_w2[e], nsa_cv_w1[e], nsa_cv_b1[e], nsa_cv_w2[e],
                                 ssd_conv_w[e], ssd_conv_b[e], ssd_dt_bias[e], ssd_a_log[e], ssd_d[e], ssd_norm[e])
            w_out = hy_w_out[e]
        else:
            parts = gdn_heads(x2d, b, s, norm_mix_pre[layer], gdn_w_in[e], gdn_conv_w[e], gdn_dt_bias[e],
                              gdn_a_log[e], gdn_norm[e])
            w_out = gdn_w_out[e]
        x2d = mixer_out_ffn(x2d, parts, w_out.astype(BF16), norm_mix_post[layer], s, norm_ffn_pre[layer],
                            ffn_w_up[layer].astype(BF16), ffn_conv_w[layer], ffn_conv_b[layer],
                            ffn_w_down[layer].astype(BF16), norm_ffn_post[layer])
    return x2d.reshape(b, s, d)
```

```python
import functools
import math

import numpy as np
import jax
import jax.numpy as jnp
from jax import lax
from jax.experimental import pallas as pl
from jax.experimental.pallas import tpu as pltpu

F32 = jnp.float32
BF16 = jnp.bfloat16

D_MODEL = 1024
NORM_EPS = 1e-6
MASKED = -1e30

NSA_HEADS = 8
NSA_GROUPS = 2
NSA_REP = NSA_HEADS // NSA_GROUPS
NSA_DH = 64
CMP_BLOCK = 32
CMP_STRIDE = 16
SLC_BLOCK = 64
SLC_TOPK = 16
WINDOW = 512
ROPE_THETA = 500000.0
ROT_DIM = NSA_DH // 4
FORCED_SCORE = 1e9

SSD_HEADS = 8
SSD_P = 64
SSD_INNER = SSD_HEADS * SSD_P
SSD_GROUPS = 2
SSD_N = 128
SSD_CHUNK = 256
SSD_CONV_DIM = SSD_INNER + 2 * SSD_GROUPS * SSD_N

GDN_HEADS = 8
GDN_DK = 128
GDN_DV = 128
GDN_CHUNK = 64
GDN_QK_W = GDN_HEADS * GDN_DK
GDN_V_W = GDN_HEADS * GDN_DV
GDN_CONV_DIM = 2 * GDN_QK_W + GDN_V_W

FFN_DIM = 2816

LANE = 128
CONV_HALO = 8
FFN_HALO = 16
VMEM_LIMIT = 56 * 1024 * 1024
ROW_TILE = 512
FFN_ROW_TILE = 1024
PREP_TILE = 512


def _cparams(sem):
    return pltpu.CompilerParams(dimension_semantics=sem, vmem_limit_bytes=VMEM_LIMIT)


def _rms(x, g):
    return x * lax.rsqrt(jnp.mean(x * x, axis=-1, keepdims=True) + NORM_EPS) * g


def _silu(x):
    return x * jax.nn.sigmoid(x)


def _softplus(x):
    return jnp.maximum(x, 0.0) + jnp.log1p(jnp.exp(-jnp.abs(x)))


def _dot(a, b):
    return jnp.dot(a, b, preferred_element_type=F32)


def _dot_nt(a, b):
    return lax.dot_general(a, b, (((1,), (1,)), ((), ())), preferred_element_type=F32)


def _dot_tn(a, b):
    return lax.dot_general(a, b, (((0,), (0,)), ((), ())), preferred_element_type=F32)


def _split3(x):
    x1 = x.astype(BF16)
    r1 = x - x1.astype(F32)
    x2 = r1.astype(BF16)
    x3 = (r1 - x2.astype(F32)).astype(BF16)
    return x1, x2, x3


def _split2(x):
    x1 = x.astype(BF16)
    x2 = (x - x1.astype(F32)).astype(BF16)
    return x1, x2


def _cumsum_cols(tril, a):
    a1, a2, a3 = _split3(a)
    return _dot(tril, a1) + _dot(tril, a2) + _dot(tril, a3)


def _cumsum_rows_t(a, triu):
    a1, a2, a3 = _split3(a)
    return _dot_tn(a1, triu) + _dot_tn(a2, triu) + _dot_tn(a3, triu)


def _causal_conv(ext, cw, halo):
    width = cw.shape[0]
    y = ext[halo:] * cw[width - 1:width]
    for k in range(1, width):
        y = y + pltpu.roll(ext, k, 0)[halo:] * cw[width - 1 - k:width - k]
    return y


def _inproj_kernel(x_ref, g_ref, w_ref, *out_refs, splits):
    hn = _rms(x_ref[...], g_ref[...]).astype(BF16)
    y = _dot(hn, w_ref[...])
    off = 0
    for o_ref, n in zip(out_refs, splits):
        o_ref[...] = y[:, off:off + n].astype(o_ref.dtype)
        off += n


def norm_matmul(x2d, g, w, splits, tm=ROW_TILE):
    t = x2d.shape[0]
    n = w.shape[1]
    assert sum(splits) == n and t % tm == 0
    return pl.pallas_call(
        functools.partial(_inproj_kernel, splits=splits),
        grid=(t // tm,),
        in_specs=[pl.BlockSpec((tm, D_MODEL), lambda i: (i, 0)),
                  pl.BlockSpec((1, D_MODEL), lambda i: (0, 0)),
                  pl.BlockSpec((D_MODEL, n), lambda i: (0, 0))],
        out_specs=[pl.BlockSpec((tm, s), lambda i: (i, 0)) for s in splits],
        out_shape=[jax.ShapeDtypeStruct((t, s), F32) for s in splits],
        compiler_params=_cparams(("parallel",)),
        name="norm_matmul",
    )(x2d, g.reshape(1, D_MODEL), w)


FFN_CHUNK = 256


def _ffn_kernel(*refs, n_parts, tm, seq):
    x_ref, halo_ref = refs[:2]
    part_refs = refs[2:2 + n_parts]
    part_halo_refs = refs[2 + n_parts:2 + 2 * n_parts]
    (wo_ref, gmix_ref, gpre_ref, wup_ref, cw_ref, cb_ref, wd_ref, gpost_ref, o_ref, hn_ref) = refs[2 + 2 * n_parts:]

    def after_mixer(x, parts):
        acc = None
        off = 0
        for p_ref in parts:
            k = p_ref.shape[-1]
            part = _dot(p_ref[...], wo_ref[off:off + k, :])
            acc = part if acc is None else acc + part
            off += k
        return x + _rms(acc, gmix_ref[...])

    o_ref[...] = after_mixer(x_ref[...], part_refs)
    keep = ((pl.program_id(0) * tm) % seq != 0).astype(F32)
    hn_ref[:FFN_HALO, :] = (_rms(after_mixer(halo_ref[...], part_halo_refs), gpre_ref[...]) * keep).astype(BF16)
    hn_ref[FFN_HALO:, :] = _rms(o_ref[...], gpre_ref[...]).astype(BF16)
    hn = hn_ref[...]

    def up(lo):
        return _dot(hn, wup_ref[:, lo:lo + FFN_CHUNK]), _dot(hn, wup_ref[:, FFN_DIM + lo:FFN_DIM + lo + FFN_CHUNK])

    def conv(u, lo):
        return _causal_conv(u, cw_ref[:, lo:lo + FFN_CHUNK], FFN_HALO) + cb_ref[:, lo:lo + FFN_CHUNK]

    hidden = []
    n_chunks = FFN_DIM // FFN_CHUNK
    ahead = up(0)
    for c in range(n_chunks):
        lo = c * FFN_CHUNK
        u_gate, u_val = ahead
        if c + 1 < n_chunks:
            ahead = up(lo + FFN_CHUNK)
        hidden.append((_silu(conv(u_gate, lo)) * conv(u_val, FFN_DIM + lo)).astype(BF16))
    down = _dot(jnp.concatenate(hidden, axis=1), wd_ref[...])
    o_ref[...] = o_ref[...] + _rms(down, gpost_ref[...])


def mixer_out_ffn(x2d, parts, w_out, gmix, seq, gpre, w_up, conv_w, conv_b, w_down, gpost, tm=FFN_ROW_TILE):
    t = x2d.shape[0]
    assert FFN_DIM % FFN_CHUNK == 0 and t % tm == 0 and seq % tm == 0 and tm % FFN_HALO == 0
    hb = tm // FFN_HALO

    def resident(a):
        return pl.BlockSpec(a.shape, lambda i: (0,) * a.ndim, pipeline_mode=pl.Buffered(1))

    def halo(width):
        return pl.BlockSpec((FFN_HALO, width), lambda i: (jnp.maximum(i * hb - 1, 0), 0))

    consts = (w_out, gmix.reshape(1, D_MODEL), gpre.reshape(1, D_MODEL), w_up, conv_w,
              conv_b.reshape(1, 2 * FFN_DIM), w_down, gpost.reshape(1, D_MODEL))
    return pl.pallas_call(
        functools.partial(_ffn_kernel, n_parts=len(parts), tm=tm, seq=seq),
        grid=(t // tm,),
        in_specs=[pl.BlockSpec((tm, D_MODEL), lambda i: (i, 0)), halo(D_MODEL)]
        + [pl.BlockSpec((tm, p.shape[1]), lambda i: (i, 0)) for p in parts]
        + [halo(p.shape[1]) for p in parts]
        + [resident(a) for a in consts],
        out_specs=pl.BlockSpec((tm, D_MODEL), lambda i: (i, 0)),
        out_shape=jax.ShapeDtypeStruct((t, D_MODEL), F32),
        scratch_shapes=[pltpu.VMEM((tm + FFN_HALO, D_MODEL), BF16)],
        compiler_params=_cparams(("parallel",)),
        name="mixer_out_ffn",
    )(x2d, x2d, *parts, *parts, *consts)


def _nsa_prep_kernel(q_ref, kv_ref, pos_ref, freq_ref, ecos_ref, esin_ref, rest_ref,
                     qh_ref, kcmp_ref, vcmp_ref, kslc_ref, vslc_ref, kwin_ref, vwin_ref):
    ang = freq_ref[...] * pos_ref[0].astype(F32)

    def spread(t, e_ref):
        t1, t2, t3 = _split3(t)
        e = e_ref[...]
        return _dot_tn(t1, e) + _dot_tn(t2, e) + _dot_tn(t3, e)

    cs = spread(jnp.cos(ang), ecos_ref) + rest_ref[...]
    sn = spread(jnp.sin(ang), esin_ref)
    lane = lax.broadcasted_iota(jnp.int32, (1, LANE), 1) % NSA_DH
    first_half = lane < ROT_DIM // 2

    def rope(x):
        partner = jnp.where(first_half, pltpu.roll(x, LANE - ROT_DIM // 2, 1), pltpu.roll(x, ROT_DIM // 2, 1))
        return x * cs + partner * sn

    ts = q_ref.shape[1]
    aug_w = kslc_ref.shape[-1]
    low_half = lax.broadcasted_iota(jnp.int32, (1, LANE), 1) < NSA_DH

    def widen(x, upper, fill):
        low = jnp.where(low_half, pltpu.roll(x, NSA_DH, 1) if upper else x, fill[:, :LANE])
        return low if aug_w == LANE else jnp.concatenate([low, fill[:, LANE:]], axis=1)

    scale = NSA_DH ** -0.5 * math.log2(math.e)
    for j in range(NSA_HEADS // 2):
        t = (rope(q_ref[0, :, j * LANE:(j + 1) * LANE]) * scale).T
        qh_ref[0, 2 * j] = t[:NSA_DH].astype(BF16)
        qh_ref[0, 2 * j + 1] = t[NSA_DH:].astype(BF16)

    tok = pl.program_id(1) * ts + lax.broadcasted_iota(jnp.int32, (ts, 1), 0)
    block_onehot = jnp.where(lax.broadcasted_iota(jnp.int32, (1, aug_w), 1) - NSA_DH == tok // SLC_BLOCK, 1.0, 0.0)
    t = rope(kv_ref[0, :, 2 * LANE:3 * LANE])
    for g in range(NSA_GROUPS):
        kslc_ref[0, g] = widen(t, g == 1, block_onehot).astype(BF16)

    for i, o_ref in ((0, kcmp_ref), (4, kwin_ref), (1, vcmp_ref)):
        t = kv_ref[0, :, i * LANE:(i + 1) * LANE]
        if i % 2 == 0:
            t = rope(t)
        for g in range(NSA_GROUPS):
            o_ref[0, g] = t[:, g * NSA_DH:(g + 1) * NSA_DH].astype(o_ref.dtype)
    ones_row = jnp.where(lax.broadcasted_iota(jnp.int32, (NSA_VROWS - NSA_DH, ts), 0) == 0, 1.0, 0.0)
    for i, o_ref in ((3, vslc_ref), (5, vwin_ref)):
        t = kv_ref[0, :, i * LANE:(i + 1) * LANE].T
        for g in range(NSA_GROUPS):
            o_ref[0, g] = jnp.concatenate([t[g * NSA_DH:(g + 1) * NSA_DH, :], ones_row], axis=0).astype(o_ref.dtype)


NSA_VROWS = NSA_DH + 16


def _aug_width(seq):
    return -(-(NSA_DH + seq // SLC_BLOCK) // LANE) * LANE


def nsa_prep(q, kv, positions, ts=PREP_TILE):
    b, s, _ = q.shape
    half = ROT_DIM // 2
    freq_rows = 16
    inv_freq = ROPE_THETA ** (-jnp.arange(0, ROT_DIM, 2, dtype=F32) / ROT_DIM)
    freq = jnp.zeros((freq_rows, 1), F32).at[:half, 0].set(inv_freq)
    d = np.arange(LANE) % NSA_DH
    j = np.arange(freq_rows)[:, None]
    e_cos = ((d[None, :] < ROT_DIM) & (d[None, :] % half == j)).astype(np.float32)
    e_sin = e_cos * np.where(d[None, :] < half, -1.0, 1.0)
    rest = (d >= ROT_DIM).astype(np.float32).reshape(1, LANE)
    tables = (jnp.asarray(e_cos, dtype=BF16), jnp.asarray(e_sin, dtype=BF16), jnp.asarray(rest))
    tok_shape = (b, NSA_GROUPS, s, NSA_DH)
    tok_spec = pl.BlockSpec((1, NSA_GROUPS, ts, NSA_DH), lambda bi, i: (bi, 0, i, 0))
    feat_shape = (b, NSA_GROUPS, NSA_VROWS, s)
    feat_spec = pl.BlockSpec((1, NSA_GROUPS, NSA_VROWS, ts), lambda bi, i: (bi, 0, 0, i))
    aug_w = _aug_width(s)
    return pl.pallas_call(
        _nsa_prep_kernel,
        grid=(b, s // ts),
        in_specs=[pl.BlockSpec((1, ts, q.shape[2]), lambda bi, i: (bi, i, 0)),
                  pl.BlockSpec((1, ts, kv.shape[2]), lambda bi, i: (bi, i, 0)),
                  pl.BlockSpec((1, 1, ts), lambda bi, i: (bi, 0, i)),
                  pl.BlockSpec((freq_rows, 1), lambda bi, i: (0, 0))]
        + [pl.BlockSpec(t.shape, lambda bi, i: (0, 0)) for t in tables],
        out_specs=[pl.BlockSpec((1, NSA_HEADS, NSA_DH, ts), lambda bi, i: (bi, 0, 0, i)),
                   tok_spec, tok_spec,
                   pl.BlockSpec((1, NSA_GROUPS, ts, aug_w), lambda bi, i: (bi, 0, i, 0)),
                   feat_spec, tok_spec, feat_spec],
        out_shape=[jax.ShapeDtypeStruct((b, NSA_HEADS, NSA_DH, s), BF16),
                   jax.ShapeDtypeStruct(tok_shape, F32), jax.ShapeDtypeStruct(tok_shape, F32),
                   jax.ShapeDtypeStruct((b, NSA_GROUPS, s, aug_w), BF16), jax.ShapeDtypeStruct(feat_shape, BF16),
                   jax.ShapeDtypeStruct(tok_shape, BF16), jax.ShapeDtypeStruct(feat_shape, BF16)],
        compiler_params=_cparams(("parallel", "parallel")),
        name="nsa_prep",
    )(q, kv, positions.reshape(b, 1, s), freq, *tables)


def _compress_kernel(k_ref, v_ref, pek_ref, pev_ref, kw1_ref, vw1_ref, kb1_ref, vb1_ref, kw2_ref, vw2_ref,
                     kc_ref, vc_ref):
    half = CMP_STRIDE * NSA_DH

    def mlp(t_ref, pe_ref, w1_ref, b1_ref, w2_ref):
        a = t_ref[0, 0]
        n = a.shape[0]
        h_lo = _dot((a + pe_ref[:, :half]).astype(BF16), w1_ref[:half, :])
        h_hi = _dot((a + pe_ref[:, half:]).astype(BF16), w1_ref[half:, :])
        hid = _silu(h_lo + pltpu.roll(h_hi, n - 1, 0) + b1_ref[...])
        return _dot(hid.astype(BF16), w2_ref[...])

    kc_ref[0, 0] = mlp(k_ref, pek_ref, kw1_ref, kb1_ref, kw2_ref).astype(kc_ref.dtype)
    vc_ref[0, 0] = mlp(v_ref, pev_ref, vw1_ref, vb1_ref, vw2_ref).astype(vc_ref.dtype).T


def nsa_compress(kcmp, vcmp, pe_k, pe_v, ck_w1, ck_b1, ck_w2, cv_w1, cv_b1, cv_w2):
    b, g, s, dh = kcmp.shape
    n = s // CMP_STRIDE
    wide = CMP_STRIDE * dh
    assert CMP_BLOCK == 2 * CMP_STRIDE
    kr = kcmp.reshape(b, g, n, wide)
    vr = vcmp.reshape(b, g, n, wide)
    blk = pl.BlockSpec((1, 1, n, wide), lambda bi, gi: (bi, gi, 0, 0))

    def full(a):
        return pl.BlockSpec(a.shape, lambda bi, gi: (0,) * a.ndim)

    args = (pe_k.reshape(1, 2 * wide), pe_v.reshape(1, 2 * wide), ck_w1.astype(BF16), cv_w1.astype(BF16),
            ck_b1.reshape(1, dh), cv_b1.reshape(1, dh), ck_w2.astype(BF16), cv_w2.astype(BF16))
    return pl.pallas_call(
        _compress_kernel,
        grid=(b, g),
        in_specs=[blk, blk] + [full(a) for a in args],
        out_specs=[pl.BlockSpec((1, 1, n, dh), lambda bi, gi: (bi, gi, 0, 0)),
                   pl.BlockSpec((1, 1, dh, n), lambda bi, gi: (bi, gi, 0, 0))],
        out_shape=[jax.ShapeDtypeStruct((b, g, n, dh), BF16), jax.ShapeDtypeStruct((b, g, dh, n), BF16)],
        compiler_params=_cparams(("parallel", "parallel")),
        name="nsa_compress",
    )(kr, vr, *args)


ATT_TQ = 256
ATT_TK = 256
ATT_RING = 4
SEL_SIZES = 4


def _att_kernel(q_ref, kc_ref, vc_ref, ks_ref, vs_ref, kw_ref, vw_ref, gate_ref, ov_ref, o_ref,
                *s_refs, seq):
    tq, tk, rep, dh = ATT_TQ, ATT_TK, NSA_REP, NSA_DH
    cols = rep * tq
    nblk = seq // SLC_BLOCK
    ncmp = kc_ref.shape[2]
    aug_w = ks_ref.shape[-1]
    t0 = pl.program_id(2) * tq
    q = jnp.concatenate([q_ref[0, r] for r in range(rep)], axis=1)
    tpos = t0 + lax.broadcasted_iota(jnp.int32, (1, tq), 1)

    def per_head(x):
        return jnp.concatenate([x] * rep, axis=1)


    band = WINDOW + tq
    start = pl.multiple_of(jnp.maximum(t0 - WINDOW, 0), tq)
    dlt = tpos - (start + lax.broadcasted_iota(jnp.int32, (band, 1), 0))
    bias_w = jnp.where((dlt >= 0) & (dlt < WINDOW), 0.0, MASKED)
    s_w = _dot(kw_ref[0, 0, pl.ds(start, band), :], q) + per_head(bias_w)

    cmp_end = lax.broadcasted_iota(jnp.int32, (ncmp, 1), 0) * CMP_STRIDE + (CMP_BLOCK - 1)
    s_c = _dot(kc_ref[0, 0], q) + per_head(jnp.where(cmp_end <= tpos, 0.0, MASKED))
    p_c = jnp.exp2(s_c - jnp.max(s_c, axis=0, keepdims=True))
    l_c = jnp.sum(p_c, axis=0, keepdims=True)
    p_c = p_c * (per_head(jnp.where(tpos >= CMP_BLOCK - 1, 1.0, 0.0)) / l_c)
    o_c = _dot(vc_ref[0, 0], p_c.astype(BF16))

    p_sum = p_c[:, :tq]
    for r in range(1, rep):
        p_sum = p_sum + p_c[:, r * tq:(r + 1) * tq]
    p_hi, p_lo = _split2(p_sum)
    importance = _dot(ov_ref[...], p_hi) + _dot(ov_ref[...], p_lo)
    cur = tpos // SLC_BLOCK

    def select(n):
        def run():
            blk = lax.broadcasted_iota(jnp.int32, (n, 1), 0)
            forced = (blk == 0) | (blk == cur) | (blk == cur - 1)
            imp = jnp.where(forced, FORCED_SCORE, jnp.where(blk <= cur, importance[:n], -1.0))
            sel = jnp.full((n, tq), MASKED, F32)
            for _ in range(min(SLC_TOPK, nblk)):
                best = jnp.max(imp, axis=0, keepdims=True)
                first = jnp.min(jnp.where(imp == best, blk, n), axis=0, keepdims=True)
                hit = blk == first
                sel = jnp.where(hit, 0.0, sel)
                imp = jnp.where(hit, -jnp.inf, imp)
            return sel if n == nblk else jnp.concatenate([sel, jnp.full((nblk - n, tq), MASKED, F32)], axis=0)
        return run

    blk_step = nblk // SEL_SIZES
    size_class = ((t0 + tq - 1) // SLC_BLOCK) // blk_step
    pick = select(nblk)
    for k in reversed(range(SEL_SIZES - 1)):
        pick = functools.partial(lax.cond, size_class == k, select((k + 1) * blk_step), pick)
    sel = pick()

    mask_rows = jnp.concatenate([sel.astype(BF16), jnp.zeros((aug_w - dh - nblk, tq), BF16)], axis=0)
    q_aug = jnp.concatenate([q, per_head(mask_rows)], axis=0)
    kidx = lax.broadcasted_iota(jnp.int32, (tk, 1), 0)

    def produce(kt, s_ref):
        k0 = pl.multiple_of(kt * tk, tk)
        s = _dot(ks_ref[0, 0, pl.ds(k0, tk), :], q_aug)
        s_ref[...] = s
        return jnp.max(s, axis=0, keepdims=True)

    def consume(kt, s_ref, tile_max, carry, causal):
        m, acc = carry
        k0 = pl.multiple_of(kt * tk, tk)
        v = vs_ref[0, 0, :, pl.ds(k0, tk)]
        s = s_ref[...]
        if causal:
            s = s + per_head(jnp.where(k0 + kidx <= tpos, 0.0, MASKED))
            tile_max = jnp.max(s, axis=0, keepdims=True)
        m_new = jnp.maximum(m, tile_max)
        alpha = jnp.exp2(m - m_new)
        p = jnp.exp2(s - m_new)
        return m_new, alpha * acc + _dot(v, p.astype(BF16))

    nbuf = len(s_refs)

    def slc_trip(j, carry):
        maxes, state = list(carry[:nbuf]), carry[nbuf:]
        for i, s_ref in enumerate(s_refs):
            state = consume(nbuf * j + i, s_ref, maxes[i], state, False)
            maxes[i] = produce(nbuf * (j + 1) + i, s_ref)
        return tuple(maxes) + state

    n_full = t0 // (nbuf * tk)
    init = tuple(produce(i, s_ref) for i, s_ref in enumerate(s_refs)) + (
        jnp.full((1, cols), MASKED, F32), jnp.zeros((NSA_VROWS, cols), F32))

    p_w = jnp.exp2(s_w - jnp.max(s_w, axis=0, keepdims=True))
    o_w = _dot(vw_ref[0, 0, :, pl.ds(start, band)], p_w.astype(BF16))
    o_w = o_w[:dh] / o_w[dh:dh + 1]

    carry = lax.fori_loop(0, n_full, slc_trip, init)
    maxes, state = carry[:nbuf], carry[nbuf:]
    own = (t0 - n_full * nbuf * tk) // tk

    def last_trip(n_before):
        def run(st):
            for i in range(n_before):
                st = consume(nbuf * n_full + i, s_refs[i], maxes[i], st, False)
            return consume(nbuf * n_full + n_before, s_refs[n_before], None, st, True)
        return run

    finish = last_trip(nbuf - 1)
    for n_before in reversed(range(nbuf - 1)):
        finish = functools.partial(lax.cond, own == n_before, last_trip(n_before), finish)
    acc_s = finish(state)[1]
    o_s = acc_s[:dh] / acc_s[dh:dh + 1]

    gates = jax.nn.sigmoid(gate_ref[0]).T
    merged = []
    for r in range(rep):
        sl = slice(r * tq, (r + 1) * tq)
        merged.append(gates[3 * r:3 * r + 1] * o_c[:, sl] + gates[3 * r + 1:3 * r + 2] * o_s[:, sl]
                      + gates[3 * r + 2:3 * r + 3] * o_w[:, sl])
    o_ref[0] = jnp.concatenate(merged, axis=0).T.astype(o_ref.dtype)


def _overlap_matrix(seq):
    ncp = seq // CMP_STRIDE
    n_cmp = (seq - CMP_BLOCK) // CMP_STRIDE + 1
    nblk = seq // SLC_BLOCK
    cs = np.arange(ncp) * CMP_STRIDE
    ss = np.arange(nblk) * SLC_BLOCK
    ov = ((cs[None, :] <= ss[:, None] + SLC_BLOCK - 1) & (cs[None, :] + CMP_BLOCK - 1 >= ss[:, None])
          & (np.arange(ncp)[None, :] < n_cmp))
    return jnp.asarray(ov.astype(np.float32), dtype=BF16)


def nsa_attention(qh, kc, vc, kslc, vslc, kwin, vwin, gates):
    b, _, dh, s = qh.shape
    aug_w = kslc.shape[-1]
    g = NSA_GROUPS
    ncp = kc.shape[2]
    nblk = s // SLC_BLOCK
    assert s % (ATT_RING * ATT_TK) == 0 and s >= WINDOW + ATT_TQ and aug_w == _aug_width(s)
    ov = _overlap_matrix(s)
    tok_spec = pl.BlockSpec((1, 1, s, dh), lambda bi, gi, i: (bi, gi, 0, 0))
    feat_spec = pl.BlockSpec((1, 1, NSA_VROWS, s), lambda bi, gi, i: (bi, gi, 0, 0))
    return pl.pallas_call(
        functools.partial(_att_kernel, seq=s),
        grid=(b, g, s // ATT_TQ),
        in_specs=[pl.BlockSpec((1, NSA_REP, dh, ATT_TQ), lambda bi, gi, i: (bi, gi, 0, i)),
                  pl.BlockSpec((1, 1, ncp, dh), lambda bi, gi, i: (bi, gi, 0, 0)),
                  pl.BlockSpec((1, 1, dh, ncp), lambda bi, gi, i: (bi, gi, 0, 0)),
                  pl.BlockSpec((1, 1, s, aug_w), lambda bi, gi, i: (bi, gi, 0, 0)),
                  feat_spec, tok_spec, feat_spec,
                  pl.BlockSpec((1, ATT_TQ, LANE), lambda bi, gi, i: (bi, i, gi)),
                  pl.BlockSpec((nblk, ncp), lambda bi, gi, i: (0, 0))],
        out_specs=pl.BlockSpec((1, ATT_TQ, NSA_REP * dh), lambda bi, gi, i: (bi, i, gi)),
        out_shape=jax.ShapeDtypeStruct((b, s, NSA_HEADS * dh), BF16),
        scratch_shapes=[pltpu.VMEM((ATT_TK, NSA_REP * ATT_TQ), F32)] * ATT_RING,
        compiler_params=_cparams(("parallel", "parallel", "arbitrary")),
        name="nsa_attention",
    )(qh, kc, vc, kslc, vslc, kwin, vwin, gates, ov)


def _ssd_kernel(z_ref, xbc_ref, dt_ref, cw_ref, cb_ref, dtb_ref, alog_ref, drow_ref, ng_ref, tril_ref, triu_ref,
                o_ref, carry_ref, state_ref, y_ref):
    cl = SSD_CHUNK
    hpg = SSD_HEADS // SSD_GROUPS

    @pl.when(pl.program_id(1) == 0)
    def _():
        carry_ref[...] = jnp.zeros_like(carry_ref)
        state_ref[...] = jnp.zeros_like(state_ref)

    xbc = xbc_ref[0]
    ext = jnp.concatenate([carry_ref[...], xbc], axis=0)
    act = _silu(_causal_conv(ext, cw_ref[...], CONV_HALO) + cb_ref[...])
    carry_ref[...] = xbc[cl - CONV_HALO:, :]

    xs = act[:, :SSD_INNER]
    dt = _softplus(dt_ref[0] + dtb_ref[...])
    a = dt * (-jnp.exp(alog_ref[...]))
    acs_c = _cumsum_cols(tril_ref[...], a)
    acs_r = _cumsum_rows_t(a, triu_ref[...])
    causal = lax.broadcasted_iota(jnp.int32, (cl, cl), 0) >= lax.broadcasted_iota(jnp.int32, (cl, cl), 1)

    for g in range(SSD_GROUPS):
        bg = act[:, SSD_INNER + g * SSD_N:SSD_INNER + (g + 1) * SSD_N].astype(BF16)
        cg = act[:, SSD_INNER + (SSD_GROUPS + g) * SSD_N:SSD_INNER + (SSD_GROUPS + g + 1) * SSD_N].astype(BF16)
        cb = _dot_nt(cg, bg)
        state = state_ref[g]
        y_off = _dot(cg, state.astype(BF16))
        weighted = []
        decay = []
        for hl in range(hpg):
            h = g * hpg + hl
            col = acs_c[:, h:h + 1]
            row = acs_r[h:h + 1, :]
            lmat = jnp.exp(jnp.where(causal, col - row, MASKED))
            xh = xs[:, h * SSD_P:(h + 1) * SSD_P]
            xdt = xh * dt[:, h:h + 1]
            y_diag = _dot((cb * lmat).astype(BF16), xdt.astype(BF16))
            y_ref[:, h * SSD_P:(h + 1) * SSD_P] = (y_diag + y_off[:, hl * SSD_P:(hl + 1) * SSD_P] * jnp.exp(col)
                                                   + drow_ref[:, h * SSD_P:(h + 1) * SSD_P] * xh)
            a_last = acs_c[cl - 1:cl, h:h + 1]
            weighted.append((xdt * jnp.exp(a_last - col)).astype(BF16))
            decay.append(jnp.broadcast_to(jnp.exp(a_last), (1, SSD_P)))
        contrib = _dot_tn(bg, jnp.concatenate(weighted, axis=1))
        state_ref[g] = state * jnp.concatenate(decay, axis=1) + contrib

    y = y_ref[...] * _silu(z_ref[0])
    gw = SSD_INNER // SSD_GROUPS
    for g in range(SSD_GROUPS):
        o_ref[0, :, g * gw:(g + 1) * gw] = _rms(y[:, g * gw:(g + 1) * gw],
                                                 ng_ref[:, g * gw:(g + 1) * gw]).astype(o_ref.dtype)


def _tri(n, upper):
    m = np.triu(np.ones((n, n), np.float32)) if upper else np.tril(np.ones((n, n), np.float32))
    return jnp.asarray(m, dtype=BF16)


def _lane_row(v, offset=0):
    row = jnp.zeros((1, LANE), F32)
    return row.at[0, offset:offset + v.shape[0]].set(v)


def ssd_mixer(z, xbc, dts, conv_w, conv_b, dt_bias, a_log, d_skip, norm_g):
    b, s, _ = z.shape
    cl = SSD_CHUNK
    hpg = SSD_HEADS // SSD_GROUPS

    def full(a):
        return pl.BlockSpec(a.shape, lambda bi, c: (0,) * a.ndim)

    args = (conv_w, conv_b.reshape(1, -1), _lane_row(dt_bias), _lane_row(a_log),
            jnp.repeat(d_skip, SSD_P).reshape(1, SSD_INNER), norm_g.reshape(1, SSD_INNER),
            _tri(cl, False), _tri(cl, True))
    return pl.pallas_call(
        _ssd_kernel,
        grid=(b, s // cl),
        in_specs=[pl.BlockSpec((1, cl, SSD_INNER), lambda bi, c: (bi, c, 0)),
                  pl.BlockSpec((1, cl, SSD_CONV_DIM), lambda bi, c: (bi, c, 0)),
                  pl.BlockSpec((1, cl, LANE), lambda bi, c: (bi, c, 0))] + [full(a) for a in args],
        out_specs=pl.BlockSpec((1, cl, SSD_INNER), lambda bi, c: (bi, c, 0)),
        out_shape=jax.ShapeDtypeStruct((b, s, SSD_INNER), BF16),
        scratch_shapes=[pltpu.VMEM((CONV_HALO, SSD_CONV_DIM), F32),
                        pltpu.VMEM((SSD_GROUPS, SSD_N, hpg * SSD_P), F32),
                        pltpu.VMEM((cl, SSD_INNER), F32)],
        compiler_params=_cparams(("parallel", "arbitrary")),
        name="ssd_mixer",
    )(z, xbc, dts, *args)


GDN_STEP_CHUNKS = 4


def _pair_diag(x):
    cl = x.shape[0]
    left = jnp.where(lax.broadcasted_iota(jnp.int32, x.shape, 1) < cl, 1.0, 0.0).astype(BF16)
    return jnp.concatenate([x * left, x * (1.0 - left).astype(BF16)], axis=0)


def _mm_pairs(a, b_diag):
    return _dot(a[0], b_diag[0]) + _dot(a[0], b_diag[1]) + _dot(a[1], b_diag[0])


def _unit_lower_inverses(a_list, row, col, fillers=()):
    eye = jnp.where(row == col, 1.0, 0.0)
    same16 = (row // 16) == (col // 16)
    same32 = (row // 32) == (col // 32)
    pending = list(fillers)

    def fill(anchor):
        if pending:
            pending.pop(0)(anchor[:1, :1] * 0.0)

    def diag_parts(x):
        hi, lo = _split2(x)
        return (hi, lo), (_pair_diag(hi), _pair_diag(lo))

    n1 = [jnp.where(same16, -a, 0.0) for a in a_list]
    p = [eye + x for x in n1]
    n = [diag_parts(x) for x in n1]
    anchor = p[-1]
    for _ in range(3):
        squares = [_mm_pairs(x, xd) for x, xd in n]
        fill(anchor)
        anchor = squares[-1]
        n = [diag_parts(x) for x in squares]
        updates = [_mm_pairs(_split2(x), yd) for x, (_, yd) in zip(p, n)]
        fill(anchor)
        p = [x + y for x, y in zip(p, updates)]
        anchor = p[-1]
    for level_mask in (jnp.where(same16, 0.0, jnp.where(same32, 1.0, 0.0)), jnp.where(same32, 0.0, 1.0)):
        off = [diag_parts(a * level_mask)[1] for a in a_list]
        ps = [diag_parts(x) for x in p]
        t = [_mm_pairs(x, y) for (x, _), y in zip(ps, off)]
        fill(anchor)
        anchor = t[-1]
        updates = [_mm_pairs(_split2(x), zd) for x, (_, zd) in zip(t, ps)]
        fill(anchor)
        p = [x - y for x, y in zip(p, updates)]
        anchor = p[-1]
    while pending:
        fill(anchor)
    return p


def _gdn_kernel(qkv_ref, z_ref, sm_ref, cw_ref, dtb_ref, alog_ref, ng_ref, tril_ref, triu_ref,
                o_ref, carry_ref, state_ref):
    cl = GDN_CHUNK
    rows = qkv_ref.shape[1]
    heads = range(GDN_HEADS)
    half_heads = GDN_HEADS // 2
    chunks = range(rows // cl)
    pairs = [(c, j) for c in chunks for j in range(half_heads)]
    chains = [(c, h) for c in chunks for h in heads]

    @pl.when(pl.program_id(1) == 0)
    def _():
        carry_ref[...] = jnp.zeros_like(carry_ref)
        state_ref[...] = jnp.zeros_like(state_ref)

    qkv = qkv_ref[0]
    ext = jnp.concatenate([carry_ref[...], qkv], axis=0)
    carry_ref[...] = qkv[rows - CONV_HALO:, :]

    def conv_silu(lo, width, tie=0.0):
        return _silu(_causal_conv(ext[:, lo:lo + width] + tie, cw_ref[:, lo:lo + width], CONV_HALO))

    sm = sm_ref[0]
    beta_all = jax.nn.sigmoid(sm)
    g_all = -jnp.exp(alog_ref[...]) * _softplus(sm + dtb_ref[...])
    row = lax.broadcasted_iota(jnp.int32, (cl, 2 * cl), 0)
    lane = lax.broadcasted_iota(jnp.int32, (cl, 2 * cl), 1)
    first = lane < cl
    col = jnp.where(first, lane, lane - cl)
    incl = row >= col

    def l2n(x):
        return x * lax.rsqrt(jnp.sum(x * x, axis=-1, keepdims=True) + NORM_EPS)

    def chunk(x, c):
        return x[c * cl:(c + 1) * cl]

    act_k = conv_silu(GDN_QK_W, GDN_QK_W)
    kn = [l2n(act_k[:, h * GDN_DK:(h + 1) * GDN_DK]) for h in heads]
    beta = [beta_all[:, h:h + 1] for h in heads]
    kb = [kn[h] * beta[h] for h in heads]
    knb = [x.astype(BF16) for x in kn]
    gcs_c = [_cumsum_cols(tril_ref[...], chunk(g_all, c)) for c in chunks]
    gcs_r = [_cumsum_rows_t(chunk(g_all, c), triu_ref[...]) for c in chunks]
    gc = {(c, h): gcs_c[c][:, GDN_HEADS + h:GDN_HEADS + h + 1] for c, h in chains}
    eg = {k: jnp.exp(g) for k, g in gc.items()}

    def decay(c, j):
        g_col = jnp.where(first, gc[c, 2 * j], gc[c, 2 * j + 1])
        g_row = jnp.concatenate([gcs_r[c][GDN_HEADS + 2 * j + i:GDN_HEADS + 2 * j + i + 1, :] for i in range(2)],
                                axis=1)
        return jnp.exp(jnp.where(incl, g_col - g_row, MASKED))

    dec = {k: decay(*k) for k in pairs}
    zeros_k = jnp.zeros((cl, GDN_DK), BF16)

    def against_keys(x, c, j):
        k0, k1 = chunk(knb[2 * j], c), chunk(knb[2 * j + 1], c)
        return (_dot_nt(chunk(x[2 * j], c).astype(BF16), jnp.concatenate([k0, zeros_k], axis=0))
                + _dot_nt(chunk(x[2 * j + 1], c).astype(BF16), jnp.concatenate([zeros_k, k1], axis=0)))

    a_kk = [jnp.where(row > col, against_keys(kb, c, j) * dec[c, j], 0.0) for c, j in pairs]

    qn, v, a_qk, rhs, q_decayed, k_decayed, z_gate = {}, {}, {}, {}, {}, {}, {}

    def fill_queries(j):
        def run(tie):
            act_q = conv_silu(2 * j * GDN_DK, 2 * GDN_DK, tie)
            for i in range(2):
                qn[2 * j + i] = l2n(act_q[:, i * GDN_DK:(i + 1) * GDN_DK]) * GDN_DK ** -0.5
        return run

    def fill_values(j):
        def run(tie):
            act_v = conv_silu(2 * GDN_QK_W + 2 * j * GDN_DV, 2 * GDN_DV, tie)
            for i in range(2):
                v[2 * j + i] = act_v[:, i * GDN_DV:(i + 1) * GDN_DV]
        return run

    def fill_intra(tie):
        for c, j in pairs:
            a_qk[c, j] = (against_keys(qn, c, j) * (dec[c, j] + tie)).astype(BF16)

    def fill_rhs(tie):
        for c, h in chains:
            rhs[c, h] = jnp.concatenate([chunk(v[h], c) * (chunk(beta[h], c) + tie),
                                         chunk(kb[h], c) * (eg[c, h] + tie)], axis=1).astype(BF16)

    def fill_recurrence_operands(tie):
        for c, h in chains:
            q_decayed[c, h] = chunk(qn[h], c) * (eg[c, h] + tie)
            k_decayed[c, h] = (chunk(kn[h], c) * jnp.exp(gc[c, h][cl - 1:cl, :] - gc[c, h] + tie)).astype(BF16)

    def fill_gate(tie):
        z_gate[0] = _silu(z_ref[0] + tie)

    fillers = ([fill_queries(j) for j in range(half_heads)] + [fill_values(j) for j in range(half_heads)]
               + [fill_intra, fill_rhs, fill_recurrence_operands, fill_gate])
    t_inv = [t.astype(BF16) for t in _unit_lower_inverses(a_kk, row, col, fillers)]

    def stacked(x0, x1):
        z = jnp.zeros_like(x0)
        return jnp.concatenate([jnp.concatenate([x0, z], axis=1), jnp.concatenate([z, x1], axis=1)], axis=0)

    uw = {(c, j): _dot(t, stacked(rhs[c, 2 * j], rhs[c, 2 * j + 1])) for (c, j), t in zip(pairs, t_inv)}
    width = GDN_DV + GDN_DK

    state = [state_ref[h] for h in heads]
    for c in chunks:
        sb = [s.astype(BF16) for s in state]
        u = [uw[c, h // 2][:, (h % 2) * width:(h % 2) * width + GDN_DV] for h in heads]
        w = [uw[c, h // 2][:, (h % 2) * width + GDN_DV:(h % 2 + 1) * width] for h in heads]
        ws = [_dot(jnp.concatenate([w[h], q_decayed[c, h]], axis=0).astype(BF16), sb[h]) for h in heads]
        vb = [(u[h] - ws[h][:cl]).astype(BF16) for h in heads]
        intra = [_dot(a_qk[c, j], stacked(vb[2 * j], vb[2 * j + 1])) for j in range(half_heads)]
        o = [ws[h][cl:] + intra[h // 2][:, (h % 2) * GDN_DV:(h % 2 + 1) * GDN_DV] for h in heads]
        state = [state[h] * jnp.exp(gc[c, h][cl - 1:cl, :]) + _dot_tn(k_decayed[c, h], vb[h]) for h in heads]
        for h in heads:
            out = _rms(o[h], ng_ref[...]) * z_gate[0][c * cl:(c + 1) * cl, h * GDN_DV:(h + 1) * GDN_DV]
            o_ref[0, c * cl:(c + 1) * cl, h * GDN_DV:(h + 1) * GDN_DV] = out.astype(o_ref.dtype)
    for h in heads:
        state_ref[h] = state[h]


def gdn_mixer(qkv, z, sm, conv_w, dt_bias, a_log, norm_g):
    b, s, _ = qkv.shape
    cl = GDN_CHUNK

    def full(a):
        return pl.BlockSpec(a.shape, lambda bi, c: (0,) * a.ndim)

    args = (conv_w, _lane_row(dt_bias, GDN_HEADS), _lane_row(a_log, GDN_HEADS), norm_g.reshape(1, GDN_DV),
            _tri(cl, False), _tri(cl, True))
    rows = GDN_STEP_CHUNKS * cl
    assert s % rows == 0
    return pl.pallas_call(
        _gdn_kernel,
        grid=(b, s // rows),
        in_specs=[pl.BlockSpec((1, rows, GDN_CONV_DIM), lambda bi, c: (bi, c, 0)),
                  pl.BlockSpec((1, rows, GDN_V_W), lambda bi, c: (bi, c, 0)),
                  pl.BlockSpec((1, rows, LANE), lambda bi, c: (bi, c, 0))] + [full(a) for a in args],
        out_specs=pl.BlockSpec((1, rows, GDN_V_W), lambda bi, c: (bi, c, 0)),
        out_shape=jax.ShapeDtypeStruct((b, s, GDN_V_W), BF16),
        scratch_shapes=[pltpu.VMEM((CONV_HALO, GDN_CONV_DIM), F32),
                        pltpu.VMEM((GDN_HEADS, GDN_DK, GDN_DV), F32)],
        compiler_params=_cparams(("parallel", "arbitrary")),
        name="gdn_mixer",
    )(qkv, z, sm, *args)


def _pad_cols(w, width=LANE):
    return jnp.pad(w, ((0, 0), (0, width - w.shape[1])))


HY_Q_W = NSA_HEADS * NSA_DH
HY_KV_W = 6 * NSA_GROUPS * NSA_DH
HY_GATE_W = NSA_HEADS * 3


def hybrid_heads(x2d, b, s, positions, g_pre, w_in, pe_k, pe_v, ck_w1, ck_b1, ck_w2,
                 cv_w1, cv_b1, cv_w2, conv_w, conv_b, dt_bias, a_log, d_skip, norm_g):
    o = 0
    cols = {}
    for name, width in (("q", HY_Q_W), ("kv", HY_KV_W), ("gate", HY_GATE_W), ("z", SSD_INNER),
                        ("xbc", SSD_CONV_DIM), ("dt", SSD_HEADS)):
        cols[name] = w_in[:, o:o + width]
        o += width
    gpg = HY_GATE_W // NSA_GROUPS
    w_cat = jnp.concatenate(
        [cols["q"], cols["kv"], cols["z"], cols["xbc"]]
        + [_pad_cols(cols["gate"][:, g * gpg:(g + 1) * gpg]) for g in range(NSA_GROUPS)]
        + [_pad_cols(cols["dt"])], axis=1).astype(BF16)
    splits = (HY_Q_W, HY_KV_W, SSD_INNER, SSD_CONV_DIM, NSA_GROUPS * LANE, LANE)
    q, kv, z, xbc, gates, dts = norm_matmul(x2d, g_pre, w_cat, splits)

    qh, kcmp, vcmp, kslc, vslc, kwin, vwin = nsa_prep(q.reshape(b, s, -1), kv.reshape(b, s, -1), positions)
    kc, vc = nsa_compress(kcmp, vcmp, pe_k, pe_v, ck_w1, ck_b1, ck_w2, cv_w1, cv_b1, cv_w2)
    o_nsa = nsa_attention(qh, kc, vc, kslc, vslc, kwin, vwin, gates.reshape(b, s, -1))
    y = ssd_mixer(z.reshape(b, s, -1), xbc.reshape(b, s, -1), dts.reshape(b, s, -1),
                  conv_w, conv_b, dt_bias, a_log, d_skip, norm_g)
    return [o_nsa.reshape(b * s, -1), y.reshape(b * s, -1)]


def gdn_heads(x2d, b, s, g_pre, w_in, conv_w, dt_bias, a_log, norm_g):
    w_cat = jnp.concatenate([w_in[:, :GDN_CONV_DIM + GDN_V_W], _pad_cols(w_in[:, GDN_CONV_DIM + GDN_V_W:])],
                            axis=1).astype(BF16)
    qkv, z, sm = norm_matmul(x2d, g_pre, w_cat, (GDN_CONV_DIM, GDN_V_W, LANE))
    o = gdn_mixer(qkv.reshape(b, s, -1), z.reshape(b, s, -1), sm.reshape(b, s, -1), conv_w, dt_bias, a_log, norm_g)
    return [o.reshape(b * s, -1)]


def kernel(x, positions, norm_mix_pre, norm_mix_post, norm_ffn_pre, norm_ffn_post, hy_w_in, hy_w_out, nsa_pe_k, nsa_pe_v, nsa_ck_w1, nsa_ck_b1, nsa_ck_w2, nsa_cv_w1, nsa_cv_b1, nsa_cv_w2, ssd_conv_w, ssd_conv_b, ssd_dt_bias, ssd_a_log, ssd_d, ssd_norm, gdn_w_in, gdn_conv_w, gdn_dt_bias, gdn_a_log, gdn_norm, gdn_w_out, ffn_w_up, ffn_conv_w, ffn_conv_b, ffn_w_down):
    b, s, d = x.shape
    x2d = x.reshape(b * s, d)
    depth = norm_mix_pre.shape[0]
    for layer in range(depth):
        e = layer // 2
        if layer % 2 == 0:
            parts = hybrid_heads(x2d, b, s, positions, norm_mix_pre[layer], hy_w_in[e], nsa_pe_k[e], nsa_pe_v[e],
                                 nsa_ck_w1[e], nsa_ck_b1[e], nsa_ck_w2[e], nsa_cv_w1[e], nsa_cv_b1[e], nsa_cv_w2[e],
                                 ssd_conv_w[e], ssd_conv_b[e], ssd_dt_bias[e], ssd_a_log[e], ssd_d[e], ssd_norm[e])
            w_out = hy_w_out[e]
        else:
            parts = gdn_heads(x2d, b, s, norm_mix_pre[layer], gdn_w_in[e], gdn_conv_w[e], gdn_dt_bias[e],
                              gdn_a_log[e], gdn_norm[e])
            w_out = gdn_w_out[e]
        x2d = mixer_out_ffn(x2d, parts, w_out.astype(BF16), norm_mix_post[layer], s, norm_ffn_pre[layer],
                            ffn_w_up[layer].astype(BF16), ffn_conv_w[layer], ffn_conv_b[layer],
                            ffn_w_down[layer].astype(BF16), norm_ffn_post[layer])
    return x2d.reshape(b, s, d)
```
